```python
import math
import jax, jax.numpy as jnp
from jax import lax
import numpy as np

D_MODEL = 1024
BATCH = 1
SEQ = 16384
DEPTH = 2

N_MIXERS = 2
HEAD_DIM = 64
N_HEADS = D_MODEL // HEAD_DIM
N_RWKV = (DEPTH + 1) // 2
N_MOBA = DEPTH // 2
DECAY_LORA = max(32, int(round(1.8 * D_MODEL ** 0.5 / 32)) * 32)
AAA_LORA = max(32, int(round(1.8 * D_MODEL ** 0.5 / 32)) * 32)
GATE_LORA = max(32, int(round(0.6 * D_MODEL ** 0.8 / 32)) * 32)
GN_EPS = HEAD_DIM * 1e-5
MOBA_BLOCK = 256
MOBA_TOPK = 3
Q_CHUNK = 128
ROPE_THETA = 10000.0
D_FF = -(-8 * D_MODEL // (3 * 256)) * 256
ALPHA = (2 * DEPTH) ** 0.25
BETA = (8 * DEPTH) ** -0.25
LN_EPS = 1e-5

kernel_name = 'rwkv7_moba_deepnorm_hybrid'


def layer_norm(x, g, b):
    xf = x.astype(jnp.float32)
    mu = xf.mean(-1, keepdims=True)
    var = jnp.square(xf - mu).mean(-1, keepdims=True)
    return ((xf - mu) * lax.rsqrt(var + LN_EPS) * g + b).astype(x.dtype)


def swiglu_ffn(x, w_in, w_down):
    gate, up = jnp.split(x @ w_in, 2, axis=-1)
    return (jax.nn.silu(gate) * up) @ w_down


def time_shift(x):
    return jnp.pad(x, ((0, 0), (1, 0), (0, 0)))[:, :-1, :]


def rwkv7_time_mix(x, mu, w_rkv, w0, w1, w2, a0, a1, a2, g1, g2, k_k, k_a, r_k, gn_g, gn_b, w_o):
    B, T, C = x.shape
    H, N = N_HEADS, HEAD_DIM
    f32 = jnp.float32
    xx = time_shift(x) - x
    xr, xw, xk, xv, xa, xg = [x + xx * mu[n] for n in range(6)]
    r, k, v = jnp.einsum('nbtc,ncd->nbtd', jnp.stack([xr, xk, xv]), w_rkv)
    w = -jax.nn.softplus(-(w0 + jnp.tanh(xw @ w1) @ w2)) - 0.5
    a = jax.nn.sigmoid(a0 + (xa @ a1) @ a2)
    g = jax.nn.sigmoid(xg @ g1) @ g2
    heads = lambda t: t.reshape(B, T, H, N).astype(f32)
    kk = heads(k * k_k)
    kk = kk / jnp.maximum(jnp.sqrt(jnp.sum(kk * kk, -1, keepdims=True)), 1e-12)
    k = k * (1 + (a - 1) * k_a)
    rh, kh, vh, ah = heads(r), heads(k), heads(v), heads(a)
    decay = jnp.exp(-jnp.exp(heads(w)))

    def step(S, inp):
        r_t, w_t, k_t, v_t, kk_t, b_t = inp
        sa = jnp.einsum('bhvk,bhk->bhv', S, -kk_t)
        S = S * w_t[:, :, None, :] + sa[..., None] * b_t[:, :, None, :] + v_t[..., None] * k_t[:, :, None, :]
        return S, jnp.einsum('bhvk,bhk->bhv', S, r_t)

    to_time = lambda t: jnp.swapaxes(t, 0, 1)
    S0 = jnp.zeros((B, H, N, N), f32)
    _, y = lax.scan(step, S0, tuple(to_time(t) for t in (rh, decay, kh, vh, kk, kk * ah)))
    y = to_time(y)
    mean = y.mean(-1, keepdims=True)
    var = jnp.square(y - mean).mean(-1, keepdims=True)
    y = ((y - mean) * lax.rsqrt(var + GN_EPS)).reshape(B, T, C) * gn_g + gn_b
    bonus = jnp.sum(rh * kh * r_k, -1, keepdims=True) * vh
    y = (y + bonus.reshape(B, T, C)).astype(x.dtype)
    return (y * g) @ w_o


def rope_tables(T):
    inv = ROPE_THETA ** (-jnp.arange(0, HEAD_DIM, 2, dtype=jnp.float32) / HEAD_DIM)
    ang = jnp.arange(T, dtype=jnp.float32)[:, None] * inv[None, :]
    return jnp.cos(ang), jnp.sin(ang)


def apply_rope(t, cos, sin):
    t1, t2 = jnp.split(t, 2, axis=-1)
    return jnp.concatenate([t1 * cos - t2 * sin, t2 * cos + t1 * sin], axis=-1)


def moba_attention(x, w_qkv, w_o):
    B, T, C = x.shape
    H, Dh = N_HEADS, HEAD_DIM
    f32 = jnp.float32
    qkv = (x @ w_qkv).reshape(B, T, 3, H, Dh)
    q, k, v = [jnp.transpose(qkv[:, :, n], (0, 2, 1, 3)).astype(f32) for n in range(3)]
    cos, sin = rope_tables(T)
    q = apply_rope(q, cos, sin) * (Dh ** -0.5)
    k = apply_rope(k, cos, sin)
    NB = -(-T // MOBA_BLOCK)
    K = min(MOBA_TOPK, NB)
    pad = NB * MOBA_BLOCK - T
    kb = jnp.pad(k, ((0, 0), (0, 0), (0, pad), (0, 0))).reshape(B, H, NB, MOBA_BLOCK, Dh)
    vb = jnp.pad(v, ((0, 0), (0, 0), (0, pad), (0, 0))).reshape(B, H, NB, MOBA_BLOCK, Dh)
    k_mean = kb.mean(axis=3)
    bi = jnp.arange(B)[:, None, None, None]
    hi = jnp.arange(H)[None, :, None, None]
    blk_ids = jnp.arange(NB)
    key_off = jnp.arange(MOBA_BLOCK)
    q_off = jnp.arange(Q_CHUNK)

    def attend_chunk(c):
        q0 = c * Q_CHUNK
        own = q0 // MOBA_BLOCK
        q_c = lax.dynamic_slice_in_dim(q, q0, Q_CHUNK, axis=2)
        gate = jnp.einsum('bhqd,bhnd->bhqn', q_c, k_mean)
        gate = jnp.where(blk_ids < own, gate, -jnp.inf)
        _, sel = lax.top_k(gate, K)
        valid = sel < own
        k_sel = kb[bi, hi, sel]
        v_sel = vb[bi, hi, sel]
        s_sel = jnp.einsum('bhqd,bhqksd->bhqks', q_c, k_sel)
        s_sel = jnp.where(valid[..., None], s_sel, -jnp.inf).reshape(B, H, Q_CHUNK, K * MOBA_BLOCK)
        k_own = lax.dynamic_index_in_dim(kb, own, axis=2, keepdims=False)
        v_own = lax.dynamic_index_in_dim(vb, own, axis=2, keepdims=False)
        s_own = jnp.einsum('bhqd,bhsd->bhqs', q_c, k_own)
        causal = (own * MOBA_BLOCK + key_off)[None, :] <= (q0 + q_off)[:, None]
        s_own = jnp.where(causal, s_own, -jnp.inf)
        p = jax.nn.softmax(jnp.concatenate([s_sel, s_own], axis=-1), axis=-1)
        p_sel = p[..., :K * MOBA_BLOCK].reshape(B, H, Q_CHUNK, K, MOBA_BLOCK)
        p_own = p[..., K * MOBA_BLOCK:]
        return (jnp.einsum('bhqks,bhqksd->bhqd', p_sel, v_sel)
                + jnp.einsum('bhqs,bhsd->bhqd', p_own, v_own))

    out = lax.map(attend_chunk, jnp.arange(T // Q_CHUNK))
    out = jnp.transpose(out, (1, 0, 3, 2, 4)).reshape(B, T, C)
    return out.astype(x.dtype) @ w_o


def setup_inputs(seed: int = 0) -> dict:
    key = jax.random.key(seed)
    ks = iter(jax.random.split(key, 32))
    C, H, N, F = D_MODEL, N_HEADS, HEAD_DIM, D_FF
    nrm = lambda shape, s: jax.random.normal(next(ks), shape, jnp.float32) * s
    return {
        'x': nrm((BATCH, SEQ, C), 1.0),
        'rwkv_mu': jax.random.uniform(next(ks), (N_RWKV, 6, C), jnp.float32),
        'rwkv_w_rkv': nrm((N_RWKV, 3, C, C), C ** -0.5),
        'rwkv_w0': jax.random.uniform(next(ks), (N_RWKV, C), jnp.float32, -5.0, 0.5),
        'rwkv_w1': nrm((N_RWKV, C, DECAY_LORA), C ** -0.5),
        'rwkv_w2': nrm((N_RWKV, DECAY_LORA, C), 0.5 * DECAY_LORA ** -0.5),
        'rwkv_a0': nrm((N_RWKV, C), 0.1),
        'rwkv_a1': nrm((N_RWKV, C, AAA_LORA), C ** -0.5),
        'rwkv_a2': nrm((N_RWKV, AAA_LORA, C), 0.5 * AAA_LORA ** -0.5),
        'rwkv_g1': nrm((N_RWKV, C, GATE_LORA), C ** -0.5),
        'rwkv_g2': nrm((N_RWKV, GATE_LORA, C), GATE_LORA ** -0.5),
        'rwkv_k_k': 0.85 + nrm((N_RWKV, C), 0.05),
        'rwkv_k_a': 1.0 + nrm((N_RWKV, C), 0.05),
        'rwkv_r_k': nrm((N_RWKV, H, N), 0.1),
        'rwkv_gn_g': 1.0 + nrm((N_RWKV, C), 0.05),
        'rwkv_gn_b': nrm((N_RWKV, C), 0.02),
        'rwkv_w_o': nrm((N_RWKV, C, C), BETA * C ** -0.5),
        'moba_w_qkv': nrm((N_MOBA, C, 3 * C), C ** -0.5),
        'moba_w_o': nrm((N_MOBA, C, C), BETA * C ** -0.5),
        'ffn_w_in': nrm((DEPTH, C, 2 * F), C ** -0.5),
        'ffn_w_down': nrm((DEPTH, F, C), BETA * F ** -0.5),
        'ln_mix_g': 1.0 + nrm((DEPTH, C), 0.05),
        'ln_mix_b': nrm((DEPTH, C), 0.02),
        'ln_ffn_g': 1.0 + nrm((DEPTH, C), 0.05),
        'ln_ffn_b': nrm((DEPTH, C), 0.02),
    }


def reference(x, rwkv_mu, rwkv_w_rkv, rwkv_w0, rwkv_w1, rwkv_w2, rwkv_a0, rwkv_a1, rwkv_a2,
              rwkv_g1, rwkv_g2, rwkv_k_k, rwkv_k_a, rwkv_r_k, rwkv_gn_g, rwkv_gn_b, rwkv_w_o,
              moba_w_qkv, moba_w_o, ffn_w_in, ffn_w_down, ln_mix_g, ln_mix_b, ln_ffn_g, ln_ffn_b):
    for i in range(DEPTH):
        j = i // N_MIXERS
        if i % N_MIXERS == 0:
            h = rwkv7_time_mix(x, rwkv_mu[j], rwkv_w_rkv[j], rwkv_w0[j], rwkv_w1[j], rwkv_w2[j],
                               rwkv_a0[j], rwkv_a1[j], rwkv_a2[j], rwkv_g1[j], rwkv_g2[j],
                               rwkv_k_k[j], rwkv_k_a[j], rwkv_r_k[j], rwkv_gn_g[j], rwkv_gn_b[j],
                               rwkv_w_o[j])
        else:
            h = moba_attention(x, moba_w_qkv[j], moba_w_o[j])
        x = layer_norm(ALPHA * x + h, ln_mix_g[i], ln_mix_b[i])
        x = layer_norm(ALPHA * x + swiglu_ffn(x, ffn_w_in[i], ffn_w_down[i]), ln_ffn_g[i], ln_ffn_b[i])
    return x
```

```python
import functools

import jax
import jax.numpy as jnp
from jax import lax
from jax.experimental import pallas as pl
from jax.experimental.pallas import tpu as pltpu

F32 = jnp.float32
BF16 = jnp.bfloat16

HEAD_DIM = 64
DEPTH = 2
ALPHA = (2 * DEPTH) ** 0.25
LN_EPS = 1e-5
GN_EPS = HEAD_DIM * 1e-5
MOBA_BLOCK = 256
MOBA_TOPK = 3
ROPE_THETA = 10000.0

LANES = 128
MXU_DIM = 256
WKV_CHUNK = 64
WKV_GROUP = MXU_DIM
NEG_BIG = -1e30
VMEM_LIMIT = 56 * 1024 * 1024


def _params(*sem):
    return pltpu.CompilerParams(dimension_semantics=sem, vmem_limit_bytes=VMEM_LIMIT)


def _full(shape):
    n = len(shape)
    return pl.BlockSpec(shape, lambda *_: (0,) * n)


def _bdot(a, b):
    return jnp.dot(a.astype(BF16), b.astype(BF16), preferred_element_type=F32)


def _nt_dot(a, b):
    return lax.dot_general(a.astype(BF16), b.astype(BF16), (((1,), (1,)), ((), ())),
                           preferred_element_type=F32)


def _tn_dot(a, b):
    return lax.dot_general(a.astype(BF16), b.astype(BF16), (((0,), (0,)), ((), ())),
                           preferred_element_type=F32)


def _split2(a):
    hi = a.astype(BF16)
    lo = (a - hi.astype(F32)).astype(BF16)
    return hi, lo


def _split3(a):
    hi = a.astype(BF16)
    r1 = a - hi.astype(F32)
    mid = r1.astype(BF16)
    lo = (r1 - mid.astype(F32)).astype(BF16)
    return hi, mid, lo


def _dot_exact_rhs(a, b_exact):
    hi, lo = _split2(a)
    return (jnp.dot(hi, b_exact, preferred_element_type=F32)
            + jnp.dot(lo, b_exact, preferred_element_type=F32))


def _dot_split(a, b):
    ah, al = _split2(a)
    bh, bl = _split2(b)
    return (jnp.dot(ah, bh, preferred_element_type=F32)
            + jnp.dot(ah, bl, preferred_element_type=F32)
            + jnp.dot(al, bh, preferred_element_type=F32))


def _head_sum(a, e, et):
    return _dot_exact_rhs(_dot_exact_rhs(a, e), et)


def _layer_norm(z, g, b):
    mu = jnp.mean(z, axis=-1, keepdims=True)
    d = z - mu
    var = jnp.mean(d * d, axis=-1, keepdims=True)
    return d * lax.rsqrt(var + LN_EPS) * g + b


def _head_indicator(c):
    h = jnp.arange(c)[:, None] // HEAD_DIM
    e = (h == jnp.arange(LANES)[None, :]).astype(BF16)
    return e, e.T


def _rwkv_proj_kernel(x_ref, xp_ref, mu_ref, wrkv_ref, w0_ref, w1_ref, w2_ref, a0_ref, a1_ref, a2_ref,
                      g1_ref, g2_ref, kk_w_ref, ka_w_ref, rk_w_ref, e_ref, et_ref,
                      r_out, lw_out, k_out, v_out, kk_out, b_out, g_out, bonus_out):
    i = pl.program_id(0)
    x = x_ref[...]
    tm = x.shape[0]
    prev_last = jnp.where(i > 0, xp_ref[7:8, :], 0.0)
    row = lax.broadcasted_iota(jnp.int32, x.shape, 0)
    shifted = jnp.where(row == 0, prev_last, pltpu.roll(x, 1, 0))
    xx = shifted - x
    mix = lambda n: x + xx * mu_ref[n:n + 1, :]
    e, et = e_ref[...], et_ref[...]

    r = _bdot(mix(0), wrkv_ref[0])
    k = _bdot(mix(2), wrkv_ref[1])
    v = _bdot(mix(3), wrkv_ref[2])
    zw = w0_ref[...] + _bdot(jnp.tanh(_bdot(mix(1), w1_ref[...])), w2_ref[...])
    u = -zw
    w_log = -(jnp.maximum(u, 0.0) + jnp.log(1.0 + jnp.exp(-jnp.abs(u)))) - 0.5
    a = jax.nn.sigmoid(a0_ref[...] + _bdot(_bdot(mix(4), a1_ref[...]), a2_ref[...]))
    g = _bdot(jax.nn.sigmoid(_bdot(mix(5), g1_ref[...])), g2_ref[...])

    kk = k * kk_w_ref[...]
    norm = jnp.sqrt(_head_sum(kk * kk, e, et))
    kk = kk / jnp.maximum(norm, 1e-12)
    k = k * (1.0 + (a - 1.0) * ka_w_ref[...])

    r_out[...] = r
    lw_out[...] = -jnp.exp(w_log)
    k_out[...] = k
    v_out[...] = v
    kk_out[...] = kk
    b_out[...] = kk * a
    g_out[...] = g
    bonus_out[...] = _head_sum(r * k * rk_w_ref[...], e, et) * v
    del tm


def _rwkv_proj(x, p, e, et, tm):
    t, c = x.shape
    row = lambda: pl.BlockSpec((tm, c), lambda i: (i, 0))
    prev = pl.BlockSpec((8, c), lambda i: (jnp.maximum(i * (tm // 8) - 1, 0), 0))
    ins = [x, x, p['mu'], p['w_rkv'], p['w0'], p['w1'], p['w2'], p['a0'], p['a1'], p['a2'],
           p['g1'], p['g2'], p['k_k'], p['k_a'], p['r_k'], e, et]
    in_specs = [row(), prev] + [_full(a.shape) for a in ins[2:]]
    out = jax.ShapeDtypeStruct((t, c), F32)
    return pl.pallas_call(
        _rwkv_proj_kernel,
        grid=(t // tm,),
        in_specs=in_specs,
        out_specs=[row() for _ in range(8)],
        out_shape=[out] * 8,
        compiler_params=_params("parallel"),
    )(*ins)


def _wkv_kernel(r_ref, lw_ref, k_ref, v_ref, kk_ref, b_ref, y_ref, sv_ref):
    tb, gw = r_ref.shape
    L = WKV_CHUNK
    nh = gw // HEAD_DIM
    n_chunks = tb // L

    @pl.when(pl.program_id(1) == 0)
    def _():
        sv_ref[...] = jnp.zeros_like(sv_ref)

    lane = lax.broadcasted_iota(jnp.int32, (1, gw), 1)
    head_masks = [(lane // HEAD_DIM == h).astype(F32) for h in range(nh)]
    col = lax.broadcasted_iota(jnp.int32, (1, nh * L), 1)
    col_masks = [(col // L == h).astype(F32) for h in range(nh)]
    ri = lax.broadcasted_iota(jnp.int32, (L, nh * L), 0)
    ci = lax.broadcasted_iota(jnp.int32, (L, nh * L), 1) % L
    strict = ci < ri
    incl = ci <= ri
    tri = (lax.broadcasted_iota(jnp.int32, (L, L), 1)
           <= lax.broadcasted_iota(jnp.int32, (L, L), 0)).astype(BF16)
    n = nh * L
    eye = (lax.broadcasted_iota(jnp.int32, (n, n), 0)
           == lax.broadcasted_iota(jnp.int32, (n, n), 1)).astype(F32)
    bd_mask = (lax.broadcasted_iota(jnp.int32, (gw, gw), 0) // HEAD_DIM
               == lax.broadcasted_iota(jnp.int32, (gw, gw), 1) // HEAD_DIM)

    def stack(a, masks):
        return jnp.concatenate([a * m for m in masks], axis=0)

    def chunk(ci_, carry):
        rows = pl.ds(pl.multiple_of(ci_ * L, L), L)
        r, lw, k, v = r_ref[rows, :], lw_ref[rows, :], k_ref[rows, :], v_ref[rows, :]
        kk, b = kk_ref[rows, :], b_ref[rows, :]
        sv = sv_ref[...]

        h1, h2, h3 = _split3(lw)
        cum = (jnp.dot(tri, h1, preferred_element_type=F32)
               + jnp.dot(tri, h2, preferred_element_type=F32)
               + jnp.dot(tri, h3, preferred_element_type=F32))
        cum_l = cum[L - 1:L, :]
        p_in = jnp.exp(cum)
        p_inv = jnp.exp(-cum)
        al = -(kk * jnp.exp(cum - lw))
        bt = b * p_inv
        kt = k * p_inv
        rt = r * p_in
        dec = jnp.exp(cum_l - cum)
        bh = b * dec
        kh = k * dec

        xs = jnp.concatenate([al, rt], axis=0)
        ys = jnp.concatenate([stack(bt, head_masks), stack(kt, head_masks)], axis=0)
        res = _nt_dot(xs, ys)
        a_ab = jnp.where(strict, res[:L, :n], 0.0)
        a_ak = jnp.where(strict, res[:L, n:], 0.0)
        a_rb = jnp.where(incl, res[L:, :n], 0.0)
        a_rk = jnp.where(incl, res[L:, n:], 0.0)

        pw = stack(a_ab, col_masks)
        tinv = eye + pw
        sq = 1
        while 2 * sq < L:
            pw = _dot_split(pw, pw)
            tinv = tinv + _dot_split(tinv, pw)
            sq *= 2
        tcat = tinv[0:L]
        for h in range(1, nh):
            tcat = tcat + tinv[h * L:(h + 1) * L]

        v_st = stack(v, head_masks)
        w = _nt_dot(al, sv) + _bdot(a_ak, v_st)
        u = _bdot(tcat, stack(w, head_masks))
        y = _nt_dot(rt, sv) + _bdot(a_rb, stack(u, head_masks)) + _bdot(a_rk, v_st)
        y_ref[rows, :] = y

        upd = _tn_dot(jnp.concatenate([u, v], axis=0), jnp.concatenate([bh, kh], axis=0))
        sv_ref[...] = sv * jnp.exp(cum_l) + jnp.where(bd_mask, upd, 0.0)
        return carry

    lax.fori_loop(0, n_chunks, chunk, 0)


def _wkv_scan(r, lw, k, v, kk, b, tb):
    t, c = r.shape
    gw = WKV_GROUP
    spec = lambda: pl.BlockSpec((tb, gw), lambda g, i: (i, g))
    return pl.pallas_call(
        _wkv_kernel,
        grid=(c // gw, t // tb),
        in_specs=[spec() for _ in range(6)],
        out_specs=spec(),
        out_shape=jax.ShapeDtypeStruct((t, c), F32),
        scratch_shapes=[pltpu.VMEM((gw, gw), F32)],
        compiler_params=_params("parallel", "arbitrary"),
    )(r, lw, k, v, kk, b)


def _rwkv_out_kernel(y_ref, g_ref, bonus_ref, x_ref, gng_ref, gnb_ref, wo_ref, lng_ref, lnb_ref,
                     e_ref, et_ref, o_ref):
    e, et = e_ref[...], et_ref[...]
    y = y_ref[...]
    mean = _head_sum(y, e, et) * (1.0 / HEAD_DIM)
    d = y - mean
    var = _head_sum(d * d, e, et) * (1.0 / HEAD_DIM)
    yn = d * lax.rsqrt(var + GN_EPS) * gng_ref[...] + gnb_ref[...]
    z = (yn + bonus_ref[...]) * g_ref[...]
    h = _bdot(z, wo_ref[...])
    o_ref[...] = _layer_norm(ALPHA * x_ref[...] + h, lng_ref[...], lnb_ref[...])


def _rwkv_out(y, g, bonus, x, gn_g, gn_b, w_o, ln_g, ln_b, e, et, tm):
    t, c = x.shape
    row = lambda: pl.BlockSpec((tm, c), lambda i: (i, 0))
    consts = [gn_g, gn_b, w_o, ln_g, ln_b, e, et]
    return pl.pallas_call(
        _rwkv_out_kernel,
        grid=(t // tm,),
        in_specs=[row() for _ in range(4)] + [_full(a.shape) for a in consts],
        out_specs=row(),
        out_shape=jax.ShapeDtypeStruct((t, c), F32),
        compiler_params=_params("parallel"),
    )(y, g, bonus, x, *consts)


def _ffn_kernel(x_ref, win_ref, wdown_ref, lng_ref, lnb_ref, o_ref, *, n_split):
    x = x_ref[...]
    xb = x.astype(BF16)
    f = wdown_ref.shape[0]
    fc = f // n_split
    acc = jnp.zeros(x.shape, F32)
    for s in range(n_split):
        gate = jnp.dot(xb, win_ref[:, s * fc:(s + 1) * fc], preferred_element_type=F32)
        up = jnp.dot(xb, win_ref[:, f + s * fc:f + (s + 1) * fc], preferred_element_type=F32)
        act = (gate * jax.nn.sigmoid(gate) * up).astype(BF16)
        acc = acc + jnp.dot(act, wdown_ref[s * fc:(s + 1) * fc, :], preferred_element_type=F32)
    o_ref[...] = _layer_norm(ALPHA * x + acc, lng_ref[...], lnb_ref[...])


def _ffn(x, w_in, w_down, ln_g, ln_b, tm):
    t, c = x.shape
    f = w_down.shape[0]
    n_split = 2 if (f // 2) % LANES == 0 else 1
    row = lambda: pl.BlockSpec((tm, c), lambda i: (i, 0))
    consts = [w_in, w_down, ln_g, ln_b]
    return pl.pallas_call(
        functools.partial(_ffn_kernel, n_split=n_split),
        grid=(t // tm,),
        in_specs=[row()] + [_full(a.shape) for a in consts],
        out_specs=row(),
        out_shape=jax.ShapeDtypeStruct((t, c), F32),
        compiler_params=_params("parallel"),
    )(x, *consts)


def _qkv_kernel(x_ref, wqt_ref, wk_ref, wvt_ref, cos_t_ref, sin_t_ref, cos_k_ref, sin_k_ref,
                qt_out, k_out, vt_out, kmean_out):
    xb = x_ref[...].astype(BF16)
    tm, c = xb.shape
    nh = c // HEAD_DIM
    half = HEAD_DIM // 2

    qt = _nt_dot(wqt_ref[...], xb)
    cos_t, sin_t = cos_t_ref[...], sin_t_ref[...]
    scale = HEAD_DIM ** -0.5
    for h in range(nh):
        q1 = qt[h * HEAD_DIM:h * HEAD_DIM + half]
        q2 = qt[h * HEAD_DIM + half:(h + 1) * HEAD_DIM]
        qt_out[h, 0:half, :] = ((q1 * cos_t - q2 * sin_t) * scale).astype(BF16)
        qt_out[h, half:HEAD_DIM, :] = ((q2 * cos_t + q1 * sin_t) * scale).astype(BF16)

    vt = _nt_dot(wvt_ref[...], xb)
    for h in range(nh):
        vt_out[h] = vt[h * HEAD_DIM:(h + 1) * HEAD_DIM].astype(BF16)

    k = jnp.dot(xb, wk_ref[...], preferred_element_type=F32)
    reps = c // cos_k_ref.shape[1]
    cos_k = jnp.concatenate([cos_k_ref[...]] * reps, axis=1)
    sin_k = jnp.concatenate([sin_k_ref[...]] * reps, axis=1)
    lane = lax.broadcasted_iota(jnp.int32, k.shape, 1)
    first = (lane % HEAD_DIM) < half
    rot = jnp.where(first, pltpu.roll(k, c - half, 1), pltpu.roll(k, half, 1))
    k = k * cos_k + rot * sin_k
    for h in range(nh):
        k_out[h] = k[:, h * HEAD_DIM:(h + 1) * HEAD_DIM].astype(BF16)
    for blk in range(tm // MOBA_BLOCK):
        kmean_out[blk] = jnp.mean(k[blk * MOBA_BLOCK:(blk + 1) * MOBA_BLOCK], axis=0, keepdims=True)


def _qkv_rope(x, wq_t, wk, wv_t, cos_t, sin_t, cos_k, sin_k, tm):
    t, c = x.shape
    nh = c // HEAD_DIM
    half = HEAD_DIM // 2
    nb = t // MOBA_BLOCK
    return pl.pallas_call(
        _qkv_kernel,
        grid=(t // tm,),
        in_specs=[pl.BlockSpec((tm, c), lambda i: (i, 0)),
                  _full(wq_t.shape), _full(wk.shape), _full(wv_t.shape),
                  pl.BlockSpec((half, tm), lambda i: (0, i)),
                  pl.BlockSpec((half, tm), lambda i: (0, i)),
                  pl.BlockSpec((tm, cos_k.shape[1]), lambda i: (i, 0)),
                  pl.BlockSpec((tm, sin_k.shape[1]), lambda i: (i, 0))],
        out_specs=[pl.BlockSpec((nh, HEAD_DIM, tm), lambda i: (0, 0, i)),
                   pl.BlockSpec((nh, tm, HEAD_DIM), lambda i: (0, i, 0)),
                   pl.BlockSpec((nh, HEAD_DIM, tm), lambda i: (0, 0, i)),
                   pl.BlockSpec((tm // MOBA_BLOCK, 1, c), lambda i: (i, 0, 0))],
        out_shape=[jax.ShapeDtypeStruct((nh, HEAD_DIM, t), BF16),
                   jax.ShapeDtypeStruct((nh, t, HEAD_DIM), BF16),
                   jax.ShapeDtypeStruct((nh, HEAD_DIM, t), BF16),
                   jax.ShapeDtypeStruct((nb, 1, c), F32)],
        compiler_params=_params("parallel"),
    )(x, wq_t, wk, wv_t, cos_t, sin_t, cos_k, sin_k)


def _moba_kernel(qt_ref, k_ref, vt_ref, kmean_ref, o_ref, sel_ref):
    i = pl.program_id(1)
    qt = qt_ref[0]
    nb = kmean_ref.shape[1]
    bq = qt.shape[1]
    blk = MOBA_BLOCK

    km_hi, km_lo = _split2(kmean_ref[0])
    gate = (jnp.dot(km_hi, qt, preferred_element_type=F32)
            + jnp.dot(km_lo, qt, preferred_element_type=F32))
    bidx = lax.broadcasted_iota(jnp.int32, gate.shape, 0)
    gate = jnp.where(bidx < i, gate, -jnp.inf)
    sel = jnp.zeros(gate.shape, F32)
    for _ in range(MOBA_TOPK):
        m = jnp.max(gate, axis=0, keepdims=True)
        is_m = jnp.logical_and(gate == m, m > -jnp.inf)
        first = jnp.min(jnp.where(is_m, bidx, nb), axis=0, keepdims=True)
        pick = bidx == first
        sel = jnp.where(pick, 1.0, sel)
        gate = jnp.where(pick, -jnp.inf, gate)
    sel_ref[...] = sel

    ones = jnp.ones((16, blk), BF16)

    def attend(j, carry, causal):
        m, acc = carry
        koff = pl.multiple_of(j * blk, blk)
        s = jnp.dot(k_ref[0, pl.ds(koff, blk), :], qt, preferred_element_type=F32)
        if causal:
            key = lax.broadcasted_iota(jnp.int32, s.shape, 0)
            qry = lax.broadcasted_iota(jnp.int32, s.shape, 1)
            s = jnp.where(key <= qry, s, -jnp.inf)
            cmax = jnp.max(s, axis=0, keepdims=True)
            m_new = jnp.maximum(m, cmax)
            m_p = m_new
        else:
            on = sel_ref[pl.ds(j, 1), :] > 0.0
            cmax = jnp.max(s, axis=0, keepdims=True)
            m_new = jnp.where(on, jnp.maximum(m, cmax), m)
            m_p = jnp.where(on, m_new, cmax)
        p = jnp.exp(s - m_p).astype(BF16)
        v_ext = jnp.concatenate([vt_ref[0, :, pl.ds(koff, blk)], ones], axis=0)
        o = jnp.dot(v_ext, p, preferred_element_type=F32)
        if not causal:
            o = jnp.where(on, o, 0.0)
        return m_new, acc * jnp.exp(m - m_new) + o

    m0 = jnp.full((1, bq), NEG_BIG, F32)
    acc0 = jnp.zeros((HEAD_DIM + 16, bq), F32)
    carry = lax.fori_loop(0, i, lambda j, c: attend(j, c, False), (m0, acc0))
    _, acc = attend(i, carry, True)
    o_ref[0] = acc[:HEAD_DIM] / acc[HEAD_DIM:HEAD_DIM + 1]


def _moba_attention(qt, k, vt, kmean):
    nh, dh, t = qt.shape
    nb = t // MOBA_BLOCK
    return pl.pallas_call(
        _moba_kernel,
        grid=(nh, nb),
        in_specs=[pl.BlockSpec((1, dh, MOBA_BLOCK), lambda h, i: (h, 0, i)),
                  pl.BlockSpec((1, t, dh), lambda h, i: (h, 0, 0)),
                  pl.BlockSpec((1, dh, t), lambda h, i: (h, 0, 0)),
                  pl.BlockSpec((1, nb, dh), lambda h, i: (h, 0, 0))],
        out_specs=pl.BlockSpec((1, dh, MOBA_BLOCK), lambda h, i: (h, 0, i)),
        out_shape=jax.ShapeDtypeStruct((nh, dh, t), F32),
        scratch_shapes=[pltpu.VMEM((nb, MOBA_BLOCK), F32)],
        compiler_params=_params("parallel", "arbitrary"),
    )(qt, k, vt, kmean)


def _attn_out_kernel(at_ref, x_ref, wo_ref, lng_ref, lnb_ref, o_ref):
    h = _tn_dot(at_ref[...], wo_ref[...])
    o_ref[...] = _layer_norm(ALPHA * x_ref[...] + h, lng_ref[...], lnb_ref[...])


def _attn_out(at, x, w_o, ln_g, ln_b, tm):
    t, c = x.shape
    consts = [w_o, ln_g, ln_b]
    return pl.pallas_call(
        _attn_out_kernel,
        grid=(t // tm,),
        in_specs=[pl.BlockSpec((c, tm), lambda i: (0, i)),
                  pl.BlockSpec((tm, c), lambda i: (i, 0))] + [_full(a.shape) for a in consts],
        out_specs=pl.BlockSpec((tm, c), lambda i: (i, 0)),
        out_shape=jax.ShapeDtypeStruct((t, c), F32),
        compiler_params=_params("parallel"),
    )(at, x, *consts)


def _row(a):
    return a.reshape(1, -1).astype(F32)


def _rwkv_layer(x, p, ln_g, ln_b, e, et, tm):
    r, lw, k, v, kk, b, g, bonus = _rwkv_proj(x, p, e, et, tm)
    y = _wkv_scan(r, lw, k, v, kk, b, tb=min(256, x.shape[0]))
    return _rwkv_out(y, g, bonus, x, p['gn_g'], p['gn_b'], p['w_o'], ln_g, ln_b, e, et, tm)


def _rope_tables(t):
    half = HEAD_DIM // 2
    inv = ROPE_THETA ** (-jnp.arange(0, HEAD_DIM, 2, dtype=F32) / HEAD_DIM)
    ang = jnp.arange(t, dtype=F32)[:, None] * inv[None, :]
    cos, sin = jnp.cos(ang), jnp.sin(ang)
    reps = LANES // HEAD_DIM
    cos_k = jnp.tile(jnp.concatenate([cos, cos], axis=1), (1, reps))
    sin_k = jnp.tile(jnp.concatenate([-sin, sin], axis=1), (1, reps))
    del half
    return cos.T, sin.T, cos_k, sin_k


def _moba_layer(x, w_qkv, w_o, ln_g, ln_b, tm):
    t, c = x.shape
    nh = c // HEAD_DIM
    wq_t = w_qkv[:, :c].T.astype(BF16)
    wk = w_qkv[:, c:2 * c].astype(BF16)
    wv_t = w_qkv[:, 2 * c:].T.astype(BF16)
    cos_t, sin_t, cos_k, sin_k = _rope_tables(t)
    qt, k, vt, kmean = _qkv_rope(x, wq_t, wk, wv_t, cos_t, sin_t, cos_k, sin_k, tm)
    kmean = jnp.transpose(kmean.reshape(t // MOBA_BLOCK, nh, HEAD_DIM), (1, 0, 2))
    at = _moba_attention(qt, k, vt, kmean)
    return _attn_out(at.reshape(c, t), x, w_o.astype(BF16), ln_g, ln_b, tm)


def kernel(x, rwkv_mu, rwkv_w_rkv, rwkv_w0, rwkv_w1, rwkv_w2, rwkv_a0, rwkv_a1, rwkv_a2, rwkv_g1, rwkv_g2,
           rwkv_k_k, rwkv_k_a, rwkv_r_k, rwkv_gn_g, rwkv_gn_b, rwkv_w_o, moba_w_qkv, moba_w_o,
           ffn_w_in, ffn_w_down, ln_mix_g, ln_mix_b, ln_ffn_g, ln_ffn_b):
    bsz, t, c = x.shape
    assert c % WKV_GROUP == 0 and t % MOBA_BLOCK == 0
    tm = min(512, t)
    e, et = _head_indicator(c)
    outs = []
    for bi in range(bsz):
        h = x[bi]
        for i in range(DEPTH):
            j = i // 2
            if i % 2 == 0:
                p = dict(mu=rwkv_mu[j], w_rkv=rwkv_w_rkv[j].astype(BF16), w0=_row(rwkv_w0[j]),
                         w1=rwkv_w1[j].astype(BF16), w2=rwkv_w2[j].astype(BF16), a0=_row(rwkv_a0[j]),
                         a1=rwkv_a1[j].astype(BF16), a2=rwkv_a2[j].astype(BF16),
                         g1=rwkv_g1[j].astype(BF16), g2=rwkv_g2[j].astype(BF16),
                         k_k=_row(rwkv_k_k[j]), k_a=_row(rwkv_k_a[j]), r_k=_row(rwkv_r_k[j]),
                         gn_g=_row(rwkv_gn_g[j]), gn_b=_row(rwkv_gn_b[j]), w_o=rwkv_w_o[j].astype(BF16))
                h = _rwkv_layer(h, p, _row(ln_mix_g[i]), _row(ln_mix_b[i]), e, et, tm)
            else:
                h = _moba_layer(h, moba_w_qkv[j], moba_w_o[j], _row(ln_mix_g[i]), _row(ln_mix_b[i]), tm)
            h = _ffn(h, ffn_w_in[i].astype(BF16), ffn_w_down[i].astype(BF16),
                     _row(ln_ffn_g[i]), _row(ln_ffn_b[i]), tm)
        outs.append(h)
    return jnp.stack(outs, axis=0)
```

```python
import functools

import jax
import jax.numpy as jnp
from jax import lax
from jax.experimental import pallas as pl
from jax.experimental.pallas import tpu as pltpu

F32 = jnp.float32
BF16 = jnp.bfloat16

HEAD_DIM = 64
DEPTH = 2
ALPHA = (2 * DEPTH) ** 0.25
LN_EPS = 1e-5
GN_EPS = HEAD_DIM * 1e-5
MOBA_BLOCK = 256
MOBA_TOPK = 3
ROPE_THETA = 10000.0
LOG2_E = 1.4426950408889634
MOBA_HEADS_PER_STEP = 4

LANES = 128
MXU_DIM = 256
WKV_CHUNK = 64
WKV_GROUP = MXU_DIM
NEG_BIG = -1e30
VMEM_LIMIT = 56 * 1024 * 1024


def _params(*sem):
    return pltpu.CompilerParams(dimension_semantics=sem, vmem_limit_bytes=VMEM_LIMIT)


def _full(shape):
    n = len(shape)
    return pl.BlockSpec(shape, lambda *_: (0,) * n)


def _bdot(a, b):
    return jnp.dot(a.astype(BF16), b.astype(BF16), preferred_element_type=F32)


def _nt_dot(a, b):
    return lax.dot_general(a.astype(BF16), b.astype(BF16), (((1,), (1,)), ((), ())),
                           preferred_element_type=F32)


def _tn_dot(a, b):
    return lax.dot_general(a.astype(BF16), b.astype(BF16), (((0,), (0,)), ((), ())),
                           preferred_element_type=F32)


def _split2(a):
    hi = a.astype(BF16)
    lo = (a - hi.astype(F32)).astype(BF16)
    return hi, lo


def _split3(a):
    hi = a.astype(BF16)
    r1 = a - hi.astype(F32)
    mid = r1.astype(BF16)
    lo = (r1 - mid.astype(F32)).astype(BF16)
    return hi, mid, lo


def _dot_exact_rhs(a, b_exact):
    hi, lo = _split2(a)
    return (jnp.dot(hi, b_exact, preferred_element_type=F32)
            + jnp.dot(lo, b_exact, preferred_element_type=F32))


def _dot_split(a, b):
    ah, al = _split2(a)
    bh, bl = _split2(b)
    return (jnp.dot(ah, bh, preferred_element_type=F32)
            + jnp.dot(ah, bl, preferred_element_type=F32)
            + jnp.dot(al, bh, preferred_element_type=F32))


def _head_sum(a, e, et):
    return _dot_exact_rhs(_dot_exact_rhs(a, e), et)


def _layer_norm(z, g, b):
    mu = jnp.mean(z, axis=-1, keepdims=True)
    d = z - mu
    var = jnp.mean(d * d, axis=-1, keepdims=True)
    return d * lax.rsqrt(var + LN_EPS) * g + b


def _head_indicator(c):
    h = jnp.arange(c)[:, None] // HEAD_DIM
    e = (h == jnp.arange(LANES)[None, :]).astype(BF16)
    return e, e.T


def _rwkv_proj_kernel(x_ref, xp_ref, mu_ref, wrkv_ref, w0_ref, w1_ref, w2_ref, a0_ref, a1_ref, a2_ref,
                      g1_ref, g2_ref, kk_w_ref, ka_w_ref, rk_w_ref, e_ref, et_ref,
                      r_out, lw_out, k_out, v_out, kk_out, b_out, g_out, bonus_out):
    i = pl.program_id(0)
    x = x_ref[...]
    tm = x.shape[0]
    prev_last = jnp.where(i > 0, xp_ref[7:8, :], 0.0)
    row = lax.broadcasted_iota(jnp.int32, x.shape, 0)
    shifted = jnp.where(row == 0, prev_last, pltpu.roll(x, 1, 0))
    xx = shifted - x
    mix = lambda n: x + xx * mu_ref[n:n + 1, :]
    e, et = e_ref[...], et_ref[...]

    r = _bdot(mix(0), wrkv_ref[0])
    k = _bdot(mix(2), wrkv_ref[1])
    v = _bdot(mix(3), wrkv_ref[2])
    zw = w0_ref[...] + _bdot(jnp.tanh(_bdot(mix(1), w1_ref[...])), w2_ref[...])
    u = -zw
    w_log = -(jnp.maximum(u, 0.0) + jnp.log(1.0 + jnp.exp(-jnp.abs(u)))) - 0.5
    a = jax.nn.sigmoid(a0_ref[...] + _bdot(_bdot(mix(4), a1_ref[...]), a2_ref[...]))
    g = _bdot(jax.nn.sigmoid(_bdot(mix(5), g1_ref[...])), g2_ref[...])

    kk = k * kk_w_ref[...]
    norm = jnp.sqrt(_head_sum(kk * kk, e, et))
    kk = kk / jnp.maximum(norm, 1e-12)
    k = k * (1.0 + (a - 1.0) * ka_w_ref[...])

    r_out[...] = r
    lw_out[...] = -jnp.exp(w_log)
    k_out[...] = k
    v_out[...] = v
    kk_out[...] = kk
    b_out[...] = kk * a
    g_out[...] = g
    bonus_out[...] = _head_sum(r * k * rk_w_ref[...], e, et) * v
    del tm


def _rwkv_proj(x, p, e, et, tm):
    t, c = x.shape
    row = lambda: pl.BlockSpec((tm, c), lambda i: (i, 0))
    prev = pl.BlockSpec((8, c), lambda i: (jnp.maximum(i * (tm // 8) - 1, 0), 0))
    ins = [x, x, p['mu'], p['w_rkv'], p['w0'], p['w1'], p['w2'], p['a0'], p['a1'], p['a2'],
           p['g1'], p['g2'], p['k_k'], p['k_a'], p['r_k'], e, et]
    in_specs = [row(), prev] + [_full(a.shape) for a in ins[2:]]
    out = jax.ShapeDtypeStruct((t, c), F32)
    return pl.pallas_call(
        _rwkv_proj_kernel,
        grid=(t // tm,),
        in_specs=in_specs,
        out_specs=[row() for _ in range(8)],
        out_shape=[out] * 8,
        compiler_params=_params("parallel"),
    )(*ins)


def _wkv_kernel(r_ref, lw_ref, k_ref, v_ref, kk_ref, b_ref, y_ref, sv_ref):
    tb, gw = r_ref.shape
    L = WKV_CHUNK
    nh = gw // HEAD_DIM
    n_chunks = tb // L

    @pl.when(pl.program_id(1) == 0)
    def _():
        sv_ref[...] = jnp.zeros_like(sv_ref)

    lane = lax.broadcasted_iota(jnp.int32, (1, gw), 1)
    head_masks = [(lane // HEAD_DIM == h).astype(F32) for h in range(nh)]
    col = lax.broadcasted_iota(jnp.int32, (1, nh * L), 1)
    col_masks = [(col // L == h).astype(F32) for h in range(nh)]
    ri = lax.broadcasted_iota(jnp.int32, (L, nh * L), 0)
    ci = lax.broadcasted_iota(jnp.int32, (L, nh * L), 1) % L
    strict = ci < ri
    incl = ci <= ri
    tri = (lax.broadcasted_iota(jnp.int32, (L, L), 1)
           <= lax.broadcasted_iota(jnp.int32, (L, L), 0)).astype(BF16)
    n = nh * L
    eye = (lax.broadcasted_iota(jnp.int32, (n, n), 0)
           == lax.broadcasted_iota(jnp.int32, (n, n), 1)).astype(F32)
    bd_mask = (lax.broadcasted_iota(jnp.int32, (gw, gw), 0) // HEAD_DIM
               == lax.broadcasted_iota(jnp.int32, (gw, gw), 1) // HEAD_DIM)

    def stack(a, masks):
        return jnp.concatenate([a * m for m in masks], axis=0)

    def chunk(ci_, carry):
        rows = pl.ds(pl.multiple_of(ci_ * L, L), L)
        r, lw, k, v = r_ref[rows, :], lw_ref[rows, :], k_ref[rows, :], v_ref[rows, :]
        kk, b = kk_ref[rows, :], b_ref[rows, :]
        sv = sv_ref[...]

        h1, h2, h3 = _split3(lw)
        cum = (jnp.dot(tri, h1, preferred_element_type=F32)
               + jnp.dot(tri, h2, preferred_element_type=F32)
               + jnp.dot(tri, h3, preferred_element_type=F32))
        cum_l = cum[L - 1:L, :]
        p_in = jnp.exp(cum)
        p_inv = jnp.exp(-cum)
        al = -(kk * jnp.exp(cum - lw))
        bt = b * p_inv
        kt = k * p_inv
        rt = r * p_in
        dec = jnp.exp(cum_l - cum)
        bh = b * dec
        kh = k * dec

        xs = jnp.concatenate([al, rt], axis=0)
        ys = jnp.concatenate([stack(bt, head_masks), stack(kt, head_masks)], axis=0)
        res = _nt_dot(xs, ys)
        a_ab = jnp.where(strict, res[:L, :n], 0.0)
        a_ak = jnp.where(strict, res[:L, n:], 0.0)
        a_rb = jnp.where(incl, res[L:, :n], 0.0)
        a_rk = jnp.where(incl, res[L:, n:], 0.0)

        pw = stack(a_ab, col_masks)
        tinv = eye + pw
        sq = 1
        while 2 * sq < L:
            pw = _dot_split(pw, pw)
            tinv = tinv + _dot_split(tinv, pw)
            sq *= 2
        tcat = tinv[0:L]
        for h in range(1, nh):
            tcat = tcat + tinv[h * L:(h + 1) * L]

        v_st = stack(v, head_masks)
        w = _nt_dot(al, sv) + _bdot(a_ak, v_st)
        u = _bdot(tcat, stack(w, head_masks))
        y = _nt_dot(rt, sv) + _bdot(a_rb, stack(u, head_masks)) + _bdot(a_rk, v_st)
        y_ref[rows, :] = y

        upd = _tn_dot(jnp.concatenate([u, v], axis=0), jnp.concatenate([bh, kh], axis=0))
        sv_ref[...] = sv * jnp.exp(cum_l) + jnp.where(bd_mask, upd, 0.0)
        return carry

    lax.fori_loop(0, n_chunks, chunk, 0)


def _wkv_scan(r, lw, k, v, kk, b, tb):
    t, c = r.shape
    gw = WKV_GROUP
    spec = lambda: pl.BlockSpec((tb, gw), lambda g, i: (i, g))
    return pl.pallas_call(
        _wkv_kernel,
        grid=(c // gw, t // tb),
        in_specs=[spec() for _ in range(6)],
        out_specs=spec(),
        out_shape=jax.ShapeDtypeStruct((t, c), F32),
        scratch_shapes=[pltpu.VMEM((gw, gw), F32)],
        compiler_params=_params("parallel", "arbitrary"),
    )(r, lw, k, v, kk, b)


def _rwkv_out_kernel(y_ref, g_ref, bonus_ref, x_ref, gng_ref, gnb_ref, wo_ref, lng_ref, lnb_ref,
                     e_ref, et_ref, o_ref):
    e, et = e_ref[...], et_ref[...]
    y = y_ref[...]
    mean = _head_sum(y, e, et) * (1.0 / HEAD_DIM)
    d = y - mean
    var = _head_sum(d * d, e, et) * (1.0 / HEAD_DIM)
    yn = d * lax.rsqrt(var + GN_EPS) * gng_ref[...] + gnb_ref[...]
    z = (yn + bonus_ref[...]) * g_ref[...]
    h = _bdot(z, wo_ref[...])
    o_ref[...] = _layer_norm(ALPHA * x_ref[...] + h, lng_ref[...], lnb_ref[...])


def _rwkv_out(y, g, bonus, x, gn_g, gn_b, w_o, ln_g, ln_b, e, et, tm):
    t, c = x.shape
    row = lambda: pl.BlockSpec((tm, c), lambda i: (i, 0))
    consts = [gn_g, gn_b, w_o, ln_g, ln_b, e, et]
    return pl.pallas_call(
        _rwkv_out_kernel,
        grid=(t // tm,),
        in_specs=[row() for _ in range(4)] + [_full(a.shape) for a in consts],
        out_specs=row(),
        out_shape=jax.ShapeDtypeStruct((t, c), F32),
        compiler_params=_params("parallel"),
    )(y, g, bonus, x, *consts)


def _ffn_kernel(x_ref, win_ref, wdown_ref, lng_ref, lnb_ref, o_ref, *, n_split):
    x = x_ref[...]
    xb = x.astype(BF16)
    f = wdown_ref.shape[0]
    fc = f // n_split
    acc = jnp.zeros(x.shape, F32)
    for s in range(n_split):
        gate = jnp.dot(xb, win_ref[:, s * fc:(s + 1) * fc], preferred_element_type=F32)
        up = jnp.dot(xb, win_ref[:, f + s * fc:f + (s + 1) * fc], preferred_element_type=F32)
        act = (gate * jax.nn.sigmoid(gate) * up).astype(BF16)
        acc = acc + jnp.dot(act, wdown_ref[s * fc:(s + 1) * fc, :], preferred_element_type=F32)
    o_ref[...] = _layer_norm(ALPHA * x + acc, lng_ref[...], lnb_ref[...])


def _ffn(x, w_in, w_down, ln_g, ln_b, tm):
    t, c = x.shape
    f = w_down.shape[0]
    n_split = 2 if (f // 2) % LANES == 0 else 1
    row = lambda: pl.BlockSpec((tm, c), lambda i: (i, 0))
    consts = [w_in, w_down, ln_g, ln_b]
    return pl.pallas_call(
        functools.partial(_ffn_kernel, n_split=n_split),
        grid=(t // tm,),
        in_specs=[row()] + [_full(a.shape) for a in consts],
        out_specs=row(),
        out_shape=jax.ShapeDtypeStruct((t, c), F32),
        compiler_params=_params("parallel"),
    )(x, *consts)


def _qkv_kernel(x_ref, wqt_ref, wk_ref, wvt_ref, cos_t_ref, sin_t_ref, cos_k_ref, sin_k_ref,
                qt_out, k_out, vt_out, kmean_out):
    xb = x_ref[...].astype(BF16)
    tm, c = xb.shape
    nh = c // HEAD_DIM
    half = HEAD_DIM // 2

    qt = _nt_dot(wqt_ref[...], xb)
    cos_t, sin_t = cos_t_ref[...], sin_t_ref[...]
    scale = HEAD_DIM ** -0.5 * LOG2_E
    for h in range(nh):
        q1 = qt[h * HEAD_DIM:h * HEAD_DIM + half]
        q2 = qt[h * HEAD_DIM + half:(h + 1) * HEAD_DIM]
        qt_out[h, 0:half, :] = ((q1 * cos_t - q2 * sin_t) * scale).astype(BF16)
        qt_out[h, half:HEAD_DIM, :] = ((q2 * cos_t + q1 * sin_t) * scale).astype(BF16)

    vt = _nt_dot(wvt_ref[...], xb)
    for h in range(nh):
        vt_out[h] = vt[h * HEAD_DIM:(h + 1) * HEAD_DIM].astype(BF16)

    k = jnp.dot(xb, wk_ref[...], preferred_element_type=F32)
    reps = c // cos_k_ref.shape[1]
    cos_k = jnp.concatenate([cos_k_ref[...]] * reps, axis=1)
    sin_k = jnp.concatenate([sin_k_ref[...]] * reps, axis=1)
    lane = lax.broadcasted_iota(jnp.int32, k.shape, 1)
    first = (lane % HEAD_DIM) < half
    rot = jnp.where(first, pltpu.roll(k, c - half, 1), pltpu.roll(k, half, 1))
    k = k * cos_k + rot * sin_k
    for h in range(nh):
        k_out[h] = k[:, h * HEAD_DIM:(h + 1) * HEAD_DIM].astype(BF16)
    for blk in range(tm // MOBA_BLOCK):
        kmean_out[blk] = jnp.mean(k[blk * MOBA_BLOCK:(blk + 1) * MOBA_BLOCK], axis=0, keepdims=True)


def _qkv_rope(x, wq_t, wk, wv_t, cos_t, sin_t, cos_k, sin_k, tm):
    t, c = x.shape
    nh = c // HEAD_DIM
    half = HEAD_DIM // 2
    nb = t // MOBA_BLOCK
    return pl.pallas_call(
        _qkv_kernel,
        grid=(t // tm,),
        in_specs=[pl.BlockSpec((tm, c), lambda i: (i, 0)),
                  _full(wq_t.shape), _full(wk.shape), _full(wv_t.shape),
                  pl.BlockSpec((half, tm), lambda i: (0, i)),
                  pl.BlockSpec((half, tm), lambda i: (0, i)),
                  pl.BlockSpec((tm, cos_k.shape[1]), lambda i: (i, 0)),
                  pl.BlockSpec((tm, sin_k.shape[1]), lambda i: (i, 0))],
        out_specs=[pl.BlockSpec((nh, HEAD_DIM, tm), lambda i: (0, 0, i)),
                   pl.BlockSpec((nh, tm, HEAD_DIM), lambda i: (0, i, 0)),
                   pl.BlockSpec((nh, HEAD_DIM, tm), lambda i: (0, 0, i)),
                   pl.BlockSpec((tm // MOBA_BLOCK, 1, c), lambda i: (i, 0, 0))],
        out_shape=[jax.ShapeDtypeStruct((nh, HEAD_DIM, t), BF16),
                   jax.ShapeDtypeStruct((nh, t, HEAD_DIM), BF16),
                   jax.ShapeDtypeStruct((nh, HEAD_DIM, t), BF16),
                   jax.ShapeDtypeStruct((nb, 1, c), F32)],
        compiler_params=_params("parallel"),
    )(x, wq_t, wk, wv_t, cos_t, sin_t, cos_k, sin_k)


def _moba_kernel(qt_ref, k_ref, vt_ref, kmean_ref, o_ref, sel_ref, s_ref):
    i = pl.program_id(1)
    hb, _, bq = qt_ref.shape
    nb = kmean_ref.shape[1]
    blk = MOBA_BLOCK

    qts = [qt_ref[h] for h in range(hb)]
    bidx = lax.broadcasted_iota(jnp.int32, (nb, bq), 0)
    for h in range(hb):
        km_hi, km_lo = _split2(kmean_ref[h])
        gate = (jnp.dot(km_hi, qts[h], preferred_element_type=F32)
                + jnp.dot(km_lo, qts[h], preferred_element_type=F32))
        gate = jnp.where(bidx < i, gate, -jnp.inf)
        sel = jnp.zeros(gate.shape, F32)
        for _ in range(MOBA_TOPK):
            m = jnp.max(gate, axis=0, keepdims=True)
            is_m = jnp.logical_and(gate == m, m > -jnp.inf)
            first = jnp.min(jnp.where(is_m, bidx, nb), axis=0, keepdims=True)
            pick = bidx == first
            sel = jnp.where(pick, 1.0, sel)
            gate = jnp.where(pick, -jnp.inf, gate)
        sel_ref[h] = sel

    ones = jnp.ones((16, blk), BF16)

    heads = range(hb)

    def put_scores(j, slot):
        koff = pl.multiple_of(j * blk, blk)
        for h in heads:
            s_ref[slot, h] = jnp.dot(k_ref[h, pl.ds(koff, blk), :], qts[h],
                                     preferred_element_type=F32)

    def attend(j, slot, carry, causal):
        koff = pl.multiple_of(j * blk, blk)
        ss = [s_ref[slot, h] for h in heads]
        if causal:
            key = lax.broadcasted_iota(jnp.int32, (blk, bq), 0)
            qry = lax.broadcasted_iota(jnp.int32, (blk, bq), 1)
            ss = [jnp.where(key <= qry, s, -jnp.inf) for s in ss]
        else:
            ons = [sel_ref[h, pl.ds(j, 1), :] > 0.0 for h in heads]
        cmaxs = [jnp.max(s, axis=0, keepdims=True) for s in ss]
        ms = [carry[h][0] for h in heads]
        if causal:
            m_news = [jnp.maximum(ms[h], cmaxs[h]) for h in heads]
            m_ps = m_news
        else:
            m_news = [jnp.where(ons[h], jnp.maximum(ms[h], cmaxs[h]), ms[h]) for h in heads]
            m_ps = [jnp.where(ons[h], m_news[h], cmaxs[h]) for h in heads]
        ps = [jnp.exp2(ss[h] - m_ps[h]).astype(BF16) for h in heads]
        os = [jnp.dot(jnp.concatenate([vt_ref[h, :, pl.ds(koff, blk)], ones], axis=0), ps[h],
                      preferred_element_type=F32) for h in heads]
        if not causal:
            os = [jnp.where(ons[h], os[h], 0.0) for h in heads]
        return tuple((m_news[h], carry[h][1] * jnp.exp2(ms[h] - m_news[h]) + os[h]) for h in heads)

    init = tuple((jnp.full((1, bq), NEG_BIG, F32), jnp.zeros((HEAD_DIM + 16, bq), F32)) for _ in heads)

    def body(jj, carry):
        j0 = 2 * jj
        put_scores(j0 + 1, 1)
        carry = attend(j0, 0, carry, False)
        put_scores(j0 + 2, 0)
        return attend(j0 + 1, 1, carry, False)

    def finish(carry):
        for h in heads:
            acc = carry[h][1]
            o_ref[h] = acc[:HEAD_DIM] / acc[HEAD_DIM:HEAD_DIM + 1]

    put_scores(0, 0)
    carry = lax.fori_loop(0, i // 2, body, init)

    @pl.when(i % 2 == 0)
    def _():
        finish(attend(i, 0, carry, True))

    @pl.when(i % 2 == 1)
    def _():
        put_scores(i, 1)
        finish(attend(i, 1, attend(i - 1, 0, carry, False), True))


def _moba_attention(qt, k, vt, kmean, hb):
    nh, dh, t = qt.shape
    nb = t // MOBA_BLOCK
    once = pl.Buffered(1)
    return pl.pallas_call(
        _moba_kernel,
        grid=(nh // hb, nb),
        in_specs=[pl.BlockSpec((hb, dh, MOBA_BLOCK), lambda g, i: (g, 0, i)),
                  pl.BlockSpec((hb, t, dh), lambda g, i: (g, 0, 0), pipeline_mode=once),
                  pl.BlockSpec((hb, dh, t), lambda g, i: (g, 0, 0), pipeline_mode=once),
                  pl.BlockSpec((hb, nb, dh), lambda g, i: (g, 0, 0))],
        out_specs=pl.BlockSpec((hb, dh, MOBA_BLOCK), lambda g, i: (g, 0, i)),
        out_shape=jax.ShapeDtypeStruct((nh, dh, t), F32),
        scratch_shapes=[pltpu.VMEM((hb, nb, MOBA_BLOCK), F32),
                        pltpu.VMEM((2, hb, MOBA_BLOCK, MOBA_BLOCK), F32)],
        compiler_params=_params("parallel", "arbitrary"),
    )(qt, k, vt, kmean)


def _attn_out_kernel(at_ref, x_ref, wo_ref, lng_ref, lnb_ref, o_ref):
    h = _tn_dot(at_ref[...], wo_ref[...])
    o_ref[...] = _layer_norm(ALPHA * x_ref[...] + h, lng_ref[...], lnb_ref[...])


def _attn_out(at, x, w_o, ln_g, ln_b, tm):
    t, c = x.shape
    consts = [w_o, ln_g, ln_b]
    return pl.pallas_call(
        _attn_out_kernel,
        grid=(t // tm,),
        in_specs=[pl.BlockSpec((c, tm), lambda i: (0, i)),
                  pl.BlockSpec((tm, c), lambda i: (i, 0))] + [_full(a.shape) for a in consts],
        out_specs=pl.BlockSpec((tm, c), lambda i: (i, 0)),
        out_shape=jax.ShapeDtypeStruct((t, c), F32),
        compiler_params=_params("parallel"),
    )(at, x, *consts)


def _row(a):
    return a.reshape(1, -1).astype(F32)


def _rwkv_layer(x, p, ln_g, ln_b, e, et, tm):
    r, lw, k, v, kk, b, g, bonus = _rwkv_proj(x, p, e, et, tm)
    y = _wkv_scan(r, lw, k, v, kk, b, tb=min(256, x.shape[0]))
    return _rwkv_out(y, g, bonus, x, p['gn_g'], p['gn_b'], p['w_o'], ln_g, ln_b, e, et, tm)


def _rope_tables(t):
    half = HEAD_DIM // 2
    inv = ROPE_THETA ** (-jnp.arange(0, HEAD_DIM, 2, dtype=F32) / HEAD_DIM)
    ang = jnp.arange(t, dtype=F32)[:, None] * inv[None, :]
    cos, sin = jnp.cos(ang), jnp.sin(ang)
    reps = LANES // HEAD_DIM
    cos_k = jnp.tile(jnp.concatenate([cos, cos], axis=1), (1, reps))
    sin_k = jnp.tile(jnp.concatenate([-sin, sin], axis=1), (1, reps))
    del half
    return cos.T, sin.T, cos_k, sin_k


def _moba_layer(x, w_qkv, w_o, ln_g, ln_b, tm):
    t, c = x.shape
    nh = c // HEAD_DIM
    wq_t = w_qkv[:, :c].T.astype(BF16)
    wk = w_qkv[:, c:2 * c].astype(BF16)
    wv_t = w_qkv[:, 2 * c:].T.astype(BF16)
    cos_t, sin_t, cos_k, sin_k = _rope_tables(t)
    qt, k, vt, kmean = _qkv_rope(x, wq_t, wk, wv_t, cos_t, sin_t, cos_k, sin_k, tm)
    kmean = jnp.transpose(kmean.reshape(t // MOBA_BLOCK, nh, HEAD_DIM), (1, 0, 2))
    at = _moba_attention(qt, k, vt, kmean, hb=min(MOBA_HEADS_PER_STEP, nh))
    return _attn_out(at.reshape(c, t), x, w_o.astype(BF16), ln_g, ln_b, tm)


def kernel(x, rwkv_mu, rwkv_w_rkv, rwkv_w0, rwkv_w1, rwkv_w2, rwkv_a0, rwkv_a1, rwkv_a2, rwkv_g1, rwkv_g2,
           rwkv_k_k, rwkv_k_a, rwkv_r_k, rwkv_gn_g, rwkv_gn_b, rwkv_w_o, moba_w_qkv, moba_w_o,
           ffn_w_in, ffn_w_down, ln_mix_g, ln_mix_b, ln_ffn_g, ln_ffn_b):
    bsz, t, c = x.shape
    assert c % WKV_GROUP == 0 and t % MOBA_BLOCK == 0
    tm = min(512, t)
    e, et = _head_indicator(c)
    outs = []
    for bi in range(bsz):
        h = x[bi]
        for i in range(DEPTH):
            j = i // 2
            if i % 2 == 0:
                p = dict(mu=rwkv_mu[j], w_rkv=rwkv_w_rkv[j].astype(BF16), w0=_row(rwkv_w0[j]),
                         w1=rwkv_w1[j].astype(BF16), w2=rwkv_w2[j].astype(BF16), a0=_row(rwkv_a0[j]),
                         a1=rwkv_a1[j].astype(BF16), a2=rwkv_a2[j].astype(BF16),
                         g1=rwkv_g1[j].astype(BF16), g2=rwkv_g2[j].astype(BF16),
                         k_k=_row(rwkv_k_k[j]), k_a=_row(rwkv_k_a[j]), r_k=_row(rwkv_r_k[j]),
                         gn_g=_row(rwkv_gn_g[j]), gn_b=_row(rwkv_gn_b[j]), w_o=rwkv_w_o[j].astype(BF16))
                h = _rwkv_layer(h, p, _row(ln_mix_g[i]), _row(ln_mix_b[i]), e, et, tm)
            else:
                h = _moba_layer(h, moba_w_qkv[j], moba_w_o[j], _row(ln_mix_g[i]), _row(ln_mix_b[i]), tm)
            h = _ffn(h, ffn_w_in[i].astype(BF16), ffn_w_down[i].astype(BF16),
                     _row(ln_ffn_g[i]), _row(ln_ffn_b[i]), tm)
        outs.append(h)
    return jnp.stack(outs, axis=0)
```

```python
import functools

import jax
import jax.numpy as jnp
from jax import lax
from jax.experimental import pallas as pl
from jax.experimental.pallas import tpu as pltpu

F32 = jnp.float32
BF16 = jnp.bfloat16

HEAD_DIM = 64
DEPTH = 2
ALPHA = (2 * DEPTH) ** 0.25
LN_EPS = 1e-5
GN_EPS = HEAD_DIM * 1e-5
MOBA_BLOCK = 256
MOBA_TOPK = 3
ROPE_THETA = 10000.0
LOG2_E = 1.4426950408889634
MOBA_HEADS_PER_STEP = 4

LANES = 128
MXU_DIM = 256
WKV_CHUNK = 64
WKV_GROUP = MXU_DIM
NEG_BIG = -1e30
VMEM_LIMIT = 56 * 1024 * 1024


def _params(*sem):
    return pltpu.CompilerParams(dimension_semantics=sem, vmem_limit_bytes=VMEM_LIMIT)


def _full(shape):
    n = len(shape)
    return pl.BlockSpec(shape, lambda *_: (0,) * n)


def _bdot(a, b):
    return jnp.dot(a.astype(BF16), b.astype(BF16), preferred_element_type=F32)


def _nt_dot(a, b):
    return lax.dot_general(a.astype(BF16), b.astype(BF16), (((1,), (1,)), ((), ())),
                           preferred_element_type=F32)


def _tn_dot(a, b):
    return lax.dot_general(a.astype(BF16), b.astype(BF16), (((0,), (0,)), ((), ())),
                           preferred_element_type=F32)


def _split2(a):
    hi = a.astype(BF16)
    lo = (a - hi.astype(F32)).astype(BF16)
    return hi, lo


def _split3(a):
    hi = a.astype(BF16)
    r1 = a - hi.astype(F32)
    mid = r1.astype(BF16)
    lo = (r1 - mid.astype(F32)).astype(BF16)
    return hi, mid, lo


def _dot_exact_rhs(a, b_exact):
    hi, lo = _split2(a)
    return (jnp.dot(hi, b_exact, preferred_element_type=F32)
            + jnp.dot(lo, b_exact, preferred_element_type=F32))


def _head_sum(a, e, et):
    return _dot_exact_rhs(_dot_exact_rhs(a, e), et)


def _layer_norm(z, g, b):
    mu = jnp.mean(z, axis=-1, keepdims=True)
    d = z - mu
    var = jnp.mean(d * d, axis=-1, keepdims=True)
    return d * lax.rsqrt(var + LN_EPS) * g + b


def _head_indicator(c):
    h = jnp.arange(c)[:, None] // HEAD_DIM
    e = (h == jnp.arange(LANES)[None, :]).astype(BF16)
    return e, e.T


def _rwkv_proj_kernel(x_ref, xp_ref, mu_ref, wrkv_ref, w0_ref, w1_ref, w2_ref, a0_ref, a1_ref, a2_ref,
                      g1_ref, g2_ref, kk_w_ref, ka_w_ref, rk_w_ref, e_ref, et_ref,
                      r_out, lw_out, k_out, v_out, kk_out, b_out, g_out, bonus_out):
    i = pl.program_id(0)
    x = x_ref[...]
    tm = x.shape[0]
    prev_last = jnp.where(i > 0, xp_ref[7:8, :], 0.0)
    row = lax.broadcasted_iota(jnp.int32, x.shape, 0)
    shifted = jnp.where(row == 0, prev_last, pltpu.roll(x, 1, 0))
    xx = shifted - x
    mix = lambda n: x + xx * mu_ref[n:n + 1, :]
    e, et = e_ref[...], et_ref[...]

    r = _bdot(mix(0), wrkv_ref[0])
    k = _bdot(mix(2), wrkv_ref[1])
    v = _bdot(mix(3), wrkv_ref[2])
    zw = w0_ref[...] + _bdot(jnp.tanh(_bdot(mix(1), w1_ref[...])), w2_ref[...])
    u = -zw
    w_log = -(jnp.maximum(u, 0.0) + jnp.log(1.0 + jnp.exp(-jnp.abs(u)))) - 0.5
    a = jax.nn.sigmoid(a0_ref[...] + _bdot(_bdot(mix(4), a1_ref[...]), a2_ref[...]))
    g = _bdot(jax.nn.sigmoid(_bdot(mix(5), g1_ref[...])), g2_ref[...])

    kk = k * kk_w_ref[...]
    norm = jnp.sqrt(_head_sum(kk * kk, e, et))
    kk = kk / jnp.maximum(norm, 1e-12)
    k = k * (1.0 + (a - 1.0) * ka_w_ref[...])

    r_out[...] = r
    lw_out[...] = -jnp.exp(w_log)
    k_out[...] = k
    v_out[...] = v
    kk_out[...] = kk
    b_out[...] = kk * a
    g_out[...] = g
    bonus_out[...] = _head_sum(r * k * rk_w_ref[...], e, et) * v
    del tm


def _rwkv_proj(x, p, e, et, tm):
    t, c = x.shape
    row = lambda: pl.BlockSpec((tm, c), lambda i: (i, 0))
    prev = pl.BlockSpec((8, c), lambda i: (jnp.maximum(i * (tm // 8) - 1, 0), 0))
    ins = [x, x, p['mu'], p['w_rkv'], p['w0'], p['w1'], p['w2'], p['a0'], p['a1'], p['a2'],
           p['g1'], p['g2'], p['k_k'], p['k_a'], p['r_k'], e, et]
    in_specs = [row(), prev] + [_full(a.shape) for a in ins[2:]]
    out = jax.ShapeDtypeStruct((t, c), F32)
    return pl.pallas_call(
        _rwkv_proj_kernel,
        grid=(t // tm,),
        in_specs=in_specs,
        out_specs=[row() for _ in range(8)],
        out_shape=[out] * 8,
        compiler_params=_params("parallel"),
    )(*ins)


def _bmm(a, b):
    return lax.dot_general(a.astype(BF16), b.astype(BF16), (((2,), (1,)), ((0,), (0,))),
                           preferred_element_type=F32)


def _wkv_local_kernel(r_ref, lw_ref, k_ref, v_ref, kk_ref, b_ref,
                      al_out, rt_out, u0_out, y0_out, bh_out, kh_out, pl_out):
    tb, gw = r_ref.shape
    L = WKV_CHUNK
    nc = tb // L
    nh = gw // HEAD_DIM
    n = nh * L

    lane = lax.broadcasted_iota(jnp.int32, (1, 1, gw), 2)
    head_masks = [lane // HEAD_DIM == h for h in range(nh)]
    col = lax.broadcasted_iota(jnp.int32, (1, 1, n), 2)
    col_masks = [col // L == h for h in range(nh)]
    ri = lax.broadcasted_iota(jnp.int32, (1, L, n), 1)
    ci = lax.broadcasted_iota(jnp.int32, (1, L, n), 2) % L
    strict = ci < ri
    incl = ci <= ri
    eye_cat = (ci == ri).astype(F32)
    rb = lax.broadcasted_iota(jnp.int32, (tb, tb), 0)
    cb = lax.broadcasted_iota(jnp.int32, (tb, tb), 1)
    tri = jnp.logical_and(cb <= rb, cb // L == rb // L).astype(BF16)

    def stack(a, masks):
        a16 = a.astype(BF16)
        return jnp.concatenate([jnp.where(m, a16, 0) for m in masks], axis=1)

    lw2 = lw_ref[...]
    h1, h2, h3 = _split3(lw2)
    cum2 = (jnp.dot(tri, h1, preferred_element_type=F32)
            + jnp.dot(tri, h2, preferred_element_type=F32)
            + jnp.dot(tri, h3, preferred_element_type=F32))
    to3 = lambda a: a.reshape(nc, L, gw)
    cum, lw = to3(cum2), to3(lw2)
    r, k, v, kk, b = (to3(ref[...]) for ref in (r_ref, k_ref, v_ref, kk_ref, b_ref))
    cum_l = cum[:, L - 1:L, :]
    p_inv = jnp.exp(-cum)
    al = -(kk * jnp.exp(cum - lw))
    bt = b * p_inv
    kt = k * p_inv
    rt = r * jnp.exp(cum)
    dec = jnp.exp(cum_l - cum)

    xs = jnp.concatenate([al, rt], axis=1)
    ys = jnp.concatenate([stack(bt, head_masks), stack(kt, head_masks)], axis=1)
    res = lax.dot_general(xs.astype(BF16), ys.astype(BF16), (((2,), (2,)), ((0,), (0,))),
                          preferred_element_type=F32)
    a_ab = jnp.where(strict, res[:, :L, :n], 0.0)
    a_ak = jnp.where(strict, res[:, :L, n:], 0.0)
    a_rb = jnp.where(incl, res[:, L:, :n], 0.0)
    a_rk = jnp.where(incl, res[:, L:, n:], 0.0)

    pw = a_ab
    tinv = eye_cat + pw
    pw = _bmm(pw, stack(pw, col_masks))
    sq = 2
    while 2 * sq < L:
        both = _bmm(jnp.concatenate([pw, tinv], axis=1), stack(pw, col_masks))
        pw, tinv = both[:, :L], tinv + both[:, L:]
        sq *= 2
    tinv = tinv + _bmm(tinv, stack(pw, col_masks))

    v_st = stack(v, head_masks)
    al_p = _bmm(tinv, stack(al, head_masks))
    u0 = _bmm(tinv, stack(_bmm(a_ak, v_st), head_masks))
    rt_p = rt + _bmm(a_rb, stack(al_p, head_masks))
    y0 = _bmm(a_rb, stack(u0, head_masks)) + _bmm(a_rk, v_st)

    to2 = lambda a: a.reshape(tb, gw)
    al_out[...] = to2(al_p).astype(BF16)
    rt_out[...] = to2(rt_p).astype(BF16)
    u0_out[...] = to2(u0)
    y0_out[...] = to2(y0)
    bh_out[...] = to2(b * dec).astype(BF16)
    kh_out[...] = to2(k * dec).astype(BF16)
    pl_out[...] = jnp.exp(cum_l)


def _wkv_local(r, lw, k, v, kk, b, tb):
    t, c = r.shape
    gw = WKV_GROUP
    nc = tb // WKV_CHUNK
    spec = lambda: pl.BlockSpec((tb, gw), lambda i, g: (i, g))
    f32 = jax.ShapeDtypeStruct((t, c), F32)
    bf16 = jax.ShapeDtypeStruct((t, c), BF16)
    return pl.pallas_call(
        _wkv_local_kernel,
        grid=(t // tb, c // gw),
        in_specs=[spec() for _ in range(6)],
        out_specs=[spec() for _ in range(6)] + [pl.BlockSpec((nc, 1, gw), lambda i, g: (i, 0, g))],
        out_shape=[bf16, bf16, f32, f32, bf16, bf16,
                   jax.ShapeDtypeStruct((t // WKV_CHUNK, 1, c), F32)],
        compiler_params=_params("parallel", "parallel"),
    )(r, lw, k, v, kk, b)


def _wkv_state_kernel(al_ref, rt_ref, u0_ref, y0_ref, bh_ref, kh_ref, v_ref, pl_ref, y_ref, sv_ref):
    tb, c = v_ref.shape
    L = WKV_CHUNK
    gw = WKV_GROUP
    groups = range(c // gw)

    @pl.when(pl.program_id(0) == 0)
    def _():
        sv_ref[...] = jnp.zeros_like(sv_ref)

    bd_mask = (lax.broadcasted_iota(jnp.int32, (gw, gw), 0) // HEAD_DIM
               == lax.broadcasted_iota(jnp.int32, (gw, gw), 1) // HEAD_DIM)

    def chunk(ci, carry):
        rows = pl.ds(pl.multiple_of(ci * L, L), L)
        lanes = [slice(g * gw, (g + 1) * gw) for g in groups]
        svs = [sv_ref[g] for g in groups]
        uys = [_nt_dot(jnp.concatenate([al_ref[rows, lanes[g]], rt_ref[rows, lanes[g]]], axis=0), svs[g])
               for g in groups]
        us = [uys[g][:L] + u0_ref[rows, lanes[g]] for g in groups]
        for g in groups:
            y_ref[rows, lanes[g]] = uys[g][L:] + y0_ref[rows, lanes[g]]
        upds = [_tn_dot(jnp.concatenate([us[g].astype(BF16), v_ref[rows, lanes[g]].astype(BF16)], axis=0),
                        jnp.concatenate([bh_ref[rows, lanes[g]], kh_ref[rows, lanes[g]]], axis=0))
                for g in groups]
        for g in groups:
            sv_ref[g] = svs[g] * pl_ref[ci, :, lanes[g]] + jnp.where(bd_mask, upds[g], 0.0)
        return carry

    lax.fori_loop(0, tb // L, chunk, 0)


def _wkv_state(al_p, rt_p, u0, y0, bh, kh, v, p_l, tb):
    t, c = v.shape
    nc = tb // WKV_CHUNK
    row = lambda: pl.BlockSpec((tb, c), lambda i: (i, 0))
    return pl.pallas_call(
        _wkv_state_kernel,
        grid=(t // tb,),
        in_specs=[row() for _ in range(7)] + [pl.BlockSpec((nc, 1, c), lambda i: (i, 0, 0))],
        out_specs=row(),
        out_shape=jax.ShapeDtypeStruct((t, c), F32),
        scratch_shapes=[pltpu.VMEM((c // WKV_GROUP, WKV_GROUP, WKV_GROUP), F32)],
        compiler_params=_params("arbitrary"),
    )(al_p, rt_p, u0, y0, bh, kh, v, p_l)


def _wkv_scan(r, lw, k, v, kk, b, tb_local, tb_state):
    al_p, rt_p, u0, y0, bh, kh, p_l = _wkv_local(r, lw, k, v, kk, b, tb_local)
    return _wkv_state(al_p, rt_p, u0, y0, bh, kh, v, p_l, tb_state)


def _rwkv_out_kernel(y_ref, g_ref, bonus_ref, x_ref, gng_ref, gnb_ref, wo_ref, lng_ref, lnb_ref,
                     e_ref, et_ref, o_ref):
    e, et = e_ref[...], et_ref[...]
    y = y_ref[...]
    mean = _head_sum(y, e, et) * (1.0 / HEAD_DIM)
    d = y - mean
    var = _head_sum(d * d, e, et) * (1.0 / HEAD_DIM)
    yn = d * lax.rsqrt(var + GN_EPS) * gng_ref[...] + gnb_ref[...]
    z = (yn + bonus_ref[...]) * g_ref[...]
    h = _bdot(z, wo_ref[...])
    o_ref[...] = _layer_norm(ALPHA * x_ref[...] + h, lng_ref[...], lnb_ref[...])


def _rwkv_out(y, g, bonus, x, gn_g, gn_b, w_o, ln_g, ln_b, e, et, tm):
    t, c = x.shape
    row = lambda: pl.BlockSpec((tm, c), lambda i: (i, 0))
    consts = [gn_g, gn_b, w_o, ln_g, ln_b, e, et]
    return pl.pallas_call(
        _rwkv_out_kernel,
        grid=(t // tm,),
        in_specs=[row() for _ in range(4)] + [_full(a.shape) for a in consts],
        out_specs=row(),
        out_shape=jax.ShapeDtypeStruct((t, c), F32),
        compiler_params=_params("parallel"),
    )(y, g, bonus, x, *consts)


def _ffn_kernel(x_ref, win_ref, wdown_ref, lng_ref, lnb_ref, o_ref, *, n_split):
    x = x_ref[...]
    xb = x.astype(BF16)
    f = wdown_ref.shape[0]
    fc = f // n_split
    acc = jnp.zeros(x.shape, F32)
    for s in range(n_split):
        gate = jnp.dot(xb, win_ref[:, s * fc:(s + 1) * fc], preferred_element_type=F32)
        up = jnp.dot(xb, win_ref[:, f + s * fc:f + (s + 1) * fc], preferred_element_type=F32)
        act = (gate * jax.nn.sigmoid(gate) * up).astype(BF16)
        acc = acc + jnp.dot(act, wdown_ref[s * fc:(s + 1) * fc, :], preferred_element_type=F32)
    o_ref[...] = _layer_norm(ALPHA * x + acc, lng_ref[...], lnb_ref[...])


def _ffn(x, w_in, w_down, ln_g, ln_b, tm):
    t, c = x.shape
    f = w_down.shape[0]
    n_split = 2 if (f // 2) % LANES == 0 else 1
    row = lambda: pl.BlockSpec((tm, c), lambda i: (i, 0))
    consts = [w_in, w_down, ln_g, ln_b]
    return pl.pallas_call(
        functools.partial(_ffn_kernel, n_split=n_split),
        grid=(t // tm,),
        in_specs=[row()] + [_full(a.shape) for a in consts],
        out_specs=row(),
        out_shape=jax.ShapeDtypeStruct((t, c), F32),
        compiler_params=_params("parallel"),
    )(x, *consts)


def _qkv_kernel(x_ref, wqt_ref, wk_ref, wvt_ref, cos_t_ref, sin_t_ref, cos_k_ref, sin_k_ref,
                qt_out, k_out, vt_out, kmean_out):
    xb = x_ref[...].astype(BF16)
    tm, c = xb.shape
    nh = c // HEAD_DIM
    half = HEAD_DIM // 2

    qt = _nt_dot(wqt_ref[...], xb)
    cos_t, sin_t = cos_t_ref[...], sin_t_ref[...]
    scale = HEAD_DIM ** -0.5 * LOG2_E
    for h in range(nh):
        q1 = qt[h * HEAD_DIM:h * HEAD_DIM + half]
        q2 = qt[h * HEAD_DIM + half:(h + 1) * HEAD_DIM]
        qt_out[h, 0:half, :] = ((q1 * cos_t - q2 * sin_t) * scale).astype(BF16)
        qt_out[h, half:HEAD_DIM, :] = ((q2 * cos_t + q1 * sin_t) * scale).astype(BF16)

    vt = _nt_dot(wvt_ref[...], xb)
    for h in range(nh):
        vt_out[h] = vt[h * HEAD_DIM:(h + 1) * HEAD_DIM].astype(BF16)

    k = jnp.dot(xb, wk_ref[...], preferred_element_type=F32)
    reps = c // cos_k_ref.shape[1]
    cos_k = jnp.concatenate([cos_k_ref[...]] * reps, axis=1)
    sin_k = jnp.concatenate([sin_k_ref[...]] * reps, axis=1)
    lane = lax.broadcasted_iota(jnp.int32, k.shape, 1)
    first = (lane % HEAD_DIM) < half
    rot = jnp.where(first, pltpu.roll(k, c - half, 1), pltpu.roll(k, half, 1))
    k = k * cos_k + rot * sin_k
    for h in range(nh):
        k_out[h] = k[:, h * HEAD_DIM:(h + 1) * HEAD_DIM].astype(BF16)
    for blk in range(tm // MOBA_BLOCK):
        kmean_out[blk] = jnp.mean(k[blk * MOBA_BLOCK:(blk + 1) * MOBA_BLOCK], axis=0, keepdims=True)


def _qkv_rope(x, wq_t, wk, wv_t, cos_t, sin_t, cos_k, sin_k, tm):
    t, c = x.shape
    nh = c // HEAD_DIM
    half = HEAD_DIM // 2
    nb = t // MOBA_BLOCK
    return pl.pallas_call(
        _qkv_kernel,
        grid=(t // tm,),
        in_specs=[pl.BlockSpec((tm, c), lambda i: (i, 0)),
                  _full(wq_t.shape), _full(wk.shape), _full(wv_t.shape),
                  pl.BlockSpec((half, tm), lambda i: (0, i)),
                  pl.BlockSpec((half, tm), lambda i: (0, i)),
                  pl.BlockSpec((tm, cos_k.shape[1]), lambda i: (i, 0)),
                  pl.BlockSpec((tm, sin_k.shape[1]), lambda i: (i, 0))],
        out_specs=[pl.BlockSpec((nh, HEAD_DIM, tm), lambda i: (0, 0, i)),
                   pl.BlockSpec((nh, tm, HEAD_DIM), lambda i: (0, i, 0)),
                   pl.BlockSpec((nh, HEAD_DIM, tm), lambda i: (0, 0, i)),
                   pl.BlockSpec((tm // MOBA_BLOCK, 1, c), lambda i: (i, 0, 0))],
        out_shape=[jax.ShapeDtypeStruct((nh, HEAD_DIM, t), BF16),
                   jax.ShapeDtypeStruct((nh, t, HEAD_DIM), BF16),
                   jax.ShapeDtypeStruct((nh, HEAD_DIM, t), BF16),
                   jax.ShapeDtypeStruct((nb, 1, c), F32)],
        compiler_params=_params("parallel"),
    )(x, wq_t, wk, wv_t, cos_t, sin_t, cos_k, sin_k)


def _moba_kernel(qt_ref, k_ref, vt_ref, kmean_ref, o_ref, sel_ref, s_ref):
    i = pl.program_id(1)
    hb, _, bq = qt_ref.shape
    nb = kmean_ref.shape[1]
    blk = MOBA_BLOCK

    qts = [qt_ref[h] for h in range(hb)]
    bidx = lax.broadcasted_iota(jnp.int32, (nb, bq), 0)
    for h in range(hb):
        km_hi, km_lo = _split2(kmean_ref[h])
        gate = (jnp.dot(km_hi, qts[h], preferred_element_type=F32)
                + jnp.dot(km_lo, qts[h], preferred_element_type=F32))
        gate = jnp.where(bidx < i, gate, -jnp.inf)
        sel = jnp.zeros(gate.shape, F32)
        for _ in range(MOBA_TOPK):
            m = jnp.max(gate, axis=0, keepdims=True)
            is_m = jnp.logical_and(gate == m, m > -jnp.inf)
            first = jnp.min(jnp.where(is_m, bidx, nb), axis=0, keepdims=True)
            pick = bidx == first
            sel = jnp.where(pick, 1.0, sel)
            gate = jnp.where(pick, -jnp.inf, gate)
        sel_ref[h] = sel

    ones = jnp.ones((16, blk), BF16)

    heads = range(hb)

    def put_scores(j, slot):
        koff = pl.multiple_of(j * blk, blk)
        for h in heads:
            s_ref[slot, h] = jnp.dot(k_ref[h, pl.ds(koff, blk), :], qts[h],
                                     preferred_element_type=F32)

    def attend(j, slot, carry, causal):
        koff = pl.multiple_of(j * blk, blk)
        ss = [s_ref[slot, h] for h in heads]
        if causal:
            key = lax.broadcasted_iota(jnp.int32, (blk, bq), 0)
            qry = lax.broadcasted_iota(jnp.int32, (blk, bq), 1)
            ss = [jnp.where(key <= qry, s, -jnp.inf) for s in ss]
        else:
            ons = [sel_ref[h, pl.ds(j, 1), :] > 0.0 for h in heads]
        cmaxs = [jnp.max(s, axis=0, keepdims=True) for s in ss]
        ms = [carry[h][0] for h in heads]
        if causal:
            m_news = [jnp.maximum(ms[h], cmaxs[h]) for h in heads]
            m_ps = m_news
        else:
            m_news = [jnp.where(ons[h], jnp.maximum(ms[h], cmaxs[h]), ms[h]) for h in heads]
            m_ps = [jnp.where(ons[h], m_news[h], cmaxs[h]) for h in heads]
        ps = [jnp.exp2(ss[h] - m_ps[h]).astype(BF16) for h in heads]
        os = [jnp.dot(jnp.concatenate([vt_ref[h, :, pl.ds(koff, blk)], ones], axis=0), ps[h],
                      preferred_element_type=F32) for h in heads]
        if not causal:
            os = [jnp.where(ons[h], os[h], 0.0) for h in heads]
        return tuple((m_news[h], carry[h][1] * jnp.exp2(ms[h] - m_news[h]) + os[h]) for h in heads)

    init = tuple((jnp.full((1, bq), NEG_BIG, F32), jnp.zeros((HEAD_DIM + 16, bq), F32)) for _ in heads)

    def body(jj, carry):
        j0 = 2 * jj
        put_scores(j0 + 1, 1)
        carry = attend(j0, 0, carry, False)
        put_scores(j0 + 2, 0)
        return attend(j0 + 1, 1, carry, False)

    def finish(carry):
        for h in heads:
            acc = carry[h][1]
            o_ref[h] = acc[:HEAD_DIM] / acc[HEAD_DIM:HEAD_DIM + 1]

    put_scores(0, 0)
    carry = lax.fori_loop(0, i // 2, body, init)

    @pl.when(i % 2 == 0)
    def _():
        finish(attend(i, 0, carry, True))

    @pl.when(i % 2 == 1)
    def _():
        put_scores(i, 1)
        finish(attend(i, 1, attend(i - 1, 0, carry, False), True))


def _moba_attention(qt, k, vt, kmean, hb):
    nh, dh, t = qt.shape
    nb = t // MOBA_BLOCK
    once = pl.Buffered(1)
    return pl.pallas_call(
        _moba_kernel,
        grid=(nh // hb, nb),
        in_specs=[pl.BlockSpec((hb, dh, MOBA_BLOCK), lambda g, i: (g, 0, i)),
                  pl.BlockSpec((hb, t, dh), lambda g, i: (g, 0, 0), pipeline_mode=once),
                  pl.BlockSpec((hb, dh, t), lambda g, i: (g, 0, 0), pipeline_mode=once),
                  pl.BlockSpec((hb, nb, dh), lambda g, i: (g, 0, 0))],
        out_specs=pl.BlockSpec((hb, dh, MOBA_BLOCK), lambda g, i: (g, 0, i)),
        out_shape=jax.ShapeDtypeStruct((nh, dh, t), F32),
        scratch_shapes=[pltpu.VMEM((hb, nb, MOBA_BLOCK), F32),
                        pltpu.VMEM((2, hb, MOBA_BLOCK, MOBA_BLOCK), F32)],
        compiler_params=_params("parallel", "arbitrary"),
    )(qt, k, vt, kmean)


def _attn_out_kernel(at_ref, x_ref, wo_ref, lng_ref, lnb_ref, o_ref):
    h = _tn_dot(at_ref[...], wo_ref[...])
    o_ref[...] = _layer_norm(ALPHA * x_ref[...] + h, lng_ref[...], lnb_ref[...])


def _attn_out(at, x, w_o, ln_g, ln_b, tm):
    t, c = x.shape
    consts = [w_o, ln_g, ln_b]
    return pl.pallas_call(
        _attn_out_kernel,
        grid=(t // tm,),
        in_specs=[pl.BlockSpec((c, tm), lambda i: (0, i)),
                  pl.BlockSpec((tm, c), lambda i: (i, 0))] + [_full(a.shape) for a in consts],
        out_specs=pl.BlockSpec((tm, c), lambda i: (i, 0)),
        out_shape=jax.ShapeDtypeStruct((t, c), F32),
        compiler_params=_params("parallel"),
    )(at, x, *consts)


def _row(a):
    return a.reshape(1, -1).astype(F32)


def _rwkv_layer(x, p, ln_g, ln_b, e, et, tm):
    r, lw, k, v, kk, b, g, bonus = _rwkv_proj(x, p, e, et, tm)
    t = x.shape[0]
    y = _wkv_scan(r, lw, k, v, kk, b, tb_local=min(512, t), tb_state=min(256, t))
    return _rwkv_out(y, g, bonus, x, p['gn_g'], p['gn_b'], p['w_o'], ln_g, ln_b, e, et, tm)


def _rope_tables(t):
    half = HEAD_DIM // 2
    inv = ROPE_THETA ** (-jnp.arange(0, HEAD_DIM, 2, dtype=F32) / HEAD_DIM)
    ang = jnp.arange(t, dtype=F32)[:, None] * inv[None, :]
    cos, sin = jnp.cos(ang), jnp.sin(ang)
    reps = LANES // HEAD_DIM
    cos_k = jnp.tile(jnp.concatenate([cos, cos], axis=1), (1, reps))
    sin_k = jnp.tile(jnp.concatenate([-sin, sin], axis=1), (1, reps))
    del half
    return cos.T, sin.T, cos_k, sin_k


def _moba_layer(x, w_qkv, w_o, ln_g, ln_b, tm):
    t, c = x.shape
    nh = c // HEAD_DIM
    wq_t = w_qkv[:, :c].T.astype(BF16)
    wk = w_qkv[:, c:2 * c].astype(BF16)
    wv_t = w_qkv[:, 2 * c:].T.astype(BF16)
    cos_t, sin_t, cos_k, sin_k = _rope_tables(t)
    qt, k, vt, kmean = _qkv_rope(x, wq_t, wk, wv_t, cos_t, sin_t, cos_k, sin_k, tm)
    kmean = jnp.transpose(kmean.reshape(t // MOBA_BLOCK, nh, HEAD_DIM), (1, 0, 2))
    at = _moba_attention(qt, k, vt, kmean, hb=min(MOBA_HEADS_PER_STEP, nh))
    return _attn_out(at.reshape(c, t), x, w_o.astype(BF16), ln_g, ln_b, tm)


def kernel(x, rwkv_mu, rwkv_w_rkv, rwkv_w0, rwkv_w1, rwkv_w2, rwkv_a0, rwkv_a1, rwkv_a2, rwkv_g1, rwkv_g2,
           rwkv_k_k, rwkv_k_a, rwkv_r_k, rwkv_gn_g, rwkv_gn_b, rwkv_w_o, moba_w_qkv, moba_w_o,
           ffn_w_in, ffn_w_down, ln_mix_g, ln_mix_b, ln_ffn_g, ln_ffn_b):
    bsz, t, c = x.shape
    assert c % WKV_GROUP == 0 and t % MOBA_BLOCK == 0
    tm = min(512, t)
    e, et = _head_indicator(c)
    outs = []
    for bi in range(bsz):
        h = x[bi]
        for i in range(DEPTH):
            j = i // 2
            if i % 2 == 0:
                p = dict(mu=rwkv_mu[j], w_rkv=rwkv_w_rkv[j].astype(BF16), w0=_row(rwkv_w0[j]),
                         w1=rwkv_w1[j].astype(BF16), w2=rwkv_w2[j].astype(BF16), a0=_row(rwkv_a0[j]),
                         a1=rwkv_a1[j].astype(BF16), a2=rwkv_a2[j].astype(BF16),
                         g1=rwkv_g1[j].astype(BF16), g2=rwkv_g2[j].astype(BF16),
                         k_k=_row(rwkv_k_k[j]), k_a=_row(rwkv_k_a[j]), r_k=_row(rwkv_r_k[j]),
                         gn_g=_row(rwkv_gn_g[j]), gn_b=_row(rwkv_gn_b[j]), w_o=rwkv_w_o[j].astype(BF16))
                h = _rwkv_layer(h, p, _row(ln_mix_g[i]), _row(ln_mix_b[i]), e, et, tm)
            else:
                h = _moba_layer(h, moba_w_qkv[j], moba_w_o[j], _row(ln_mix_g[i]), _row(ln_mix_b[i]), tm)
            h = _ffn(h, ffn_w_in[i].astype(BF16), ffn_w_down[i].astype(BF16),
                     _row(ln_ffn_g[i]), _row(ln_ffn_b[i]), tm)
        outs.append(h)
    return jnp.stack(outs, axis=0)
```

```python
import functools

import jax
import jax.numpy as jnp
from jax import lax
from jax.experimental import pallas as pl
from jax.experimental.pallas import tpu as pltpu

F32 = jnp.float32
BF16 = jnp.bfloat16

HEAD_DIM = 64
DEPTH = 2
ALPHA = (2 * DEPTH) ** 0.25
LN_EPS = 1e-5
GN_EPS = HEAD_DIM * 1e-5
MOBA_BLOCK = 256
MOBA_TOPK = 3
ROPE_THETA = 10000.0
LOG2_E = 1.4426950408889634
MOBA_HEADS_PER_STEP = 4
MOBA_BLOCKS_PER_ITER = 4

LANES = 128
MXU_DIM = 256
WKV_CHUNK = 64
WKV_GROUP = MXU_DIM
NEG_BIG = -1e30
VMEM_LIMIT = 56 * 1024 * 1024


def _params(*sem):
    return pltpu.CompilerParams(dimension_semantics=sem, vmem_limit_bytes=VMEM_LIMIT)


def _full(shape):
    n = len(shape)
    return pl.BlockSpec(shape, lambda *_: (0,) * n)


def _bdot(a, b):
    return jnp.dot(a.astype(BF16), b.astype(BF16), preferred_element_type=F32)


def _nt_dot(a, b):
    return lax.dot_general(a.astype(BF16), b.astype(BF16), (((1,), (1,)), ((), ())),
                           preferred_element_type=F32)


def _tn_dot(a, b):
    return lax.dot_general(a.astype(BF16), b.astype(BF16), (((0,), (0,)), ((), ())),
                           preferred_element_type=F32)


def _split2(a):
    hi = a.astype(BF16)
    lo = (a - hi.astype(F32)).astype(BF16)
    return hi, lo


def _split3(a):
    hi = a.astype(BF16)
    r1 = a - hi.astype(F32)
    mid = r1.astype(BF16)
    lo = (r1 - mid.astype(F32)).astype(BF16)
    return hi, mid, lo


def _dot_exact_rhs(a, b_exact):
    hi, lo = _split2(a)
    return (jnp.dot(hi, b_exact, preferred_element_type=F32)
            + jnp.dot(lo, b_exact, preferred_element_type=F32))


def _head_sum(a, ones_bd):
    w = ones_bd.shape[0]
    return jnp.concatenate([_dot_exact_rhs(a[:, s:s + w], ones_bd) for s in range(0, a.shape[1], w)],
                           axis=1)


def _layer_norm(z, g, b):
    mu = jnp.mean(z, axis=-1, keepdims=True)
    d = z - mu
    var = jnp.mean(d * d, axis=-1, keepdims=True)
    return d * lax.rsqrt(var + LN_EPS) * g + b


def _head_ones():
    h = jnp.arange(MXU_DIM) // HEAD_DIM
    return (h[:, None] == h[None, :]).astype(BF16)


def _rwkv_proj_kernel(x_ref, xp_ref, mu_ref, wrkv_ref, w0_ref, w1_ref, w2_ref, a0_ref, a1_ref, a2_ref,
                      g1_ref, g2_ref, kk_w_ref, ka_w_ref, rk_w_ref, hs_ref,
                      r_out, lw_out, k_out, v_out, kk_out, b_out, g_out, bonus_out):
    i = pl.program_id(0)
    x = x_ref[...]
    prev_last = jnp.where(i > 0, xp_ref[7:8, :], 0.0)
    row = lax.broadcasted_iota(jnp.int32, x.shape, 0)
    shifted = jnp.where(row == 0, prev_last, pltpu.roll(x, 1, 0))
    xx = shifted - x
    mix = lambda n: x + xx * mu_ref[n:n + 1, :]
    hs = hs_ref[...]

    r = _bdot(mix(0), wrkv_ref[0])
    k = _bdot(mix(2), wrkv_ref[1])
    v = _bdot(mix(3), wrkv_ref[2])
    zw = w0_ref[...] + _bdot(jnp.tanh(_bdot(mix(1), w1_ref[...])), w2_ref[...])
    u = -zw
    w_log = -(jnp.maximum(u, 0.0) + jnp.log(1.0 + jnp.exp(-jnp.abs(u)))) - 0.5
    a = jax.nn.sigmoid(a0_ref[...] + _bdot(_bdot(mix(4), a1_ref[...]), a2_ref[...]))
    g = _bdot(jax.nn.sigmoid(_bdot(mix(5), g1_ref[...])), g2_ref[...])

    kk = k * kk_w_ref[...]
    norm = jnp.sqrt(_head_sum(kk * kk, hs))
    kk = kk / jnp.maximum(norm, 1e-12)
    k = k * (1.0 + (a - 1.0) * ka_w_ref[...])

    r_out[...] = r.astype(BF16)
    lw_out[...] = -jnp.exp(w_log)
    k_out[...] = k.astype(BF16)
    v_out[...] = v.astype(BF16)
    kk_out[...] = kk.astype(BF16)
    b_out[...] = (kk * a).astype(BF16)
    g_out[...] = g.astype(BF16)
    bonus_out[...] = _head_sum(r * k * rk_w_ref[...], hs) * v


def _rwkv_proj(x, p, hs, tm):
    t, c = x.shape
    row = lambda: pl.BlockSpec((tm, c), lambda i: (i, 0))
    prev = pl.BlockSpec((8, c), lambda i: (jnp.maximum(i * (tm // 8) - 1, 0), 0))
    ins = [x, x, p['mu'], p['w_rkv'], p['w0'], p['w1'], p['w2'], p['a0'], p['a1'], p['a2'],
           p['g1'], p['g2'], p['k_k'], p['k_a'], p['r_k'], hs]
    in_specs = [row(), prev] + [_full(a.shape) for a in ins[2:]]
    f32 = jax.ShapeDtypeStruct((t, c), F32)
    bf16 = jax.ShapeDtypeStruct((t, c), BF16)
    return pl.pallas_call(
        _rwkv_proj_kernel,
        grid=(t // tm,),
        in_specs=in_specs,
        out_specs=[row() for _ in range(8)],
        out_shape=[bf16, f32, bf16, bf16, bf16, bf16, bf16, f32],
        compiler_params=_params("parallel"),
    )(*ins)


def _bmm(a, b):
    return lax.dot_general(a.astype(BF16), b.astype(BF16), (((2,), (1,)), ((0,), (0,))),
                           preferred_element_type=F32)


def _wkv_local_kernel(r_ref, lw_ref, k_ref, v_ref, kk_ref, b_ref,
                      al_out, rt_out, u0_out, y0_out, bh_out, kh_out, pl_out):
    tb, gw = r_ref.shape
    L = WKV_CHUNK
    nc = tb // L
    nh = gw // HEAD_DIM
    n = nh * L

    lane = lax.broadcasted_iota(jnp.int32, (1, 1, gw), 2)
    head_masks = [lane // HEAD_DIM == h for h in range(nh)]
    col = lax.broadcasted_iota(jnp.int32, (1, 1, n), 2)
    col_masks = [col // L == h for h in range(nh)]
    ri = lax.broadcasted_iota(jnp.int32, (1, L, n), 1)
    ci = lax.broadcasted_iota(jnp.int32, (1, L, n), 2) % L
    strict = ci < ri
    incl = ci <= ri
    eye_cat = (ci == ri).astype(F32)
    rb = lax.broadcasted_iota(jnp.int32, (tb, tb), 0)
    cb = lax.broadcasted_iota(jnp.int32, (tb, tb), 1)
    tri = jnp.logical_and(cb <= rb, cb // L == rb // L).astype(BF16)

    def stack(a, masks):
        a16 = a.astype(BF16)
        return jnp.concatenate([jnp.where(m, a16, 0) for m in masks], axis=1)

    lw2 = lw_ref[...]
    h1, h2, h3 = _split3(lw2)
    cum2 = (jnp.dot(tri, h1, preferred_element_type=F32)
            + jnp.dot(tri, h2, preferred_element_type=F32)
            + jnp.dot(tri, h3, preferred_element_type=F32))
    to3 = lambda a: a.reshape(nc, L, gw)
    cum, lw = to3(cum2), to3(lw2)
    r, k, v, kk, b = (to3(ref[...]) for ref in (r_ref, k_ref, v_ref, kk_ref, b_ref))
    cum_l = cum[:, L - 1:L, :]
    p_inv = jnp.exp(-cum)
    al = -(kk * jnp.exp(cum - lw))
    bt = b * p_inv
    kt = k * p_inv
    rt = r * jnp.exp(cum)
    dec = jnp.exp(cum_l - cum)

    xs = jnp.concatenate([al, rt], axis=1)
    ys = jnp.concatenate([stack(bt, head_masks), stack(kt, head_masks)], axis=1)
    res = lax.dot_general(xs.astype(BF16), ys.astype(BF16), (((2,), (2,)), ((0,), (0,))),
                          preferred_element_type=F32)
    a_ab = jnp.where(strict, res[:, :L, :n], 0.0)
    a_ak = jnp.where(strict, res[:, :L, n:], 0.0)
    a_rb = jnp.where(incl, res[:, L:, :n], 0.0)
    a_rk = jnp.where(incl, res[:, L:, n:], 0.0)

    pw = a_ab
    tinv = eye_cat + pw
    pw = _bmm(pw, stack(pw, col_masks))
    sq = 2
    while 2 * sq < L:
        both = _bmm(jnp.concatenate([pw, tinv], axis=1), stack(pw, col_masks))
        pw, tinv = both[:, :L], tinv + both[:, L:]
        sq *= 2
    tinv = tinv + _bmm(tinv, stack(pw, col_masks))

    v_st = stack(v, head_masks)
    al_p = _bmm(tinv, stack(al, head_masks))
    u0 = _bmm(tinv, stack(_bmm(a_ak, v_st), head_masks))
    rt_p = rt + _bmm(a_rb, stack(al_p, head_masks))
    y0 = _bmm(a_rb, stack(u0, head_masks)) + _bmm(a_rk, v_st)

    to2 = lambda a: a.reshape(tb, gw)
    al_out[...] = to2(al_p).astype(BF16)
    rt_out[...] = to2(rt_p).astype(BF16)
    u0_out[...] = to2(u0)
    y0_out[...] = to2(y0)
    bh_out[...] = to2(b * dec).astype(BF16)
    kh_out[...] = to2(k * dec).astype(BF16)
    pl_out[...] = jnp.exp(cum_l)


def _wkv_local(r, lw, k, v, kk, b, tb):
    t, c = r.shape
    gw = WKV_GROUP
    nc = tb // WKV_CHUNK
    spec = lambda: pl.BlockSpec((tb, gw), lambda i, g: (i, g))
    f32 = jax.ShapeDtypeStruct((t, c), F32)
    bf16 = jax.ShapeDtypeStruct((t, c), BF16)
    return pl.pallas_call(
        _wkv_local_kernel,
        grid=(t // tb, c // gw),
        in_specs=[spec() for _ in range(6)],
        out_specs=[spec() for _ in range(6)] + [pl.BlockSpec((nc, 1, gw), lambda i, g: (i, 0, g))],
        out_shape=[bf16, bf16, f32, f32, bf16, bf16,
                   jax.ShapeDtypeStruct((t // WKV_CHUNK, 1, c), F32)],
        compiler_params=_params("parallel", "parallel"),
    )(r, lw, k, v, kk, b)


def _wkv_state_kernel(al_ref, rt_ref, u0_ref, y0_ref, bh_ref, kh_ref, v_ref, pl_ref, y_ref, sv_ref):
    tb, c = v_ref.shape
    L = WKV_CHUNK
    gw = WKV_GROUP
    groups = range(c // gw)

    @pl.when(pl.program_id(0) == 0)
    def _():
        sv_ref[...] = jnp.zeros_like(sv_ref)

    bd_mask = (lax.broadcasted_iota(jnp.int32, (gw, gw), 0) // HEAD_DIM
               == lax.broadcasted_iota(jnp.int32, (gw, gw), 1) // HEAD_DIM)

    def chunk(ci, carry):
        rows = pl.ds(pl.multiple_of(ci * L, L), L)
        lanes = [slice(g * gw, (g + 1) * gw) for g in groups]
        svs = [sv_ref[g] for g in groups]
        uys = [_nt_dot(jnp.concatenate([al_ref[rows, lanes[g]], rt_ref[rows, lanes[g]]], axis=0), svs[g])
               for g in groups]
        us = [uys[g][:L] + u0_ref[rows, lanes[g]] for g in groups]
        for g in groups:
            y_ref[rows, lanes[g]] = uys[g][L:] + y0_ref[rows, lanes[g]]
        upds = [_tn_dot(jnp.concatenate([us[g].astype(BF16), v_ref[rows, lanes[g]].astype(BF16)], axis=0),
                        jnp.concatenate([bh_ref[rows, lanes[g]], kh_ref[rows, lanes[g]]], axis=0))
                for g in groups]
        for g in groups:
            sv_ref[g] = svs[g] * pl_ref[ci, :, lanes[g]] + jnp.where(bd_mask, upds[g], 0.0)
        return carry

    lax.fori_loop(0, tb // L, chunk, 0)


def _wkv_state(al_p, rt_p, u0, y0, bh, kh, v, p_l, tb):
    t, c = v.shape
    nc = tb // WKV_CHUNK
    row = lambda: pl.BlockSpec((tb, c), lambda i: (i, 0))
    return pl.pallas_call(
        _wkv_state_kernel,
        grid=(t // tb,),
        in_specs=[row() for _ in range(7)] + [pl.BlockSpec((nc, 1, c), lambda i: (i, 0, 0))],
        out_specs=row(),
        out_shape=jax.ShapeDtypeStruct((t, c), F32),
        scratch_shapes=[pltpu.VMEM((c // WKV_GROUP, WKV_GROUP, WKV_GROUP), F32)],
        compiler_params=_params("arbitrary"),
    )(al_p, rt_p, u0, y0, bh, kh, v, p_l)


def _wkv_scan(r, lw, k, v, kk, b, tb_local, tb_state):
    al_p, rt_p, u0, y0, bh, kh, p_l = _wkv_local(r, lw, k, v, kk, b, tb_local)
    return _wkv_state(al_p, rt_p, u0, y0, bh, kh, v, p_l, tb_state)


def _rwkv_out_kernel(y_ref, g_ref, bonus_ref, x_ref, gng_ref, gnb_ref, wo_ref, lng_ref, lnb_ref,
                     hs_ref, o_ref):
    hs = hs_ref[...]
    y = y_ref[...]
    mean = _head_sum(y, hs) * (1.0 / HEAD_DIM)
    d = y - mean
    var = _head_sum(d * d, hs) * (1.0 / HEAD_DIM)
    yn = d * lax.rsqrt(var + GN_EPS) * gng_ref[...] + gnb_ref[...]
    z = (yn + bonus_ref[...]) * g_ref[...]
    h = _bdot(z, wo_ref[...])
    o_ref[...] = _layer_norm(ALPHA * x_ref[...] + h, lng_ref[...], lnb_ref[...])


def _rwkv_out(y, g, bonus, x, gn_g, gn_b, w_o, ln_g, ln_b, hs, tm):
    t, c = x.shape
    row = lambda: pl.BlockSpec((tm, c), lambda i: (i, 0))
    consts = [gn_g, gn_b, w_o, ln_g, ln_b, hs]
    return pl.pallas_call(
        _rwkv_out_kernel,
        grid=(t // tm,),
        in_specs=[row() for _ in range(4)] + [_full(a.shape) for a in consts],
        out_specs=row(),
        out_shape=jax.ShapeDtypeStruct((t, c), F32),
        compiler_params=_params("parallel"),
    )(y, g, bonus, x, *consts)


def _ffn_kernel(x_ref, win_ref, wdown_ref, lng_ref, lnb_ref, o_ref, *, n_split):
    x = x_ref[...]
    xb = x.astype(BF16)
    f = wdown_ref.shape[0]
    fc = f // n_split
    acc = jnp.zeros(x.shape, F32)
    for s in range(n_split):
        gate = jnp.dot(xb, win_ref[:, s * fc:(s + 1) * fc], preferred_element_type=F32)
        up = jnp.dot(xb, win_ref[:, f + s * fc:f + (s + 1) * fc], preferred_element_type=F32)
        act = (gate * jax.nn.sigmoid(gate) * up).astype(BF16)
        acc = acc + jnp.dot(act, wdown_ref[s * fc:(s + 1) * fc, :], preferred_element_type=F32)
    o_ref[...] = _layer_norm(ALPHA * x + acc, lng_ref[...], lnb_ref[...])


def _ffn(x, w_in, w_down, ln_g, ln_b, tm):
    t, c = x.shape
    f = w_down.shape[0]
    n_split = 2 if (f // 2) % LANES == 0 else 1
    row = lambda: pl.BlockSpec((tm, c), lambda i: (i, 0))
    consts = [w_in, w_down, ln_g, ln_b]
    return pl.pallas_call(
        functools.partial(_ffn_kernel, n_split=n_split),
        grid=(t // tm,),
        in_specs=[row()] + [_full(a.shape) for a in consts],
        out_specs=row(),
        out_shape=jax.ShapeDtypeStruct((t, c), F32),
        compiler_params=_params("parallel"),
    )(x, *consts)


def _qkv_kernel(x_ref, wqt_ref, wk_ref, wvt_ref, cos_t_ref, sin_t_ref, cos_k_ref, sin_k_ref,
                qt_out, k_out, vt_out, kmean_out):
    xb = x_ref[...].astype(BF16)
    tm, c = xb.shape
    nh = c // HEAD_DIM
    half = HEAD_DIM // 2

    qt = _nt_dot(wqt_ref[...], xb)
    cos_t, sin_t = cos_t_ref[...], sin_t_ref[...]
    scale = HEAD_DIM ** -0.5 * LOG2_E
    for h in range(nh):
        q1 = qt[h * HEAD_DIM:h * HEAD_DIM + half]
        q2 = qt[h * HEAD_DIM + half:(h + 1) * HEAD_DIM]
        qt_out[h, 0:half, :] = ((q1 * cos_t - q2 * sin_t) * scale).astype(BF16)
        qt_out[h, half:HEAD_DIM, :] = ((q2 * cos_t + q1 * sin_t) * scale).astype(BF16)

    vt = _nt_dot(wvt_ref[...], xb)
    for h in range(nh):
        vt_out[h] = vt[h * HEAD_DIM:(h + 1) * HEAD_DIM].astype(BF16)

    k = jnp.dot(xb, wk_ref[...], preferred_element_type=F32)
    reps = c // cos_k_ref.shape[1]
    cos_k = jnp.concatenate([cos_k_ref[...]] * reps, axis=1)
    sin_k = jnp.concatenate([sin_k_ref[...]] * reps, axis=1)
    lane = lax.broadcasted_iota(jnp.int32, k.shape, 1)
    first = (lane % HEAD_DIM) < half
    rot = jnp.where(first, pltpu.roll(k, c - half, 1), pltpu.roll(k, half, 1))
    k = k * cos_k + rot * sin_k
    for h in range(nh):
        k_out[h] = k[:, h * HEAD_DIM:(h + 1) * HEAD_DIM].astype(BF16)
    for blk in range(tm // MOBA_BLOCK):
        kmean_out[blk] = jnp.mean(k[blk * MOBA_BLOCK:(blk + 1) * MOBA_BLOCK], axis=0, keepdims=True)


def _qkv_rope(x, wq_t, wk, wv_t, cos_t, sin_t, cos_k, sin_k, tm):
    t, c = x.shape
    nh = c // HEAD_DIM
    half = HEAD_DIM // 2
    nb = t // MOBA_BLOCK
    return pl.pallas_call(
        _qkv_kernel,
        grid=(t // tm,),
        in_specs=[pl.BlockSpec((tm, c), lambda i: (i, 0)),
                  _full(wq_t.shape), _full(wk.shape), _full(wv_t.shape),
                  pl.BlockSpec((half, tm), lambda i: (0, i)),
                  pl.BlockSpec((half, tm), lambda i: (0, i)),
                  pl.BlockSpec((tm, cos_k.shape[1]), lambda i: (i, 0)),
                  pl.BlockSpec((tm, sin_k.shape[1]), lambda i: (i, 0))],
        out_specs=[pl.BlockSpec((nh, HEAD_DIM, tm), lambda i: (0, 0, i)),
                   pl.BlockSpec((nh, tm, HEAD_DIM), lambda i: (0, i, 0)),
                   pl.BlockSpec((nh, HEAD_DIM, tm), lambda i: (0, 0, i)),
                   pl.BlockSpec((tm // MOBA_BLOCK, 1, c), lambda i: (i, 0, 0))],
        out_shape=[jax.ShapeDtypeStruct((nh, HEAD_DIM, t), BF16),
                   jax.ShapeDtypeStruct((nh, t, HEAD_DIM), BF16),
                   jax.ShapeDtypeStruct((nh, HEAD_DIM, t), BF16),
                   jax.ShapeDtypeStruct((nb, 1, c), F32)],
        compiler_params=_params("parallel"),
    )(x, wq_t, wk, wv_t, cos_t, sin_t, cos_k, sin_k)


def _moba_kernel(qt_ref, k_ref, vt_ref, kmean_ref, o_ref, sel_ref, s_ref):
    i = pl.program_id(1)
    hb, _, bq = qt_ref.shape
    nb = kmean_ref.shape[1]
    blk = MOBA_BLOCK

    qts = [qt_ref[h] for h in range(hb)]
    bidx = lax.broadcasted_iota(jnp.int32, (nb, bq), 0)
    for h in range(hb):
        km_hi, km_lo = _split2(kmean_ref[h])
        gate = (jnp.dot(km_hi, qts[h], preferred_element_type=F32)
                + jnp.dot(km_lo, qts[h], preferred_element_type=F32))
        gate = jnp.where(bidx < i, gate, -jnp.inf)
        sel = jnp.zeros(gate.shape, F32)
        for _ in range(MOBA_TOPK):
            m = jnp.max(gate, axis=0, keepdims=True)
            is_m = jnp.logical_and(gate == m, m > -jnp.inf)
            first = jnp.min(jnp.where(is_m, bidx, nb), axis=0, keepdims=True)
            pick = bidx == first
            sel = jnp.where(pick, 1.0, sel)
            gate = jnp.where(pick, -jnp.inf, gate)
        sel_ref[h] = sel

    ones = jnp.ones((16, blk), BF16)

    heads = range(hb)

    def put_scores(j, slot):
        koff = pl.multiple_of(j * blk, blk)
        for h in heads:
            s_ref[slot, h] = jnp.dot(k_ref[h, pl.ds(koff, blk), :], qts[h],
                                     preferred_element_type=F32)

    def attend(j, slot, carry, causal):
        koff = pl.multiple_of(j * blk, blk)
        ss = [s_ref[slot, h] for h in heads]
        if causal:
            key = lax.broadcasted_iota(jnp.int32, (blk, bq), 0)
            qry = lax.broadcasted_iota(jnp.int32, (blk, bq), 1)
            ss = [jnp.where(key <= qry, s, -jnp.inf) for s in ss]
        else:
            ons = [sel_ref[h, pl.ds(j, 1), :] > 0.0 for h in heads]
        cmaxs = [jnp.max(s, axis=0, keepdims=True) for s in ss]
        ms = [carry[h][0] for h in heads]
        if causal:
            m_news = [jnp.maximum(ms[h], cmaxs[h]) for h in heads]
            m_ps = m_news
        else:
            m_news = [jnp.where(ons[h], jnp.maximum(ms[h], cmaxs[h]), ms[h]) for h in heads]
            m_ps = [jnp.where(ons[h], m_news[h], cmaxs[h]) for h in heads]
        ps = [jnp.exp2(ss[h] - m_ps[h]).astype(BF16) for h in heads]
        os = [jnp.dot(jnp.concatenate([vt_ref[h, :, pl.ds(koff, blk)], ones], axis=0), ps[h],
                      preferred_element_type=F32) for h in heads]
        if not causal:
            os = [jnp.where(ons[h], os[h], 0.0) for h in heads]
        return tuple((m_news[h], carry[h][1] * jnp.exp2(ms[h] - m_news[h]) + os[h]) for h in heads)

    init = tuple((jnp.full((1, bq), NEG_BIG, F32), jnp.zeros((HEAD_DIM + 16, bq), F32)) for _ in heads)

    def blocks(j0, count, carry):
        for d in range(count):
            put_scores(j0 + d + 1, (d + 1) % 2)
            carry = attend(j0 + d, d % 2, carry, False)
        return carry

    def finish(carry):
        for h in heads:
            acc = carry[h][1]
            o_ref[h] = acc[:HEAD_DIM] / acc[HEAD_DIM:HEAD_DIM + 1]

    unroll = MOBA_BLOCKS_PER_ITER
    put_scores(0, 0)
    carry = lax.fori_loop(0, i // unroll, lambda jj, c: blocks(unroll * jj, unroll, c), init)
    done = (i // unroll) * unroll
    carry = lax.fori_loop(0, (i - done) // 2, lambda jj, c: blocks(done + 2 * jj, 2, c), carry)

    @pl.when(i % 2 == 0)
    def _():
        finish(attend(i, 0, carry, True))

    @pl.when(i % 2 == 1)
    def _():
        put_scores(i, 1)
        finish(attend(i, 1, attend(i - 1, 0, carry, False), True))


def _moba_attention(qt, k, vt, kmean, hb):
    nh, dh, t = qt.shape
    nb = t // MOBA_BLOCK
    once = pl.Buffered(1)
    return pl.pallas_call(
        _moba_kernel,
        grid=(nh // hb, nb),
        in_specs=[pl.BlockSpec((hb, dh, MOBA_BLOCK), lambda g, i: (g, 0, i)),
                  pl.BlockSpec((hb, t, dh), lambda g, i: (g, 0, 0), pipeline_mode=once),
                  pl.BlockSpec((hb, dh, t), lambda g, i: (g, 0, 0), pipeline_mode=once),
                  pl.BlockSpec((hb, nb, dh), lambda g, i: (g, 0, 0))],
        out_specs=pl.BlockSpec((hb, dh, MOBA_BLOCK), lambda g, i: (g, 0, i)),
        out_shape=jax.ShapeDtypeStruct((nh, dh, t), F32),
        scratch_shapes=[pltpu.VMEM((hb, nb, MOBA_BLOCK), F32),
                        pltpu.VMEM((2, hb, MOBA_BLOCK, MOBA_BLOCK), F32)],
        compiler_params=_params("parallel", "arbitrary"),
    )(qt, k, vt, kmean)


def _attn_out_kernel(at_ref, x_ref, wo_ref, lng_ref, lnb_ref, o_ref):
    h = _tn_dot(at_ref[...], wo_ref[...])
    o_ref[...] = _layer_norm(ALPHA * x_ref[...] + h, lng_ref[...], lnb_ref[...])


def _attn_out(at, x, w_o, ln_g, ln_b, tm):
    t, c = x.shape
    consts = [w_o, ln_g, ln_b]
    return pl.pallas_call(
        _attn_out_kernel,
        grid=(t // tm,),
        in_specs=[pl.BlockSpec((c, tm), lambda i: (0, i)),
                  pl.BlockSpec((tm, c), lambda i: (i, 0))] + [_full(a.shape) for a in consts],
        out_specs=pl.BlockSpec((tm, c), lambda i: (i, 0)),
        out_shape=jax.ShapeDtypeStruct((t, c), F32),
        compiler_params=_params("parallel"),
    )(at, x, *consts)


def _row(a):
    return a.reshape(1, -1).astype(F32)


def _rwkv_layer(x, p, ln_g, ln_b, hs, tm):
    r, lw, k, v, kk, b, g, bonus = _rwkv_proj(x, p, hs, tm)
    t = x.shape[0]
    y = _wkv_scan(r, lw, k, v, kk, b, tb_local=min(512, t), tb_state=min(256, t))
    return _rwkv_out(y, g, bonus, x, p['gn_g'], p['gn_b'], p['w_o'], ln_g, ln_b, hs, tm)


def _rope_tables(t):
    half = HEAD_DIM // 2
    inv = ROPE_THETA ** (-jnp.arange(0, HEAD_DIM, 2, dtype=F32) / HEAD_DIM)
    ang = jnp.arange(t, dtype=F32)[:, None] * inv[None, :]
    cos, sin = jnp.cos(ang), jnp.sin(ang)
    reps = LANES // HEAD_DIM
    cos_k = jnp.tile(jnp.concatenate([cos, cos], axis=1), (1, reps))
    sin_k = jnp.tile(jnp.concatenate([-sin, sin], axis=1), (1, reps))
    del half
    return cos.T, sin.T, cos_k, sin_k


def _moba_layer(x, w_qkv, w_o, ln_g, ln_b, tm):
    t, c = x.shape
    nh = c // HEAD_DIM
    wq_t = w_qkv[:, :c].T.astype(BF16)
    wk = w_qkv[:, c:2 * c].astype(BF16)
    wv_t = w_qkv[:, 2 * c:].T.astype(BF16)
    cos_t, sin_t, cos_k, sin_k = _rope_tables(t)
    qt, k, vt, kmean = _qkv_rope(x, wq_t, wk, wv_t, cos_t, sin_t, cos_k, sin_k, tm)
    kmean = jnp.transpose(kmean.reshape(t // MOBA_BLOCK, nh, HEAD_DIM), (1, 0, 2))
    at = _moba_attention(qt, k, vt, kmean, hb=min(MOBA_HEADS_PER_STEP, nh))
    return _attn_out(at.reshape(c, t), x, w_o.astype(BF16), ln_g, ln_b, tm)


def kernel(x, rwkv_mu, rwkv_w_rkv, rwkv_w0, rwkv_w1, rwkv_w2, rwkv_a0, rwkv_a1, rwkv_a2, rwkv_g1, rwkv_g2,
           rwkv_k_k, rwkv_k_a, rwkv_r_k, rwkv_gn_g, rwkv_gn_b, rwkv_w_o, moba_w_qkv, moba_w_o,
           ffn_w_in, ffn_w_down, ln_mix_g, ln_mix_b, ln_ffn_g, ln_ffn_b):
    bsz, t, c = x.shape
    assert c % WKV_GROUP == 0 and t % MOBA_BLOCK == 0
    tm = min(512, t)
    hs = _head_ones()
    outs = []
    for bi in range(bsz):
        h = x[bi]
        for i in range(DEPTH):
            j = i // 2
            if i % 2 == 0:
                p = dict(mu=rwkv_mu[j], w_rkv=rwkv_w_rkv[j].astype(BF16), w0=_row(rwkv_w0[j]),
                         w1=rwkv_w1[j].astype(BF16), w2=rwkv_w2[j].astype(BF16), a0=_row(rwkv_a0[j]),
                         a1=rwkv_a1[j].astype(BF16), a2=rwkv_a2[j].astype(BF16),
                         g1=rwkv_g1[j].astype(BF16), g2=rwkv_g2[j].astype(BF16),
                         k_k=_row(rwkv_k_k[j]), k_a=_row(rwkv_k_a[j]), r_k=_row(rwkv_r_k[j]),
                         gn_g=_row(rwkv_gn_g[j]), gn_b=_row(rwkv_gn_b[j]), w_o=rwkv_w_o[j].astype(BF16))
                h = _rwkv_layer(h, p, _row(ln_mix_g[i]), _row(ln_mix_b[i]), hs, tm)
            else:
                h = _moba_layer(h, moba_w_qkv[j], moba_w_o[j], _row(ln_mix_g[i]), _row(ln_mix_b[i]), tm)
            h = _ffn(h, ffn_w_in[i].astype(BF16), ffn_w_down[i].astype(BF16),
                     _row(ln_ffn_g[i]), _row(ln_ffn_b[i]), tm)
        outs.append(h)
    return jnp.stack(outs, axis=0)
```

```python
import functools

import jax
import jax.numpy as jnp
from jax import lax
from jax.experimental import pallas as pl
from jax.experimental.pallas import tpu as pltpu

F32 = jnp.float32
BF16 = jnp.bfloat16

HEAD_DIM = 64
DEPTH = 2
ALPHA = (2 * DEPTH) ** 0.25
LN_EPS = 1e-5
GN_EPS = HEAD_DIM * 1e-5
MOBA_BLOCK = 256
MOBA_TOPK = 3
ROPE_THETA = 10000.0
LOG2_E = 1.4426950408889634
MOBA_HEADS_PER_STEP = 4
MOBA_BLOCKS_PER_ITER = 8

LANES = 128
MXU_DIM = 256
WKV_CHUNK = 64
WKV_GROUP = MXU_DIM
NEG_BIG = -1e30
VMEM_LIMIT = 56 * 1024 * 1024


def _params(*sem):
    return pltpu.CompilerParams(dimension_semantics=sem, vmem_limit_bytes=VMEM_LIMIT)


def _full(shape):
    n = len(shape)
    return pl.BlockSpec(shape, lambda *_: (0,) * n)


def _bdot(a, b):
    return jnp.dot(a.astype(BF16), b.astype(BF16), preferred_element_type=F32)


def _nt_dot(a, b):
    return lax.dot_general(a.astype(BF16), b.astype(BF16), (((1,), (1,)), ((), ())),
                           preferred_element_type=F32)


def _tn_dot(a, b):
    return lax.dot_general(a.astype(BF16), b.astype(BF16), (((0,), (0,)), ((), ())),
                           preferred_element_type=F32)


def _split2(a):
    hi = a.astype(BF16)
    lo = (a - hi.astype(F32)).astype(BF16)
    return hi, lo


def _dot_exact_rhs(a, b_exact):
    hi, lo = _split2(a)
    return (jnp.dot(hi, b_exact, preferred_element_type=F32)
            + jnp.dot(lo, b_exact, preferred_element_type=F32))


def _head_sum(a, ones_bd):
    w = ones_bd.shape[0]
    return jnp.concatenate([_dot_exact_rhs(a[:, s:s + w], ones_bd) for s in range(0, a.shape[1], w)],
                           axis=1)


def _layer_norm(z, g, b):
    mu = jnp.mean(z, axis=-1, keepdims=True)
    d = z - mu
    var = jnp.mean(d * d, axis=-1, keepdims=True)
    return d * lax.rsqrt(var + LN_EPS) * g + b


def _head_ones():
    h = jnp.arange(MXU_DIM) // HEAD_DIM
    return (h[:, None] == h[None, :]).astype(BF16)


def _rwkv_proj_kernel(x_ref, xp_ref, mu_ref, wrkv_ref, w0_ref, w1_ref, w2_ref, a0_ref, a1_ref, a2_ref,
                      g1_ref, g2_ref, kk_w_ref, ka_w_ref, rk_w_ref, hs_ref,
                      r_out, lw_out, k_out, v_out, kk_out, b_out, g_out, bonus_out):
    i = pl.program_id(0)
    x = x_ref[...]
    prev_last = jnp.where(i > 0, xp_ref[7:8, :], 0.0)
    row = lax.broadcasted_iota(jnp.int32, x.shape, 0)
    shifted = jnp.where(row == 0, prev_last, pltpu.roll(x, 1, 0))
    xx = shifted - x
    mix = lambda n: x + xx * mu_ref[n:n + 1, :]
    hs = hs_ref[...]

    r = _bdot(mix(0), wrkv_ref[0])
    k = _bdot(mix(2), wrkv_ref[1])
    v = _bdot(mix(3), wrkv_ref[2])
    zw = w0_ref[...] + _bdot(jnp.tanh(_bdot(mix(1), w1_ref[...])), w2_ref[...])
    u = -zw
    w_log = -(jnp.maximum(u, 0.0) + jnp.log(1.0 + jnp.exp(-jnp.abs(u)))) - 0.5
    a = jax.nn.sigmoid(a0_ref[...] + _bdot(_bdot(mix(4), a1_ref[...]), a2_ref[...]))
    g = _bdot(jax.nn.sigmoid(_bdot(mix(5), g1_ref[...])), g2_ref[...])

    kk = k * kk_w_ref[...]
    norm = jnp.sqrt(_head_sum(kk * kk, hs))
    kk = kk / jnp.maximum(norm, 1e-12)
    k = k * (1.0 + (a - 1.0) * ka_w_ref[...])

    r_out[...] = r.astype(BF16)
    lw_out[...] = -jnp.exp(w_log)
    k_out[...] = k.astype(BF16)
    v_out[...] = v.astype(BF16)
    kk_out[...] = kk.astype(BF16)
    b_out[...] = (kk * a).astype(BF16)
    g_out[...] = g.astype(BF16)
    bonus_out[...] = _head_sum(r * k * rk_w_ref[...], hs) * v


def _rwkv_proj(x, p, hs, tm):
    t, c = x.shape
    row = lambda: pl.BlockSpec((tm, c), lambda i: (i, 0))
    prev = pl.BlockSpec((8, c), lambda i: (jnp.maximum(i * (tm // 8) - 1, 0), 0))
    ins = [x, x, p['mu'], p['w_rkv'], p['w0'], p['w1'], p['w2'], p['a0'], p['a1'], p['a2'],
           p['g1'], p['g2'], p['k_k'], p['k_a'], p['r_k'], hs]
    in_specs = [row(), prev] + [_full(a.shape) for a in ins[2:]]
    f32 = jax.ShapeDtypeStruct((t, c), F32)
    bf16 = jax.ShapeDtypeStruct((t, c), BF16)
    return pl.pallas_call(
        _rwkv_proj_kernel,
        grid=(t // tm,),
        in_specs=in_specs,
        out_specs=[row() for _ in range(8)],
        out_shape=[bf16, f32, bf16, bf16, bf16, bf16, bf16, f32],
        compiler_params=_params("parallel"),
    )(*ins)


def _bmm(a, b):
    return lax.dot_general(a.astype(BF16), b.astype(BF16), (((2,), (1,)), ((0,), (0,))),
                           preferred_element_type=F32)


def _wkv_local_kernel(r_ref, lw_ref, k_ref, v_ref, kk_ref, b_ref,
                      al_out, rt_out, u0_out, y0_out, bh_out, kh_out, pl_out):
    tb, gw = r_ref.shape
    L = WKV_CHUNK
    nc = tb // L
    nh = gw // HEAD_DIM
    n = nh * L

    lane = lax.broadcasted_iota(jnp.int32, (1, 1, gw), 2)
    head_masks = [lane // HEAD_DIM == h for h in range(nh)]
    col = lax.broadcasted_iota(jnp.int32, (1, 1, n), 2)
    col_masks = [col // L == h for h in range(nh)]
    ri = lax.broadcasted_iota(jnp.int32, (1, L, n), 1)
    ci = lax.broadcasted_iota(jnp.int32, (1, L, n), 2) % L
    strict = ci < ri
    incl = ci <= ri
    eye_cat = (ci == ri).astype(F32)
    tt = min(tb, MXU_DIM)
    rb = lax.broadcasted_iota(jnp.int32, (tt, tt), 0)
    cb = lax.broadcasted_iota(jnp.int32, (tt, tt), 1)
    tri = jnp.logical_and(cb <= rb, cb // L == rb // L).astype(BF16)

    def stack(a, masks):
        a16 = a.astype(BF16)
        return jnp.concatenate([jnp.where(m, a16, 0) for m in masks], axis=1)

    lw2 = lw_ref[...]
    hi, lo = _split2(lw2)
    cum2 = jnp.concatenate(
        [jnp.dot(tri, hi[s:s + tt], preferred_element_type=F32)
         + jnp.dot(tri, lo[s:s + tt], preferred_element_type=F32) for s in range(0, tb, tt)], axis=0)
    to3 = lambda a: a.reshape(nc, L, gw)
    cum, lw = to3(cum2), to3(lw2)
    r, k, v, kk, b = (to3(ref[...]) for ref in (r_ref, k_ref, v_ref, kk_ref, b_ref))
    cum_l = cum[:, L - 1:L, :]
    p_inv = jnp.exp(-cum)
    al = -(kk * jnp.exp(cum - lw))
    bt = b * p_inv
    kt = k * p_inv
    rt = r * jnp.exp(cum)
    dec = jnp.exp(cum_l - cum)

    xs = jnp.concatenate([al, rt], axis=1)
    ys = jnp.concatenate([stack(bt, head_masks), stack(kt, head_masks)], axis=1)
    res = lax.dot_general(xs.astype(BF16), ys.astype(BF16), (((2,), (2,)), ((0,), (0,))),
                          preferred_element_type=F32)
    a_ab = jnp.where(strict, res[:, :L, :n], 0.0)
    a_ak = jnp.where(strict, res[:, :L, n:], 0.0)
    a_rb = jnp.where(incl, res[:, L:, :n], 0.0)
    a_rk = jnp.where(incl, res[:, L:, n:], 0.0)

    pw = a_ab
    tinv = eye_cat + pw
    pw = _bmm(pw, stack(pw, col_masks))
    sq = 2
    while 2 * sq < L:
        both = _bmm(jnp.concatenate([pw, tinv], axis=1), stack(pw, col_masks))
        pw, tinv = both[:, :L], tinv + both[:, L:]
        sq *= 2
    tinv = tinv + _bmm(tinv, stack(pw, col_masks))

    v_st = stack(v, head_masks)
    al_p = _bmm(tinv, stack(al, head_masks))
    u0 = _bmm(tinv, stack(_bmm(a_ak, v_st), head_masks))
    rt_p = rt + _bmm(a_rb, stack(al_p, head_masks))
    y0 = _bmm(a_rb, stack(u0, head_masks)) + _bmm(a_rk, v_st)

    to2 = lambda a: a.reshape(tb, gw)
    al_out[...] = to2(al_p).astype(BF16)
    rt_out[...] = to2(rt_p).astype(BF16)
    u0_out[...] = to2(u0)
    y0_out[...] = to2(y0)
    bh_out[...] = to2(b * dec).astype(BF16)
    kh_out[...] = to2(k * dec).astype(BF16)
    pl_out[...] = jnp.exp(cum_l)


def _wkv_local(r, lw, k, v, kk, b, tb):
    t, c = r.shape
    gw = WKV_GROUP
    nc = tb // WKV_CHUNK
    spec = lambda: pl.BlockSpec((tb, gw), lambda i, g: (i, g))
    f32 = jax.ShapeDtypeStruct((t, c), F32)
    bf16 = jax.ShapeDtypeStruct((t, c), BF16)
    return pl.pallas_call(
        _wkv_local_kernel,
        grid=(t // tb, c // gw),
        in_specs=[spec() for _ in range(6)],
        out_specs=[spec() for _ in range(6)] + [pl.BlockSpec((nc, 1, gw), lambda i, g: (i, 0, g))],
        out_shape=[bf16, bf16, f32, f32, bf16, bf16,
                   jax.ShapeDtypeStruct((t // WKV_CHUNK, 1, c), F32)],
        compiler_params=_params("parallel", "parallel"),
    )(r, lw, k, v, kk, b)


def _wkv_state_kernel(al_ref, rt_ref, u0_ref, y0_ref, bh_ref, kh_ref, v_ref, pl_ref, y_ref, sv_ref):
    tb, c = v_ref.shape
    L = WKV_CHUNK
    gw = WKV_GROUP
    groups = range(c // gw)

    @pl.when(pl.program_id(0) == 0)
    def _():
        sv_ref[...] = jnp.zeros_like(sv_ref)

    bd_mask = (lax.broadcasted_iota(jnp.int32, (gw, gw), 0) // HEAD_DIM
               == lax.broadcasted_iota(jnp.int32, (gw, gw), 1) // HEAD_DIM)

    def chunk(ci, carry):
        rows = pl.ds(pl.multiple_of(ci * L, L), L)
        lanes = [slice(g * gw, (g + 1) * gw) for g in groups]
        svs = [sv_ref[g] for g in groups]
        uys = [_nt_dot(jnp.concatenate([al_ref[rows, lanes[g]], rt_ref[rows, lanes[g]]], axis=0), svs[g])
               for g in groups]
        us = [uys[g][:L] + u0_ref[rows, lanes[g]] for g in groups]
        for g in groups:
            y_ref[rows, lanes[g]] = uys[g][L:] + y0_ref[rows, lanes[g]]
        upds = [_tn_dot(jnp.concatenate([us[g].astype(BF16), v_ref[rows, lanes[g]].astype(BF16)], axis=0),
                        jnp.concatenate([bh_ref[rows, lanes[g]], kh_ref[rows, lanes[g]]], axis=0))
                for g in groups]
        for g in groups:
            sv_ref[g] = svs[g] * pl_ref[ci, :, lanes[g]] + jnp.where(bd_mask, upds[g], 0.0)
        return carry

    lax.fori_loop(0, tb // L, chunk, 0)


def _wkv_state(al_p, rt_p, u0, y0, bh, kh, v, p_l, tb):
    t, c = v.shape
    nc = tb // WKV_CHUNK
    row = lambda: pl.BlockSpec((tb, c), lambda i: (i, 0))
    return pl.pallas_call(
        _wkv_state_kernel,
        grid=(t // tb,),
        in_specs=[row() for _ in range(7)] + [pl.BlockSpec((nc, 1, c), lambda i: (i, 0, 0))],
        out_specs=row(),
        out_shape=jax.ShapeDtypeStruct((t, c), F32),
        scratch_shapes=[pltpu.VMEM((c // WKV_GROUP, WKV_GROUP, WKV_GROUP), F32)],
        compiler_params=_params("arbitrary"),
    )(al_p, rt_p, u0, y0, bh, kh, v, p_l)


def _wkv_scan(r, lw, k, v, kk, b, tb_local, tb_state):
    al_p, rt_p, u0, y0, bh, kh, p_l = _wkv_local(r, lw, k, v, kk, b, tb_local)
    return _wkv_state(al_p, rt_p, u0, y0, bh, kh, v, p_l, tb_state)


def _rwkv_out_kernel(y_ref, g_ref, bonus_ref, x_ref, gng_ref, gnb_ref, wo_ref, lng_ref, lnb_ref,
                     hs_ref, o_ref):
    hs = hs_ref[...]
    y = y_ref[...]
    mean = _head_sum(y, hs) * (1.0 / HEAD_DIM)
    d = y - mean
    var = _head_sum(d * d, hs) * (1.0 / HEAD_DIM)
    yn = d * lax.rsqrt(var + GN_EPS) * gng_ref[...] + gnb_ref[...]
    z = (yn + bonus_ref[...]) * g_ref[...]
    h = _bdot(z, wo_ref[...])
    o_ref[...] = _layer_norm(ALPHA * x_ref[...] + h, lng_ref[...], lnb_ref[...])


def _rwkv_out(y, g, bonus, x, gn_g, gn_b, w_o, ln_g, ln_b, hs, tm):
    t, c = x.shape
    row = lambda: pl.BlockSpec((tm, c), lambda i: (i, 0))
    consts = [gn_g, gn_b, w_o, ln_g, ln_b, hs]
    return pl.pallas_call(
        _rwkv_out_kernel,
        grid=(t // tm,),
        in_specs=[row() for _ in range(4)] + [_full(a.shape) for a in consts],
        out_specs=row(),
        out_shape=jax.ShapeDtypeStruct((t, c), F32),
        compiler_params=_params("parallel"),
    )(y, g, bonus, x, *consts)


def _ffn_kernel(x_ref, win_ref, wdown_ref, lng_ref, lnb_ref, o_ref, *, n_split):
    x = x_ref[...]
    xb = x.astype(BF16)
    f = wdown_ref.shape[0]
    fc = f // n_split
    acc = jnp.zeros(x.shape, F32)
    for s in range(n_split):
        gate = jnp.dot(xb, win_ref[:, s * fc:(s + 1) * fc], preferred_element_type=F32)
        up = jnp.dot(xb, win_ref[:, f + s * fc:f + (s + 1) * fc], preferred_element_type=F32)
        act = (gate * jax.nn.sigmoid(gate) * up).astype(BF16)
        acc = acc + jnp.dot(act, wdown_ref[s * fc:(s + 1) * fc, :], preferred_element_type=F32)
    o_ref[...] = _layer_norm(ALPHA * x + acc, lng_ref[...], lnb_ref[...])


def _ffn(x, w_in, w_down, ln_g, ln_b, tm):
    t, c = x.shape
    f = w_down.shape[0]
    n_split = 2 if (f // 2) % LANES == 0 else 1
    row = lambda: pl.BlockSpec((tm, c), lambda i: (i, 0))
    consts = [w_in, w_down, ln_g, ln_b]
    return pl.pallas_call(
        functools.partial(_ffn_kernel, n_split=n_split),
        grid=(t // tm,),
        in_specs=[row()] + [_full(a.shape) for a in consts],
        out_specs=row(),
        out_shape=jax.ShapeDtypeStruct((t, c), F32),
        compiler_params=_params("parallel"),
    )(x, *consts)


def _qkv_kernel(x_ref, wqt_ref, wk_ref, wvt_ref, cos_t_ref, sin_t_ref, cos_k_ref, sin_k_ref,
                qt_out, k_out, vt_out, kmean_out):
    xb = x_ref[...].astype(BF16)
    tm, c = xb.shape
    nh = c // HEAD_DIM
    half = HEAD_DIM // 2

    qt = _nt_dot(wqt_ref[...], xb)
    cos_t, sin_t = cos_t_ref[...], sin_t_ref[...]
    scale = HEAD_DIM ** -0.5 * LOG2_E
    for h in range(nh):
        q1 = qt[h * HEAD_DIM:h * HEAD_DIM + half]
        q2 = qt[h * HEAD_DIM + half:(h + 1) * HEAD_DIM]
        qt_out[h, 0:half, :] = ((q1 * cos_t - q2 * sin_t) * scale).astype(BF16)
        qt_out[h, half:HEAD_DIM, :] = ((q2 * cos_t + q1 * sin_t) * scale).astype(BF16)

    vt = _nt_dot(wvt_ref[...], xb)
    for h in range(nh):
        vt_out[h] = vt[h * HEAD_DIM:(h + 1) * HEAD_DIM].astype(BF16)

    k = jnp.dot(xb, wk_ref[...], preferred_element_type=F32)
    reps = c // cos_k_ref.shape[1]
    cos_k = jnp.concatenate([cos_k_ref[...]] * reps, axis=1)
    sin_k = jnp.concatenate([sin_k_ref[...]] * reps, axis=1)
    lane = lax.broadcasted_iota(jnp.int32, k.shape, 1)
    first = (lane % HEAD_DIM) < half
    rot = jnp.where(first, pltpu.roll(k, c - half, 1), pltpu.roll(k, half, 1))
    k = k * cos_k + rot * sin_k
    for h in range(nh):
        k_out[h] = k[:, h * HEAD_DIM:(h + 1) * HEAD_DIM].astype(BF16)
    for blk in range(tm // MOBA_BLOCK):
        kmean_out[blk] = jnp.mean(k[blk * MOBA_BLOCK:(blk + 1) * MOBA_BLOCK], axis=0, keepdims=True)


def _qkv_rope(x, wq_t, wk, wv_t, cos_t, sin_t, cos_k, sin_k, tm):
    t, c = x.shape
    nh = c // HEAD_DIM
    half = HEAD_DIM // 2
    nb = t // MOBA_BLOCK
    return pl.pallas_call(
        _qkv_kernel,
        grid=(t // tm,),
        in_specs=[pl.BlockSpec((tm, c), lambda i: (i, 0)),
                  _full(wq_t.shape), _full(wk.shape), _full(wv_t.shape),
                  pl.BlockSpec((half, tm), lambda i: (0, i)),
                  pl.BlockSpec((half, tm), lambda i: (0, i)),
                  pl.BlockSpec((tm, cos_k.shape[1]), lambda i: (i, 0)),
                  pl.BlockSpec((tm, sin_k.shape[1]), lambda i: (i, 0))],
        out_specs=[pl.BlockSpec((nh, HEAD_DIM, tm), lambda i: (0, 0, i)),
                   pl.BlockSpec((nh, tm, HEAD_DIM), lambda i: (0, i, 0)),
                   pl.BlockSpec((nh, HEAD_DIM, tm), lambda i: (0, 0, i)),
                   pl.BlockSpec((tm // MOBA_BLOCK, 1, c), lambda i: (i, 0, 0))],
        out_shape=[jax.ShapeDtypeStruct((nh, HEAD_DIM, t), BF16),
                   jax.ShapeDtypeStruct((nh, t, HEAD_DIM), BF16),
                   jax.ShapeDtypeStruct((nh, HEAD_DIM, t), BF16),
                   jax.ShapeDtypeStruct((nb, 1, c), F32)],
        compiler_params=_params("parallel"),
    )(x, wq_t, wk, wv_t, cos_t, sin_t, cos_k, sin_k)


def _moba_kernel(qt_ref, k_ref, vt_ref, kmean_ref, o_ref, sel_ref, s_ref):
    i = pl.program_id(1)
    hb, _, bq = qt_ref.shape
    nb = kmean_ref.shape[1]
    blk = MOBA_BLOCK

    qts = [qt_ref[h] for h in range(hb)]
    heads = range(hb)
    ones = jnp.ones((16, blk), BF16)

    def select_blocks():
        bidx = lax.broadcasted_iota(jnp.int32, (nb, bq), 0)
        gates = []
        for h in heads:
            km_hi, km_lo = _split2(kmean_ref[h])
            gate = (jnp.dot(km_hi, qts[h], preferred_element_type=F32)
                    + jnp.dot(km_lo, qts[h], preferred_element_type=F32))
            gates.append(jnp.where(bidx < i, gate, -jnp.inf))
        sels = [jnp.zeros((nb, bq), F32) for _ in heads]
        for _ in range(MOBA_TOPK):
            ms = [jnp.max(g, axis=0, keepdims=True) for g in gates]
            firsts = [jnp.min(jnp.where(jnp.logical_and(gates[h] == ms[h], ms[h] > -jnp.inf), bidx, nb),
                              axis=0, keepdims=True) for h in heads]
            picks = [bidx == f for f in firsts]
            sels = [jnp.where(picks[h], 1.0, sels[h]) for h in heads]
            gates = [jnp.where(picks[h], -jnp.inf, gates[h]) for h in heads]
        for h in heads:
            sel_ref[h] = sels[h]

    def put_scores(j, slot):
        koff = pl.multiple_of(j * blk, blk)
        for h in heads:
            s_ref[slot, h] = jnp.dot(k_ref[h, pl.ds(koff, blk), :], qts[h],
                                     preferred_element_type=F32)

    def attend(j, slot, carry, causal):
        koff = pl.multiple_of(j * blk, blk)
        ss = [s_ref[slot, h] for h in heads]
        if causal:
            key = lax.broadcasted_iota(jnp.int32, (blk, bq), 0)
            qry = lax.broadcasted_iota(jnp.int32, (blk, bq), 1)
            ss = [jnp.where(key <= qry, s, -jnp.inf) for s in ss]
        else:
            ons = [sel_ref[h, pl.ds(j, 1), :] > 0.0 for h in heads]
        cmaxs = [jnp.max(s, axis=0, keepdims=True) for s in ss]
        ms = [carry[h][0] for h in heads]
        if causal:
            m_news = [jnp.maximum(ms[h], cmaxs[h]) for h in heads]
            m_ps = m_news
        else:
            m_news = [jnp.where(ons[h], jnp.maximum(ms[h], cmaxs[h]), ms[h]) for h in heads]
            m_ps = [jnp.where(ons[h], m_news[h], cmaxs[h]) for h in heads]
        ps = [jnp.exp2(ss[h] - m_ps[h]).astype(BF16) for h in heads]
        os = [jnp.dot(jnp.concatenate([vt_ref[h, :, pl.ds(koff, blk)], ones], axis=0), ps[h],
                      preferred_element_type=F32) for h in heads]
        if not causal:
            os = [jnp.where(ons[h], os[h], 0.0) for h in heads]
        return tuple((m_news[h], carry[h][1] * jnp.exp2(ms[h] - m_news[h]) + os[h]) for h in heads)

    init = tuple((jnp.full((1, bq), NEG_BIG, F32), jnp.zeros((HEAD_DIM + 16, bq), F32)) for _ in heads)

    def blocks(j0, count, carry):
        for d in range(count):
            put_scores(j0 + d + 1, (d + 1) % 2)
            carry = attend(j0 + d, d % 2, carry, False)
        return carry

    def finish(carry):
        for h in heads:
            acc = carry[h][1]
            o_ref[h] = acc[:HEAD_DIM] / acc[HEAD_DIM:HEAD_DIM + 1]

    unroll = MOBA_BLOCKS_PER_ITER
    put_scores(0, 0)
    select_blocks()
    carry = lax.fori_loop(0, i // unroll, lambda jj, c: blocks(unroll * jj, unroll, c), init)
    done = (i // unroll) * unroll
    carry = lax.fori_loop(0, (i - done) // 2, lambda jj, c: blocks(done + 2 * jj, 2, c), carry)

    @pl.when(i % 2 == 0)
    def _():
        finish(attend(i, 0, carry, True))

    @pl.when(i % 2 == 1)
    def _():
        put_scores(i, 1)
        finish(attend(i, 1, attend(i - 1, 0, carry, False), True))


def _moba_attention(qt, k, vt, kmean, hb):
    nh, dh, t = qt.shape
    nb = t // MOBA_BLOCK
    once = pl.Buffered(1)
    return pl.pallas_call(
        _moba_kernel,
        grid=(nh // hb, nb),
        in_specs=[pl.BlockSpec((hb, dh, MOBA_BLOCK), lambda g, i: (g, 0, i)),
                  pl.BlockSpec((hb, t, dh), lambda g, i: (g, 0, 0), pipeline_mode=once),
                  pl.BlockSpec((hb, dh, t), lambda g, i: (g, 0, 0), pipeline_mode=once),
                  pl.BlockSpec((hb, nb, dh), lambda g, i: (g, 0, 0))],
        out_specs=pl.BlockSpec((hb, dh, MOBA_BLOCK), lambda g, i: (g, 0, i)),
        out_shape=jax.ShapeDtypeStruct((nh, dh, t), F32),
        scratch_shapes=[pltpu.VMEM((hb, nb, MOBA_BLOCK), F32),
                        pltpu.VMEM((2, hb, MOBA_BLOCK, MOBA_BLOCK), F32)],
        compiler_params=_params("parallel", "arbitrary"),
    )(qt, k, vt, kmean)


def _attn_out_kernel(at_ref, x_ref, wo_ref, lng_ref, lnb_ref, o_ref):
    h = _tn_dot(at_ref[...], wo_ref[...])
    o_ref[...] = _layer_norm(ALPHA * x_ref[...] + h, lng_ref[...], lnb_ref[...])


def _attn_out(at, x, w_o, ln_g, ln_b, tm):
    t, c = x.shape
    consts = [w_o, ln_g, ln_b]
    return pl.pallas_call(
        _attn_out_kernel,
        grid=(t // tm,),
        in_specs=[pl.BlockSpec((c, tm), lambda i: (0, i)),
                  pl.BlockSpec((tm, c), lambda i: (i, 0))] + [_full(a.shape) for a in consts],
        out_specs=pl.BlockSpec((tm, c), lambda i: (i, 0)),
        out_shape=jax.ShapeDtypeStruct((t, c), F32),
        compiler_params=_params("parallel"),
    )(at, x, *consts)


def _row(a):
    return a.reshape(1, -1).astype(F32)


def _rwkv_layer(x, p, ln_g, ln_b, hs, tm):
    r, lw, k, v, kk, b, g, bonus = _rwkv_proj(x, p, hs, tm)
    t = x.shape[0]
    y = _wkv_scan(r, lw, k, v, kk, b, tb_local=min(512, t), tb_state=min(256, t))
    return _rwkv_out(y, g, bonus, x, p['gn_g'], p['gn_b'], p['w_o'], ln_g, ln_b, hs, tm)


def _rope_tables(t):
    half = HEAD_DIM // 2
    inv = ROPE_THETA ** (-jnp.arange(0, HEAD_DIM, 2, dtype=F32) / HEAD_DIM)
    ang = jnp.arange(t, dtype=F32)[:, None] * inv[None, :]
    cos, sin = jnp.cos(ang), jnp.sin(ang)
    reps = LANES // HEAD_DIM
    cos_k = jnp.tile(jnp.concatenate([cos, cos], axis=1), (1, reps))
    sin_k = jnp.tile(jnp.concatenate([-sin, sin], axis=1), (1, reps))
    del half
    return cos.T, sin.T, cos_k, sin_k


def _moba_layer(x, w_qkv, w_o, ln_g, ln_b, tm):
    t, c = x.shape
    nh = c // HEAD_DIM
    wq_t = w_qkv[:, :c].T.astype(BF16)
    wk = w_qkv[:, c:2 * c].astype(BF16)
    wv_t = w_qkv[:, 2 * c:].T.astype(BF16)
    cos_t, sin_t, cos_k, sin_k = _rope_tables(t)
    qt, k, vt, kmean = _qkv_rope(x, wq_t, wk, wv_t, cos_t, sin_t, cos_k, sin_k, tm)
    kmean = jnp.transpose(kmean.reshape(t // MOBA_BLOCK, nh, HEAD_DIM), (1, 0, 2))
    at = _moba_attention(qt, k, vt, kmean, hb=min(MOBA_HEADS_PER_STEP, nh))
    return _attn_out(at.reshape(c, t), x, w_o.astype(BF16), ln_g, ln_b, tm)


def kernel(x, rwkv_mu, rwkv_w_rkv, rwkv_w0, rwkv_w1, rwkv_w2, rwkv_a0, rwkv_a1, rwkv_a2, rwkv_g1, rwkv_g2,
           rwkv_k_k, rwkv_k_a, rwkv_r_k, rwkv_gn_g, rwkv_gn_b, rwkv_w_o, moba_w_qkv, moba_w_o,
           ffn_w_in, ffn_w_down, ln_mix_g, ln_mix_b, ln_ffn_g, ln_ffn_b):
    bsz, t, c = x.shape
    assert c % WKV_GROUP == 0 and t % MOBA_BLOCK == 0
    tm = min(512, t)
    hs = _head_ones()
    outs = []
    for bi in range(bsz):
        h = x[bi]
        for i in range(DEPTH):
            j = i // 2
            if i % 2 == 0:
                p = dict(mu=rwkv_mu[j], w_rkv=rwkv_w_rkv[j].astype(BF16), w0=_row(rwkv_w0[j]),
                         w1=rwkv_w1[j].astype(BF16), w2=rwkv_w2[j].astype(BF16), a0=_row(rwkv_a0[j]),
                         a1=rwkv_a1[j].astype(BF16), a2=rwkv_a2[j].astype(BF16),
                         g1=rwkv_g1[j].astype(BF16), g2=rwkv_g2[j].astype(BF16),
                         k_k=_row(rwkv_k_k[j]), k_a=_row(rwkv_k_a[j]), r_k=_row(rwkv_r_k[j]),
                         gn_g=_row(rwkv_gn_g[j]), gn_b=_row(rwkv_gn_b[j]), w_o=rwkv_w_o[j].astype(BF16))
                h = _rwkv_layer(h, p, _row(ln_mix_g[i]), _row(ln_mix_b[i]), hs, tm)
            else:
                h = _moba_layer(h, moba_w_qkv[j], moba_w_o[j], _row(ln_mix_g[i]), _row(ln_mix_b[i]), tm)
            h = _ffn(h, ffn_w_in[i].astype(BF16), ffn_w_down[i].astype(BF16),
                     _row(ln_ffn_g[i]), _row(ln_ffn_b[i]), tm)
        outs.append(h)
    return jnp.stack(outs, axis=0)
```

```python
import functools

import jax
import jax.numpy as jnp
from jax import lax
from jax.experimental import pallas as pl
from jax.experimental.pallas import tpu as pltpu

F32 = jnp.float32
BF16 = jnp.bfloat16

HEAD_DIM = 64
DEPTH = 2
ALPHA = (2 * DEPTH) ** 0.25
LN_EPS = 1e-5
GN_EPS = HEAD_DIM * 1e-5
MOBA_BLOCK = 256
MOBA_TOPK = 3
ROPE_THETA = 10000.0
LOG2_E = 1.4426950408889634
MOBA_HEADS_PER_STEP = 4
MOBA_BLOCKS_PER_ITER = 8

LANES = 128
MXU_DIM = 256
WKV_CHUNK = 64
WKV_GROUP = MXU_DIM
NEG_BIG = -1e30
VMEM_LIMIT = 56 * 1024 * 1024


def _params(*sem):
    return pltpu.CompilerParams(dimension_semantics=sem, vmem_limit_bytes=VMEM_LIMIT)


def _full(shape):
    n = len(shape)
    return pl.BlockSpec(shape, lambda *_: (0,) * n)


def _bdot(a, b):
    return jnp.dot(a.astype(BF16), b.astype(BF16), preferred_element_type=F32)


def _nt_dot(a, b):
    return lax.dot_general(a.astype(BF16), b.astype(BF16), (((1,), (1,)), ((), ())),
                           preferred_element_type=F32)


def _tn_dot(a, b):
    return lax.dot_general(a.astype(BF16), b.astype(BF16), (((0,), (0,)), ((), ())),
                           preferred_element_type=F32)


def _split2(a):
    hi = a.astype(BF16)
    lo = (a - hi.astype(F32)).astype(BF16)
    return hi, lo


def _dot_exact_rhs(a, b_exact):
    if a.dtype == BF16:
        return jnp.dot(a, b_exact, preferred_element_type=F32)
    hi, lo = _split2(a)
    return (jnp.dot(hi, b_exact, preferred_element_type=F32)
            + jnp.dot(lo, b_exact, preferred_element_type=F32))


def _head_sum(a, ones_bd):
    w = ones_bd.shape[0]
    return jnp.concatenate([_dot_exact_rhs(a[:, s:s + w], ones_bd) for s in range(0, a.shape[1], w)],
                           axis=1)


def _layer_norm(z, g, b):
    mu = jnp.mean(z, axis=-1, keepdims=True)
    d = z - mu
    var = jnp.mean(d * d, axis=-1, keepdims=True)
    return d * lax.rsqrt(var + LN_EPS) * g + b


def _head_ones():
    h = jnp.arange(MXU_DIM) // HEAD_DIM
    return (h[:, None] == h[None, :]).astype(BF16)


def _rwkv_proj_kernel(x_ref, xp_ref, mu_ref, wrkv_ref, w0_ref, w1_ref, w2_ref, a0_ref, a1_ref, a2_ref,
                      g1_ref, g2_ref, kk_w_ref, ka_w_ref, rk_w_ref, hs_ref,
                      r_out, lw_out, k_out, v_out, kk_out, b_out, g_out, bonus_out):
    i = pl.program_id(0)
    x = x_ref[...]
    prev_last = jnp.where(i > 0, xp_ref[7:8, :], 0.0)
    row = lax.broadcasted_iota(jnp.int32, x.shape, 0)
    shifted = jnp.where(row == 0, prev_last, pltpu.roll(x, 1, 0))
    xx = shifted - x
    mix = lambda n: x + xx * mu_ref[n:n + 1, :]
    hs = hs_ref[...]

    r = _bdot(mix(0), wrkv_ref[0])
    k = _bdot(mix(2), wrkv_ref[1])
    v = _bdot(mix(3), wrkv_ref[2])
    zw = w0_ref[...] + _bdot(jnp.tanh(_bdot(mix(1), w1_ref[...])), w2_ref[...])
    u = -zw
    w_log = -(jnp.maximum(u, 0.0) + jnp.log(1.0 + jnp.exp(-jnp.abs(u)))) - 0.5
    a = jax.nn.sigmoid(a0_ref[...] + _bdot(_bdot(mix(4), a1_ref[...]), a2_ref[...]))
    g = _bdot(jax.nn.sigmoid(_bdot(mix(5), g1_ref[...])), g2_ref[...])

    kk = k * kk_w_ref[...]
    norm = jnp.sqrt(_head_sum(kk * kk, hs))
    kk = kk / jnp.maximum(norm, 1e-12)
    k = k * (1.0 + (a - 1.0) * ka_w_ref[...])

    r_out[...] = r.astype(BF16)
    lw_out[...] = -jnp.exp(w_log)
    k_out[...] = k.astype(BF16)
    v_out[...] = v.astype(BF16)
    kk_out[...] = kk.astype(BF16)
    b_out[...] = (kk * a).astype(BF16)
    g_out[...] = g.astype(BF16)
    bonus_out[...] = (_head_sum(r * k * rk_w_ref[...], hs) * v).astype(BF16)


def _rwkv_proj(x, p, hs, tm):
    t, c = x.shape
    row = lambda: pl.BlockSpec((tm, c), lambda i: (i, 0))
    prev = pl.BlockSpec((8, c), lambda i: (jnp.maximum(i * (tm // 8) - 1, 0), 0))
    ins = [x, x, p['mu'], p['w_rkv'], p['w0'], p['w1'], p['w2'], p['a0'], p['a1'], p['a2'],
           p['g1'], p['g2'], p['k_k'], p['k_a'], p['r_k'], hs]
    in_specs = [row(), prev] + [_full(a.shape) for a in ins[2:]]
    f32 = jax.ShapeDtypeStruct((t, c), F32)
    bf16 = jax.ShapeDtypeStruct((t, c), BF16)
    return pl.pallas_call(
        _rwkv_proj_kernel,
        grid=(t // tm,),
        in_specs=in_specs,
        out_specs=[row() for _ in range(8)],
        out_shape=[bf16, f32, bf16, bf16, bf16, bf16, bf16, bf16],
        compiler_params=_params("parallel"),
    )(*ins)


def _bmm(a, b):
    return lax.dot_general(a.astype(BF16), b.astype(BF16), (((2,), (1,)), ((0,), (0,))),
                           preferred_element_type=F32)


def _wkv_local_kernel(r_ref, lw_ref, k_ref, v_ref, kk_ref, b_ref,
                      al_out, rt_out, u0_out, y0_out, bh_out, kh_out, pl_out):
    tb, gw = r_ref.shape
    L = WKV_CHUNK
    nc = tb // L
    nh = gw // HEAD_DIM
    n = nh * L

    lane = lax.broadcasted_iota(jnp.int32, (1, 1, gw), 2)
    head_masks = [lane // HEAD_DIM == h for h in range(nh)]
    col = lax.broadcasted_iota(jnp.int32, (1, 1, n), 2)
    col_masks = [col // L == h for h in range(nh)]
    ri = lax.broadcasted_iota(jnp.int32, (1, L, n), 1)
    ci = lax.broadcasted_iota(jnp.int32, (1, L, n), 2) % L
    strict = ci < ri
    incl = ci <= ri
    eye_cat = (ci == ri).astype(F32)
    tt = min(tb, MXU_DIM)
    rb = lax.broadcasted_iota(jnp.int32, (tt, tt), 0)
    cb = lax.broadcasted_iota(jnp.int32, (tt, tt), 1)
    tri = jnp.logical_and(cb <= rb, cb // L == rb // L).astype(BF16)

    def stack(a, masks):
        a16 = a.astype(BF16)
        return jnp.concatenate([jnp.where(m, a16, 0) for m in masks], axis=1)

    lw2 = lw_ref[...]
    hi, lo = _split2(lw2)
    cum2 = jnp.concatenate(
        [jnp.dot(tri, hi[s:s + tt], preferred_element_type=F32)
         + jnp.dot(tri, lo[s:s + tt], preferred_element_type=F32) for s in range(0, tb, tt)], axis=0)
    to3 = lambda a: a.reshape(nc, L, gw)
    cum, lw = to3(cum2), to3(lw2)
    r, k, v, kk, b = (to3(ref[...]) for ref in (r_ref, k_ref, v_ref, kk_ref, b_ref))
    cum_l = cum[:, L - 1:L, :]
    p_inv = jnp.exp(-cum)
    al = -(kk * jnp.exp(cum - lw))
    bt = b * p_inv
    kt = k * p_inv
    rt = r * jnp.exp(cum)
    dec = jnp.exp(cum_l - cum)

    xs = jnp.concatenate([al, rt], axis=1)
    ys = jnp.concatenate([stack(bt, head_masks), stack(kt, head_masks)], axis=1)
    res = lax.dot_general(xs.astype(BF16), ys.astype(BF16), (((2,), (2,)), ((0,), (0,))),
                          preferred_element_type=F32)
    a_ab = jnp.where(strict, res[:, :L, :n], 0.0)
    a_ak = jnp.where(strict, res[:, :L, n:], 0.0)
    a_rb = jnp.where(incl, res[:, L:, :n], 0.0)
    a_rk = jnp.where(incl, res[:, L:, n:], 0.0)

    pw = a_ab
    tinv = eye_cat + pw
    pw = _bmm(pw, stack(pw, col_masks))
    sq = 2
    while 2 * sq < L:
        both = _bmm(jnp.concatenate([pw, tinv], axis=1), stack(pw, col_masks))
        pw, tinv = both[:, :L], tinv + both[:, L:]
        sq *= 2
    tinv = tinv + _bmm(tinv, stack(pw, col_masks))

    v_st = stack(v, head_masks)
    al_p = _bmm(tinv, stack(al, head_masks))
    u0 = _bmm(tinv, stack(_bmm(a_ak, v_st), head_masks))
    rt_p = rt + _bmm(a_rb, stack(al_p, head_masks))
    y0 = _bmm(a_rb, stack(u0, head_masks)) + _bmm(a_rk, v_st)

    to2 = lambda a: a.reshape(tb, gw)
    al_out[...] = to2(al_p).astype(BF16)
    rt_out[...] = to2(rt_p).astype(BF16)
    u0_out[...] = to2(u0).astype(BF16)
    y0_out[...] = to2(y0).astype(BF16)
    bh_out[...] = to2(b * dec).astype(BF16)
    kh_out[...] = to2(k * dec).astype(BF16)
    pl_out[...] = jnp.exp(cum_l)


def _wkv_local(r, lw, k, v, kk, b, tb):
    t, c = r.shape
    gw = WKV_GROUP
    nc = tb // WKV_CHUNK
    spec = lambda: pl.BlockSpec((tb, gw), lambda i, g: (i, g))
    f32 = jax.ShapeDtypeStruct((t, c), F32)
    bf16 = jax.ShapeDtypeStruct((t, c), BF16)
    return pl.pallas_call(
        _wkv_local_kernel,
        grid=(t // tb, c // gw),
        in_specs=[spec() for _ in range(6)],
        out_specs=[spec() for _ in range(6)] + [pl.BlockSpec((nc, 1, gw), lambda i, g: (i, 0, g))],
        out_shape=[bf16, bf16, bf16, bf16, bf16, bf16,
                   jax.ShapeDtypeStruct((t // WKV_CHUNK, 1, c), F32)],
        compiler_params=_params("parallel", "parallel"),
    )(r, lw, k, v, kk, b)


def _wkv_state_kernel(al_ref, rt_ref, u0_ref, y0_ref, bh_ref, kh_ref, v_ref, pl_ref, y_ref, sv_ref):
    tb, c = v_ref.shape
    L = WKV_CHUNK
    gw = WKV_GROUP
    groups = range(c // gw)

    @pl.when(pl.program_id(0) == 0)
    def _():
        sv_ref[...] = jnp.zeros_like(sv_ref)

    bd_mask = (lax.broadcasted_iota(jnp.int32, (gw, gw), 0) // HEAD_DIM
               == lax.broadcasted_iota(jnp.int32, (gw, gw), 1) // HEAD_DIM)

    def chunk(ci, carry):
        rows = pl.ds(pl.multiple_of(ci * L, L), L)
        lanes = [slice(g * gw, (g + 1) * gw) for g in groups]
        svs = [sv_ref[g] for g in groups]
        uys = [_nt_dot(jnp.concatenate([al_ref[rows, lanes[g]], rt_ref[rows, lanes[g]]], axis=0), svs[g])
               for g in groups]
        us = [uys[g][:L] + u0_ref[rows, lanes[g]] for g in groups]
        for g in groups:
            y_ref[rows, lanes[g]] = (uys[g][L:] + y0_ref[rows, lanes[g]]).astype(BF16)
        upds = [_tn_dot(jnp.concatenate([us[g].astype(BF16), v_ref[rows, lanes[g]].astype(BF16)], axis=0),
                        jnp.concatenate([bh_ref[rows, lanes[g]], kh_ref[rows, lanes[g]]], axis=0))
                for g in groups]
        for g in groups:
            sv_ref[g] = svs[g] * pl_ref[ci, :, lanes[g]] + jnp.where(bd_mask, upds[g], 0.0)
        return carry

    lax.fori_loop(0, tb // L, chunk, 0)


def _wkv_state(al_p, rt_p, u0, y0, bh, kh, v, p_l, tb):
    t, c = v.shape
    nc = tb // WKV_CHUNK
    row = lambda: pl.BlockSpec((tb, c), lambda i: (i, 0))
    return pl.pallas_call(
        _wkv_state_kernel,
        grid=(t // tb,),
        in_specs=[row() for _ in range(7)] + [pl.BlockSpec((nc, 1, c), lambda i: (i, 0, 0))],
        out_specs=row(),
        out_shape=jax.ShapeDtypeStruct((t, c), BF16),
        scratch_shapes=[pltpu.VMEM((c // WKV_GROUP, WKV_GROUP, WKV_GROUP), F32)],
        compiler_params=_params("arbitrary"),
    )(al_p, rt_p, u0, y0, bh, kh, v, p_l)


def _wkv_scan(r, lw, k, v, kk, b, tb_local, tb_state):
    al_p, rt_p, u0, y0, bh, kh, p_l = _wkv_local(r, lw, k, v, kk, b, tb_local)
    return _wkv_state(al_p, rt_p, u0, y0, bh, kh, v, p_l, tb_state)


def _rwkv_out_kernel(y_ref, g_ref, bonus_ref, x_ref, gng_ref, gnb_ref, wo_ref, lng_ref, lnb_ref,
                     hs_ref, o_ref):
    hs = hs_ref[...]
    y = y_ref[...]
    mean = _head_sum(y, hs) * (1.0 / HEAD_DIM)
    d = y - mean
    var = _head_sum(d * d, hs) * (1.0 / HEAD_DIM)
    yn = d * lax.rsqrt(var + GN_EPS) * gng_ref[...] + gnb_ref[...]
    z = (yn + bonus_ref[...]) * g_ref[...]
    h = _bdot(z, wo_ref[...])
    o_ref[...] = _layer_norm(ALPHA * x_ref[...] + h, lng_ref[...], lnb_ref[...])


def _rwkv_out(y, g, bonus, x, gn_g, gn_b, w_o, ln_g, ln_b, hs, tm):
    t, c = x.shape
    row = lambda: pl.BlockSpec((tm, c), lambda i: (i, 0))
    consts = [gn_g, gn_b, w_o, ln_g, ln_b, hs]
    return pl.pallas_call(
        _rwkv_out_kernel,
        grid=(t // tm,),
        in_specs=[row() for _ in range(4)] + [_full(a.shape) for a in consts],
        out_specs=row(),
        out_shape=jax.ShapeDtypeStruct((t, c), F32),
        compiler_params=_params("parallel"),
    )(y, g, bonus, x, *consts)


def _ffn_kernel(x_ref, win_ref, wdown_ref, lng_ref, lnb_ref, o_ref, *, bounds):
    x = x_ref[...]
    xb = x.astype(BF16)
    f = wdown_ref.shape[0]
    acc = jnp.zeros(x.shape, F32)
    for lo, hi in bounds:
        gate = jnp.dot(xb, win_ref[:, lo:hi], preferred_element_type=F32)
        up = jnp.dot(xb, win_ref[:, f + lo:f + hi], preferred_element_type=F32)
        act = (gate * jax.nn.sigmoid(gate) * up).astype(BF16)
        acc = acc + jnp.dot(act, wdown_ref[lo:hi, :], preferred_element_type=F32)
    o_ref[...] = _layer_norm(ALPHA * x + acc, lng_ref[...], lnb_ref[...])


def _ffn(x, w_in, w_down, ln_g, ln_b, tm):
    t, c = x.shape
    f = w_down.shape[0]
    mid = min(f, -(-(f // MXU_DIM) // 2) * MXU_DIM) if f % MXU_DIM == 0 else f
    bounds = tuple(b for b in ((0, mid), (mid, f)) if b[1] > b[0])
    row = lambda: pl.BlockSpec((tm, c), lambda i: (i, 0))
    consts = [w_in, w_down, ln_g, ln_b]
    return pl.pallas_call(
        functools.partial(_ffn_kernel, bounds=bounds),
        grid=(t // tm,),
        in_specs=[row()] + [_full(a.shape) for a in consts],
        out_specs=row(),
        out_shape=jax.ShapeDtypeStruct((t, c), F32),
        compiler_params=_params("parallel"),
    )(x, *consts)


def _qkv_kernel(x_ref, wqt_ref, wk_ref, wvt_ref, cos_t_ref, sin_t_ref, cos_k_ref, sin_k_ref,
                qt_out, k_out, vt_out, kmean_out):
    xb = x_ref[...].astype(BF16)
    tm, c = xb.shape
    nh = c // HEAD_DIM
    half = HEAD_DIM // 2

    qt = _nt_dot(wqt_ref[...], xb)
    cos_t, sin_t = cos_t_ref[...], sin_t_ref[...]
    scale = HEAD_DIM ** -0.5 * LOG2_E
    for h in range(nh):
        q1 = qt[h * HEAD_DIM:h * HEAD_DIM + half]
        q2 = qt[h * HEAD_DIM + half:(h + 1) * HEAD_DIM]
        qt_out[h, 0:half, :] = ((q1 * cos_t - q2 * sin_t) * scale).astype(BF16)
        qt_out[h, half:HEAD_DIM, :] = ((q2 * cos_t + q1 * sin_t) * scale).astype(BF16)

    vt = _nt_dot(wvt_ref[...], xb)
    for h in range(nh):
        vt_out[h] = vt[h * HEAD_DIM:(h + 1) * HEAD_DIM].astype(BF16)

    k = jnp.dot(xb, wk_ref[...], preferred_element_type=F32)
    reps = c // cos_k_ref.shape[1]
    cos_k = jnp.concatenate([cos_k_ref[...]] * reps, axis=1)
    sin_k = jnp.concatenate([sin_k_ref[...]] * reps, axis=1)
    lane = lax.broadcasted_iota(jnp.int32, k.shape, 1)
    first = (lane % HEAD_DIM) < half
    rot = jnp.where(first, pltpu.roll(k, c - half, 1), pltpu.roll(k, half, 1))
    k = k * cos_k + rot * sin_k
    for h in range(nh):
        k_out[h] = k[:, h * HEAD_DIM:(h + 1) * HEAD_DIM].astype(BF16)
    for blk in range(tm // MOBA_BLOCK):
        kmean_out[blk] = jnp.mean(k[blk * MOBA_BLOCK:(blk + 1) * MOBA_BLOCK], axis=0, keepdims=True)


def _qkv_rope(x, wq_t, wk, wv_t, cos_t, sin_t, cos_k, sin_k, tm):
    t, c = x.shape
    nh = c // HEAD_DIM
    half = HEAD_DIM // 2
    nb = t // MOBA_BLOCK
    return pl.pallas_call(
        _qkv_kernel,
        grid=(t // tm,),
        in_specs=[pl.BlockSpec((tm, c), lambda i: (i, 0)),
                  _full(wq_t.shape), _full(wk.shape), _full(wv_t.shape),
                  pl.BlockSpec((half, tm), lambda i: (0, i)),
                  pl.BlockSpec((half, tm), lambda i: (0, i)),
                  pl.BlockSpec((tm, cos_k.shape[1]), lambda i: (i, 0)),
                  pl.BlockSpec((tm, sin_k.shape[1]), lambda i: (i, 0))],
        out_specs=[pl.BlockSpec((nh, HEAD_DIM, tm), lambda i: (0, 0, i)),
                   pl.BlockSpec((nh, tm, HEAD_DIM), lambda i: (0, i, 0)),
                   pl.BlockSpec((nh, HEAD_DIM, tm), lambda i: (0, 0, i)),
                   pl.BlockSpec((tm // MOBA_BLOCK, 1, c), lambda i: (i, 0, 0))],
        out_shape=[jax.ShapeDtypeStruct((nh, HEAD_DIM, t), BF16),
                   jax.ShapeDtypeStruct((nh, t, HEAD_DIM), BF16),
                   jax.ShapeDtypeStruct((nh, HEAD_DIM, t), BF16),
                   jax.ShapeDtypeStruct((nb, 1, c), F32)],
        compiler_params=_params("parallel"),
    )(x, wq_t, wk, wv_t, cos_t, sin_t, cos_k, sin_k)


def _moba_kernel(qt_ref, k_ref, vt_ref, kmean_ref, o_ref, sel_ref, s_ref):
    i = pl.program_id(1)
    hb, _, bq = qt_ref.shape
    nb = kmean_ref.shape[1]
    blk = MOBA_BLOCK

    qts = [qt_ref[h] for h in range(hb)]
    heads = range(hb)
    ones = jnp.ones((16, blk), BF16)

    def select_blocks():
        bidx = lax.broadcasted_iota(jnp.int32, (nb, bq), 0)
        gates = []
        for h in heads:
            km_hi, km_lo = _split2(kmean_ref[h])
            gate = (jnp.dot(km_hi, qts[h], preferred_element_type=F32)
                    + jnp.dot(km_lo, qts[h], preferred_element_type=F32))
            gates.append(jnp.where(bidx < i, gate, -jnp.inf))
        sels = [jnp.zeros((nb, bq), F32) for _ in heads]
        for _ in range(MOBA_TOPK):
            ms = [jnp.max(g, axis=0, keepdims=True) for g in gates]
            firsts = [jnp.min(jnp.where(jnp.logical_and(gates[h] == ms[h], ms[h] > -jnp.inf), bidx, nb),
                              axis=0, keepdims=True) for h in heads]
            picks = [bidx == f for f in firsts]
            sels = [jnp.where(picks[h], 1.0, sels[h]) for h in heads]
            gates = [jnp.where(picks[h], -jnp.inf, gates[h]) for h in heads]
        for h in heads:
            sel_ref[h] = sels[h]

    def put_scores(j, slot):
        koff = pl.multiple_of(j * blk, blk)
        for h in heads:
            s_ref[slot, h] = jnp.dot(k_ref[h, pl.ds(koff, blk), :], qts[h],
                                     preferred_element_type=F32)

    def attend(j, slot, carry, causal):
        koff = pl.multiple_of(j * blk, blk)
        ss = [s_ref[slot, h] for h in heads]
        if causal:
            key = lax.broadcasted_iota(jnp.int32, (blk, bq), 0)
            qry = lax.broadcasted_iota(jnp.int32, (blk, bq), 1)
            ss = [jnp.where(key <= qry, s, -jnp.inf) for s in ss]
        else:
            ons = [sel_ref[h, pl.ds(j, 1), :] > 0.0 for h in heads]
        cmaxs = [jnp.max(s, axis=0, keepdims=True) for s in ss]
        ms = [carry[h][0] for h in heads]
        if causal:
            m_news = [jnp.maximum(ms[h], cmaxs[h]) for h in heads]
            m_ps = m_news
        else:
            m_news = [jnp.where(ons[h], jnp.maximum(ms[h], cmaxs[h]), ms[h]) for h in heads]
            m_ps = [jnp.where(ons[h], m_news[h], cmaxs[h]) for h in heads]
        ps = [jnp.exp2(ss[h] - m_ps[h]).astype(BF16) for h in heads]
        os = [jnp.dot(jnp.concatenate([vt_ref[h, :, pl.ds(koff, blk)], ones], axis=0), ps[h],
                      preferred_element_type=F32) for h in heads]
        if not causal:
            os = [jnp.where(ons[h], os[h], 0.0) for h in heads]
        return tuple((m_news[h], carry[h][1] * jnp.exp2(ms[h] - m_news[h]) + os[h]) for h in heads)

    init = tuple((jnp.full((1, bq), NEG_BIG, F32), jnp.zeros((HEAD_DIM + 16, bq), F32)) for _ in heads)

    def blocks(j0, count, carry):
        for d in range(count):
            put_scores(j0 + d + 1, (d + 1) % 2)
            carry = attend(j0 + d, d % 2, carry, False)
        return carry

    def finish(carry):
        for h in heads:
            acc = carry[h][1]
            o_ref[h] = acc[:HEAD_DIM] / acc[HEAD_DIM:HEAD_DIM + 1]

    unroll = MOBA_BLOCKS_PER_ITER
    put_scores(0, 0)
    select_blocks()
    carry = lax.fori_loop(0, i // unroll, lambda jj, c: blocks(unroll * jj, unroll, c), init)
    done = (i // unroll) * unroll
    carry = lax.fori_loop(0, (i - done) // 2, lambda jj, c: blocks(done + 2 * jj, 2, c), carry)

    @pl.when(i % 2 == 0)
    def _():
        finish(attend(i, 0, carry, True))

    @pl.when(i % 2 == 1)
    def _():
        put_scores(i, 1)
        finish(attend(i, 1, attend(i - 1, 0, carry, False), True))


def _moba_attention(qt, k, vt, kmean, hb):
    nh, dh, t = qt.shape
    nb = t // MOBA_BLOCK
    once = pl.Buffered(1)
    return pl.pallas_call(
        _moba_kernel,
        grid=(nh // hb, nb),
        in_specs=[pl.BlockSpec((hb, dh, MOBA_BLOCK), lambda g, i: (g, 0, i)),
                  pl.BlockSpec((hb, t, dh), lambda g, i: (g, 0, 0), pipeline_mode=once),
                  pl.BlockSpec((hb, dh, t), lambda g, i: (g, 0, 0), pipeline_mode=once),
                  pl.BlockSpec((hb, nb, dh), lambda g, i: (g, 0, 0))],
        out_specs=pl.BlockSpec((hb, dh, MOBA_BLOCK), lambda g, i: (g, 0, i)),
        out_shape=jax.ShapeDtypeStruct((nh, dh, t), F32),
        scratch_shapes=[pltpu.VMEM((hb, nb, MOBA_BLOCK), F32),
                        pltpu.VMEM((2, hb, MOBA_BLOCK, MOBA_BLOCK), F32)],
        compiler_params=_params("parallel", "arbitrary"),
    )(qt, k, vt, kmean)


def _attn_out_kernel(at_ref, x_ref, wo_ref, lng_ref, lnb_ref, o_ref):
    h = _tn_dot(at_ref[...], wo_ref[...])
    o_ref[...] = _layer_norm(ALPHA * x_ref[...] + h, lng_ref[...], lnb_ref[...])


def _attn_out(at, x, w_o, ln_g, ln_b, tm):
    t, c = x.shape
    consts = [w_o, ln_g, ln_b]
    return pl.pallas_call(
        _attn_out_kernel,
        grid=(t // tm,),
        in_specs=[pl.BlockSpec((c, tm), lambda i: (0, i)),
                  pl.BlockSpec((tm, c), lambda i: (i, 0))] + [_full(a.shape) for a in consts],
        out_specs=pl.BlockSpec((tm, c), lambda i: (i, 0)),
        out_shape=jax.ShapeDtypeStruct((t, c), F32),
        compiler_params=_params("parallel"),
    )(at, x, *consts)


def _row(a):
    return a.reshape(1, -1).astype(F32)


def _rwkv_layer(x, p, ln_g, ln_b, hs, tm):
    r, lw, k, v, kk, b, g, bonus = _rwkv_proj(x, p, hs, tm)
    t = x.shape[0]
    y = _wkv_scan(r, lw, k, v, kk, b, tb_local=min(512, t), tb_state=min(256, t))
    return _rwkv_out(y, g, bonus, x, p['gn_g'], p['gn_b'], p['w_o'], ln_g, ln_b, hs, tm)


def _rope_tables(t):
    half = HEAD_DIM // 2
    inv = ROPE_THETA ** (-jnp.arange(0, HEAD_DIM, 2, dtype=F32) / HEAD_DIM)
    ang = jnp.arange(t, dtype=F32)[:, None] * inv[None, :]
    cos, sin = jnp.cos(ang), jnp.sin(ang)
    reps = LANES // HEAD_DIM
    cos_k = jnp.tile(jnp.concatenate([cos, cos], axis=1), (1, reps))
    sin_k = jnp.tile(jnp.concatenate([-sin, sin], axis=1), (1, reps))
    del half
    return cos.T, sin.T, cos_k, sin_k


def _moba_layer(x, w_qkv, w_o, ln_g, ln_b, tm):
    t, c = x.shape
    nh = c // HEAD_DIM
    wq_t = w_qkv[:, :c].T.astype(BF16)
    wk = w_qkv[:, c:2 * c].astype(BF16)
    wv_t = w_qkv[:, 2 * c:].T.astype(BF16)
    cos_t, sin_t, cos_k, sin_k = _rope_tables(t)
    qt, k, vt, kmean = _qkv_rope(x, wq_t, wk, wv_t, cos_t, sin_t, cos_k, sin_k, tm)
    kmean = jnp.transpose(kmean.reshape(t // MOBA_BLOCK, nh, HEAD_DIM), (1, 0, 2))
    at = _moba_attention(qt, k, vt, kmean, hb=min(MOBA_HEADS_PER_STEP, nh))
    return _attn_out(at.reshape(c, t), x, w_o.astype(BF16), ln_g, ln_b, tm)


def kernel(x, rwkv_mu, rwkv_w_rkv, rwkv_w0, rwkv_w1, rwkv_w2, rwkv_a0, rwkv_a1, rwkv_a2, rwkv_g1, rwkv_g2,
           rwkv_k_k, rwkv_k_a, rwkv_r_k, rwkv_gn_g, rwkv_gn_b, rwkv_w_o, moba_w_qkv, moba_w_o,
           ffn_w_in, ffn_w_down, ln_mix_g, ln_mix_b, ln_ffn_g, ln_ffn_b):
    bsz, t, c = x.shape
    assert c % WKV_GROUP == 0 and t % MOBA_BLOCK == 0
    tm = min(512, t)
    hs = _head_ones()
    outs = []
    for bi in range(bsz):
        h = x[bi]
        for i in range(DEPTH):
            j = i // 2
            if i % 2 == 0:
                p = dict(mu=rwkv_mu[j], w_rkv=rwkv_w_rkv[j].astype(BF16), w0=_row(rwkv_w0[j]),
                         w1=rwkv_w1[j].astype(BF16), w2=rwkv_w2[j].astype(BF16), a0=_row(rwkv_a0[j]),
                         a1=rwkv_a1[j].astype(BF16), a2=rwkv_a2[j].astype(BF16),
                         g1=rwkv_g1[j].astype(BF16), g2=rwkv_g2[j].astype(BF16),
                         k_k=_row(rwkv_k_k[j]), k_a=_row(rwkv_k_a[j]), r_k=_row(rwkv_r_k[j]),
                         gn_g=_row(rwkv_gn_g[j]), gn_b=_row(rwkv_gn_b[j]), w_o=rwkv_w_o[j].astype(BF16))
                h = _rwkv_layer(h, p, _row(ln_mix_g[i]), _row(ln_mix_b[i]), hs, tm)
            else:
                h = _moba_layer(h, moba_w_qkv[j], moba_w_o[j], _row(ln_mix_g[i]), _row(ln_mix_b[i]), tm)
            h = _ffn(h, ffn_w_in[i].astype(BF16), ffn_w_down[i].astype(BF16),
                     _row(ln_ffn_g[i]), _row(ln_ffn_b[i]), tm)
        outs.append(h)
    return jnp.stack(outs, axis=0)
```

```python
import jax
import jax.numpy as jnp
from jax import lax
from jax.experimental import pallas as pl
from jax.experimental.pallas import tpu as pltpu

F32 = jnp.float32
BF16 = jnp.bfloat16

HEAD_DIM = 64
DEPTH = 2
ALPHA = (2 * DEPTH) ** 0.25
LN_EPS = 1e-5
GN_EPS = HEAD_DIM * 1e-5
MOBA_BLOCK = 256
MOBA_TOPK = 3
ROPE_THETA = 10000.0
LOG2_E = 1.4426950408889634
MOBA_HEADS_PER_STEP = 4
MOBA_BLOCKS_PER_ITER = 8

LANES = 128
MXU_DIM = 256
WKV_CHUNK = 64
WKV_GROUP = MXU_DIM
NEG_BIG = -1e30
VMEM_LIMIT = 56 * 1024 * 1024


def _params(*sem):
    return pltpu.CompilerParams(dimension_semantics=sem, vmem_limit_bytes=VMEM_LIMIT)


def _full(shape):
    n = len(shape)
    return pl.BlockSpec(shape, lambda *_: (0,) * n)


def _bdot(a, b):
    return jnp.dot(a.astype(BF16), b.astype(BF16), preferred_element_type=F32)


def _nt_dot(a, b):
    return lax.dot_general(a.astype(BF16), b.astype(BF16), (((1,), (1,)), ((), ())),
                           preferred_element_type=F32)


def _tn_dot(a, b):
    return lax.dot_general(a.astype(BF16), b.astype(BF16), (((0,), (0,)), ((), ())),
                           preferred_element_type=F32)


def _split2(a):
    hi = a.astype(BF16)
    lo = (a - hi.astype(F32)).astype(BF16)
    return hi, lo


def _dot_exact_rhs(a, b_exact):
    if a.dtype == BF16:
        return jnp.dot(a, b_exact, preferred_element_type=F32)
    hi, lo = _split2(a)
    return (jnp.dot(hi, b_exact, preferred_element_type=F32)
            + jnp.dot(lo, b_exact, preferred_element_type=F32))


def _head_sum(a, ones_bd):
    w = ones_bd.shape[0]
    return jnp.concatenate([_dot_exact_rhs(a[:, s:s + w], ones_bd) for s in range(0, a.shape[1], w)],
                           axis=1)


def _layer_norm(z, g, b):
    mu = jnp.mean(z, axis=-1, keepdims=True)
    d = z - mu
    var = jnp.mean(d * d, axis=-1, keepdims=True)
    return d * lax.rsqrt(var + LN_EPS) * g + b


def _head_ones():
    h = jnp.arange(MXU_DIM) // HEAD_DIM
    return (h[:, None] == h[None, :]).astype(BF16)


def _rwkv_proj_kernel(x_ref, xp_ref, mu_ref, wrkv_ref, w0_ref, w1_ref, w2_ref, a0_ref, a1_ref, a2_ref,
                      g1_ref, g2_ref, kk_w_ref, ka_w_ref, rk_w_ref, hs_ref,
                      r_out, lw_out, k_out, v_out, kk_out, b_out, g_out, bonus_out):
    i = pl.program_id(0)
    x = x_ref[...]
    prev_last = jnp.where(i > 0, xp_ref[7:8, :], 0.0)
    row = lax.broadcasted_iota(jnp.int32, x.shape, 0)
    shifted = jnp.where(row == 0, prev_last, pltpu.roll(x, 1, 0))
    xx = shifted - x
    mix = lambda n: x + xx * mu_ref[n:n + 1, :]
    hs = hs_ref[...]

    r = _bdot(mix(0), wrkv_ref[0])
    k = _bdot(mix(2), wrkv_ref[1])
    v = _bdot(mix(3), wrkv_ref[2])
    zw = w0_ref[...] + _bdot(jnp.tanh(_bdot(mix(1), w1_ref[...])), w2_ref[...])
    u = -zw
    w_log = -(jnp.maximum(u, 0.0) + jnp.log(1.0 + jnp.exp(-jnp.abs(u)))) - 0.5
    a = jax.nn.sigmoid(a0_ref[...] + _bdot(_bdot(mix(4), a1_ref[...]), a2_ref[...]))
    g = _bdot(jax.nn.sigmoid(_bdot(mix(5), g1_ref[...])), g2_ref[...])

    kk = k * kk_w_ref[...]
    norm = jnp.sqrt(_head_sum(kk * kk, hs))
    kk = kk / jnp.maximum(norm, 1e-12)
    k = k * (1.0 + (a - 1.0) * ka_w_ref[...])

    r_out[...] = r.astype(BF16)
    lw_out[...] = -jnp.exp(w_log)
    k_out[...] = k.astype(BF16)
    v_out[...] = v.astype(BF16)
    kk_out[...] = kk.astype(BF16)
    b_out[...] = (kk * a).astype(BF16)
    g_out[...] = g.astype(BF16)
    bonus_out[...] = (_head_sum(r * k * rk_w_ref[...], hs) * v).astype(BF16)


def _rwkv_proj(x, p, hs, tm):
    t, c = x.shape
    row = lambda: pl.BlockSpec((tm, c), lambda i: (i, 0))
    prev = pl.BlockSpec((8, c), lambda i: (jnp.maximum(i * (tm // 8) - 1, 0), 0))
    ins = [x, x, p['mu'], p['w_rkv'], p['w0'], p['w1'], p['w2'], p['a0'], p['a1'], p['a2'],
           p['g1'], p['g2'], p['k_k'], p['k_a'], p['r_k'], hs]
    in_specs = [row(), prev] + [_full(a.shape) for a in ins[2:]]
    f32 = jax.ShapeDtypeStruct((t, c), F32)
    bf16 = jax.ShapeDtypeStruct((t, c), BF16)
    return pl.pallas_call(
        _rwkv_proj_kernel,
        grid=(t // tm,),
        in_specs=in_specs,
        out_specs=[row() for _ in range(8)],
        out_shape=[bf16, f32, bf16, bf16, bf16, bf16, bf16, bf16],
        compiler_params=_params("parallel"),
    )(*ins)


def _bmm(a, b):
    return lax.dot_general(a.astype(BF16), b.astype(BF16), (((2,), (1,)), ((0,), (0,))),
                           preferred_element_type=F32)


def _wkv_local_kernel(r_ref, lw_ref, k_ref, v_ref, kk_ref, b_ref,
                      al_out, rt_out, u0_out, y0_out, bh_out, kh_out, pl_out):
    tb, gw = r_ref.shape
    L = WKV_CHUNK
    nc = tb // L
    nh = gw // HEAD_DIM
    n = nh * L

    lane = lax.broadcasted_iota(jnp.int32, (1, 1, gw), 2)
    head_masks = [lane // HEAD_DIM == h for h in range(nh)]
    col = lax.broadcasted_iota(jnp.int32, (1, 1, n), 2)
    col_masks = [col // L == h for h in range(nh)]
    ri = lax.broadcasted_iota(jnp.int32, (1, L, n), 1)
    ci = lax.broadcasted_iota(jnp.int32, (1, L, n), 2) % L
    strict = ci < ri
    incl = ci <= ri
    eye_cat = (ci == ri).astype(F32)
    tt = min(tb, MXU_DIM)
    rb = lax.broadcasted_iota(jnp.int32, (tt, tt), 0)
    cb = lax.broadcasted_iota(jnp.int32, (tt, tt), 1)
    tri = jnp.logical_and(cb <= rb, cb // L == rb // L).astype(BF16)

    def stack(a, masks):
        a16 = a.astype(BF16)
        return jnp.concatenate([jnp.where(m, a16, 0) for m in masks], axis=1)

    lw2 = lw_ref[...]
    hi, lo = _split2(lw2)
    cum2 = jnp.concatenate(
        [jnp.dot(tri, hi[s:s + tt], preferred_element_type=F32)
         + jnp.dot(tri, lo[s:s + tt], preferred_element_type=F32) for s in range(0, tb, tt)], axis=0)
    to3 = lambda a: a.reshape(nc, L, gw)
    cum, lw = to3(cum2), to3(lw2)
    r, k, v, kk, b = (to3(ref[...]) for ref in (r_ref, k_ref, v_ref, kk_ref, b_ref))
    cum_l = cum[:, L - 1:L, :]
    p_inv = jnp.exp(-cum)
    al = -(kk * jnp.exp(cum - lw))
    bt = b * p_inv
    kt = k * p_inv
    rt = r * jnp.exp(cum)
    dec = jnp.exp(cum_l - cum)

    xs = jnp.concatenate([al, rt], axis=1)
    ys = jnp.concatenate([stack(bt, head_masks), stack(kt, head_masks)], axis=1)
    res = lax.dot_general(xs.astype(BF16), ys.astype(BF16), (((2,), (2,)), ((0,), (0,))),
                          preferred_element_type=F32)
    a_ab = jnp.where(strict, res[:, :L, :n], 0.0)
    a_ak = jnp.where(strict, res[:, :L, n:], 0.0)
    a_rb = jnp.where(incl, res[:, L:, :n], 0.0)
    a_rk = jnp.where(incl, res[:, L:, n:], 0.0)

    pw = a_ab
    tinv = eye_cat + pw
    pw = _bmm(pw, stack(pw, col_masks))
    sq = 2
    while 2 * sq < L:
        both = _bmm(jnp.concatenate([pw, tinv], axis=1), stack(pw, col_masks))
        pw, tinv = both[:, :L], tinv + both[:, L:]
        sq *= 2
    tinv = tinv + _bmm(tinv, stack(pw, col_masks))

    v_st = stack(v, head_masks)
    al_p = _bmm(tinv, stack(al, head_masks))
    u0 = _bmm(tinv, stack(_bmm(a_ak, v_st), head_masks))
    rt_p = rt + _bmm(a_rb, stack(al_p, head_masks))
    y0 = _bmm(a_rb, stack(u0, head_masks)) + _bmm(a_rk, v_st)

    to2 = lambda a: a.reshape(tb, gw)
    al_out[...] = to2(al_p).astype(BF16)
    rt_out[...] = to2(rt_p).astype(BF16)
    u0_out[...] = to2(u0).astype(BF16)
    y0_out[...] = to2(y0).astype(BF16)
    bh_out[...] = to2(b * dec).astype(BF16)
    kh_out[...] = to2(k * dec).astype(BF16)
    pl_out[...] = jnp.exp(cum_l)


def _wkv_local(r, lw, k, v, kk, b, tb):
    t, c = r.shape
    gw = WKV_GROUP
    nc = tb // WKV_CHUNK
    spec = lambda: pl.BlockSpec((tb, gw), lambda i, g: (i, g))
    f32 = jax.ShapeDtypeStruct((t, c), F32)
    bf16 = jax.ShapeDtypeStruct((t, c), BF16)
    return pl.pallas_call(
        _wkv_local_kernel,
        grid=(t // tb, c // gw),
        in_specs=[spec() for _ in range(6)],
        out_specs=[spec() for _ in range(6)] + [pl.BlockSpec((nc, 1, gw), lambda i, g: (i, 0, g))],
        out_shape=[bf16, bf16, bf16, bf16, bf16, bf16,
                   jax.ShapeDtypeStruct((t // WKV_CHUNK, 1, c), F32)],
        compiler_params=_params("parallel", "parallel"),
    )(r, lw, k, v, kk, b)


def _wkv_state_kernel(al_ref, rt_ref, u0_ref, y0_ref, bh_ref, kh_ref, v_ref, pl_ref, y_ref, sv_ref):
    tb, c = v_ref.shape
    L = WKV_CHUNK
    gw = WKV_GROUP
    groups = range(c // gw)

    @pl.when(pl.program_id(0) == 0)
    def _():
        sv_ref[...] = jnp.zeros_like(sv_ref)

    bd_mask = (lax.broadcasted_iota(jnp.int32, (gw, gw), 0) // HEAD_DIM
               == lax.broadcasted_iota(jnp.int32, (gw, gw), 1) // HEAD_DIM)

    def chunk(ci, carry):
        rows = pl.ds(pl.multiple_of(ci * L, L), L)
        lanes = [slice(g * gw, (g + 1) * gw) for g in groups]
        svs = [sv_ref[g] for g in groups]
        uys = [_nt_dot(jnp.concatenate([al_ref[rows, lanes[g]], rt_ref[rows, lanes[g]]], axis=0), svs[g])
               for g in groups]
        us = [uys[g][:L] + u0_ref[rows, lanes[g]] for g in groups]
        for g in groups:
            y_ref[rows, lanes[g]] = (uys[g][L:] + y0_ref[rows, lanes[g]]).astype(BF16)
        upds = [_tn_dot(jnp.concatenate([us[g].astype(BF16), v_ref[rows, lanes[g]].astype(BF16)], axis=0),
                        jnp.concatenate([bh_ref[rows, lanes[g]], kh_ref[rows, lanes[g]]], axis=0))
                for g in groups]
        for g in groups:
            sv_ref[g] = svs[g] * pl_ref[ci, :, lanes[g]] + jnp.where(bd_mask, upds[g], 0.0)
        return carry

    lax.fori_loop(0, tb // L, chunk, 0)


def _wkv_state(al_p, rt_p, u0, y0, bh, kh, v, p_l, tb):
    t, c = v.shape
    nc = tb // WKV_CHUNK
    row = lambda: pl.BlockSpec((tb, c), lambda i: (i, 0))
    return pl.pallas_call(
        _wkv_state_kernel,
        grid=(t // tb,),
        in_specs=[row() for _ in range(7)] + [pl.BlockSpec((nc, 1, c), lambda i: (i, 0, 0))],
        out_specs=row(),
        out_shape=jax.ShapeDtypeStruct((t, c), BF16),
        scratch_shapes=[pltpu.VMEM((c // WKV_GROUP, WKV_GROUP, WKV_GROUP), F32)],
        compiler_params=_params("arbitrary"),
    )(al_p, rt_p, u0, y0, bh, kh, v, p_l)


def _wkv_scan(r, lw, k, v, kk, b, tb_local, tb_state):
    al_p, rt_p, u0, y0, bh, kh, p_l = _wkv_local(r, lw, k, v, kk, b, tb_local)
    return _wkv_state(al_p, rt_p, u0, y0, bh, kh, v, p_l, tb_state)


def _once(shape):
    n = len(shape)
    return pl.BlockSpec(shape, lambda *_: (0,) * n, pipeline_mode=pl.Buffered(1))


def _ffn_bounds(f):
    mid = min(f, -(-(f // MXU_DIM) // 2) * MXU_DIM) if f % MXU_DIM == 0 else f
    return tuple(b for b in ((0, mid), (mid, f)) if b[1] > b[0])


def _swiglu_ln(x, win_ref, wdown_ref, lng_ref, lnb_ref):
    xb = x.astype(BF16)
    f = wdown_ref.shape[0]
    acc = jnp.zeros(x.shape, F32)
    for lo, hi in _ffn_bounds(f):
        gate = jnp.dot(xb, win_ref[:, lo:hi], preferred_element_type=F32)
        up = jnp.dot(xb, win_ref[:, f + lo:f + hi], preferred_element_type=F32)
        act = (gate * jax.nn.sigmoid(gate) * up).astype(BF16)
        acc = acc + jnp.dot(act, wdown_ref[lo:hi, :], preferred_element_type=F32)
    return _layer_norm(ALPHA * x + acc, lng_ref[...], lnb_ref[...])


def _rwkv_out_ffn_kernel(y_ref, g_ref, bonus_ref, x_ref, gng_ref, gnb_ref, wo_ref, lng_ref, lnb_ref, hs_ref,
                         win_ref, wdown_ref, lng2_ref, lnb2_ref, o_ref):
    hs = hs_ref[...]
    y = y_ref[...]
    mean = _head_sum(y, hs) * (1.0 / HEAD_DIM)
    d = y - mean
    var = _head_sum(d * d, hs) * (1.0 / HEAD_DIM)
    yn = d * lax.rsqrt(var + GN_EPS) * gng_ref[...] + gnb_ref[...]
    z = (yn + bonus_ref[...]) * g_ref[...]
    h = _bdot(z, wo_ref[...])
    x1 = _layer_norm(ALPHA * x_ref[...] + h, lng_ref[...], lnb_ref[...])
    o_ref[...] = _swiglu_ln(x1, win_ref, wdown_ref, lng2_ref, lnb2_ref)


def _rwkv_out_ffn(y, g, bonus, x, gn_g, gn_b, w_o, ln_g, ln_b, hs, ffn, tm):
    t, c = x.shape
    row = lambda: pl.BlockSpec((tm, c), lambda i: (i, 0))
    consts = [gn_g, gn_b, w_o, ln_g, ln_b, hs, *ffn]
    return pl.pallas_call(
        _rwkv_out_ffn_kernel,
        grid=(t // tm,),
        in_specs=[row() for _ in range(4)] + [_once(a.shape) for a in consts],
        out_specs=row(),
        out_shape=jax.ShapeDtypeStruct((t, c), F32),
        compiler_params=_params("parallel"),
    )(y, g, bonus, x, *consts)


def _attn_out_ffn_kernel(at_ref, x_ref, wo_ref, lng_ref, lnb_ref, win_ref, wdown_ref, lng2_ref, lnb2_ref,
                         o_ref):
    h = _tn_dot(at_ref[...], wo_ref[...])
    x1 = _layer_norm(ALPHA * x_ref[...] + h, lng_ref[...], lnb_ref[...])
    o_ref[...] = _swiglu_ln(x1, win_ref, wdown_ref, lng2_ref, lnb2_ref)


def _attn_out_ffn(at, x, w_o, ln_g, ln_b, ffn, tm):
    t, c = x.shape
    consts = [w_o, ln_g, ln_b, *ffn]
    return pl.pallas_call(
        _attn_out_ffn_kernel,
        grid=(t // tm,),
        in_specs=[pl.BlockSpec((c, tm), lambda i: (0, i)),
                  pl.BlockSpec((tm, c), lambda i: (i, 0))] + [_once(a.shape) for a in consts],
        out_specs=pl.BlockSpec((tm, c), lambda i: (i, 0)),
        out_shape=jax.ShapeDtypeStruct((t, c), F32),
        compiler_params=_params("parallel"),
    )(at, x, *consts)


def _qkv_kernel(x_ref, wqt_ref, wk_ref, wvt_ref, cos_t_ref, sin_t_ref, cos_k_ref, sin_k_ref,
                qt_out, k_out, vt_out, kmean_out):
    xb = x_ref[...].astype(BF16)
    tm, c = xb.shape
    nh = c // HEAD_DIM
    half = HEAD_DIM // 2

    qt = _nt_dot(wqt_ref[...], xb)
    cos_t, sin_t = cos_t_ref[...], sin_t_ref[...]
    scale = HEAD_DIM ** -0.5 * LOG2_E
    for h in range(nh):
        q1 = qt[h * HEAD_DIM:h * HEAD_DIM + half]
        q2 = qt[h * HEAD_DIM + half:(h + 1) * HEAD_DIM]
        qt_out[h, 0:half, :] = ((q1 * cos_t - q2 * sin_t) * scale).astype(BF16)
        qt_out[h, half:HEAD_DIM, :] = ((q2 * cos_t + q1 * sin_t) * scale).astype(BF16)

    vt = _nt_dot(wvt_ref[...], xb)
    for h in range(nh):
        vt_out[h] = vt[h * HEAD_DIM:(h + 1) * HEAD_DIM].astype(BF16)

    k = jnp.dot(xb, wk_ref[...], preferred_element_type=F32)
    reps = c // cos_k_ref.shape[1]
    cos_k = jnp.concatenate([cos_k_ref[...]] * reps, axis=1)
    sin_k = jnp.concatenate([sin_k_ref[...]] * reps, axis=1)
    lane = lax.broadcasted_iota(jnp.int32, k.shape, 1)
    first = (lane % HEAD_DIM) < half
    rot = jnp.where(first, pltpu.roll(k, c - half, 1), pltpu.roll(k, half, 1))
    k = k * cos_k + rot * sin_k
    for h in range(nh):
        k_out[h] = k[:, h * HEAD_DIM:(h + 1) * HEAD_DIM].astype(BF16)
    for blk in range(tm // MOBA_BLOCK):
        kmean_out[blk] = jnp.mean(k[blk * MOBA_BLOCK:(blk + 1) * MOBA_BLOCK], axis=0, keepdims=True)


def _qkv_rope(x, wq_t, wk, wv_t, cos_t, sin_t, cos_k, sin_k, tm):
    t, c = x.shape
    nh = c // HEAD_DIM
    half = HEAD_DIM // 2
    nb = t // MOBA_BLOCK
    return pl.pallas_call(
        _qkv_kernel,
        grid=(t // tm,),
        in_specs=[pl.BlockSpec((tm, c), lambda i: (i, 0)),
                  _full(wq_t.shape), _full(wk.shape), _full(wv_t.shape),
                  pl.BlockSpec((half, tm), lambda i: (0, i)),
                  pl.BlockSpec((half, tm), lambda i: (0, i)),
                  pl.BlockSpec((tm, cos_k.shape[1]), lambda i: (i, 0)),
                  pl.BlockSpec((tm, sin_k.shape[1]), lambda i: (i, 0))],
        out_specs=[pl.BlockSpec((nh, HEAD_DIM, tm), lambda i: (0, 0, i)),
                   pl.BlockSpec((nh, tm, HEAD_DIM), lambda i: (0, i, 0)),
                   pl.BlockSpec((nh, HEAD_DIM, tm), lambda i: (0, 0, i)),
                   pl.BlockSpec((tm // MOBA_BLOCK, 1, c), lambda i: (i, 0, 0))],
        out_shape=[jax.ShapeDtypeStruct((nh, HEAD_DIM, t), BF16),
                   jax.ShapeDtypeStruct((nh, t, HEAD_DIM), BF16),
                   jax.ShapeDtypeStruct((nh, HEAD_DIM, t), BF16),
                   jax.ShapeDtypeStruct((nb, 1, c), F32)],
        compiler_params=_params("parallel"),
    )(x, wq_t, wk, wv_t, cos_t, sin_t, cos_k, sin_k)


def _moba_kernel(qt_ref, k_ref, vt_ref, kmean_ref, o_ref, sel_ref, s_ref):
    i = pl.program_id(1)
    hb, _, bq = qt_ref.shape
    nb = kmean_ref.shape[1]
    blk = MOBA_BLOCK

    qts = [qt_ref[h] for h in range(hb)]
    heads = range(hb)
    ones = jnp.ones((16, blk), BF16)

    def select_blocks():
        bidx = lax.broadcasted_iota(jnp.int32, (nb, bq), 0)
        gates = []
        for h in heads:
            km_hi, km_lo = _split2(kmean_ref[h])
            gate = (jnp.dot(km_hi, qts[h], preferred_element_type=F32)
                    + jnp.dot(km_lo, qts[h], preferred_element_type=F32))
            gates.append(jnp.where(bidx < i, gate, -jnp.inf))
        sels = [jnp.zeros((nb, bq), F32) for _ in heads]
        for _ in range(MOBA_TOPK):
            ms = [jnp.max(g, axis=0, keepdims=True) for g in gates]
            firsts = [jnp.min(jnp.where(jnp.logical_and(gates[h] == ms[h], ms[h] > -jnp.inf), bidx, nb),
                              axis=0, keepdims=True) for h in heads]
            picks = [bidx == f for f in firsts]
            sels = [jnp.where(picks[h], 1.0, sels[h]) for h in heads]
            gates = [jnp.where(picks[h], -jnp.inf, gates[h]) for h in heads]
        for h in heads:
            sel_ref[h] = sels[h]

    def put_scores(j, slot):
        koff = pl.multiple_of(j * blk, blk)
        for h in heads:
            s_ref[slot, h] = jnp.dot(k_ref[h, pl.ds(koff, blk), :], qts[h],
                                     preferred_element_type=F32)

    def attend(j, slot, carry, causal):
        koff = pl.multiple_of(j * blk, blk)
        ss = [s_ref[slot, h] for h in heads]
        if causal:
            key = lax.broadcasted_iota(jnp.int32, (blk, bq), 0)
            qry = lax.broadcasted_iota(jnp.int32, (blk, bq), 1)
            ss = [jnp.where(key <= qry, s, -jnp.inf) for s in ss]
        else:
            ons = [sel_ref[h, pl.ds(j, 1), :] > 0.0 for h in heads]
        cmaxs = [jnp.max(s, axis=0, keepdims=True) for s in ss]
        ms = [carry[h][0] for h in heads]
        if causal:
            m_news = [jnp.maximum(ms[h], cmaxs[h]) for h in heads]
            m_ps = m_news
        else:
            m_news = [jnp.where(ons[h], jnp.maximum(ms[h], cmaxs[h]), ms[h]) for h in heads]
            m_ps = [jnp.where(ons[h], m_news[h], cmaxs[h]) for h in heads]
        ps = [jnp.exp2(ss[h] - m_ps[h]).astype(BF16) for h in heads]
        os = [jnp.dot(jnp.concatenate([vt_ref[h, :, pl.ds(koff, blk)], ones], axis=0), ps[h],
                      preferred_element_type=F32) for h in heads]
        if not causal:
            os = [jnp.where(ons[h], os[h], 0.0) for h in heads]
        return tuple((m_news[h], carry[h][1] * jnp.exp2(ms[h] - m_news[h]) + os[h]) for h in heads)

    init = tuple((jnp.full((1, bq), NEG_BIG, F32), jnp.zeros((HEAD_DIM + 16, bq), F32)) for _ in heads)

    def blocks(j0, count, carry):
        for d in range(count):
            put_scores(j0 + d + 1, (d + 1) % 2)
            carry = attend(j0 + d, d % 2, carry, False)
        return carry

    def finish(carry):
        for h in heads:
            acc = carry[h][1]
            o_ref[h] = acc[:HEAD_DIM] / acc[HEAD_DIM:HEAD_DIM + 1]

    unroll = MOBA_BLOCKS_PER_ITER
    put_scores(0, 0)
    select_blocks()
    carry = lax.fori_loop(0, i // unroll, lambda jj, c: blocks(unroll * jj, unroll, c), init)
    done = (i // unroll) * unroll
    carry = lax.fori_loop(0, (i - done) // 2, lambda jj, c: blocks(done + 2 * jj, 2, c), carry)

    @pl.when(i % 2 == 0)
    def _():
        finish(attend(i, 0, carry, True))

    @pl.when(i % 2 == 1)
    def _():
        put_scores(i, 1)
        finish(attend(i, 1, attend(i - 1, 0, carry, False), True))


def _moba_attention(qt, k, vt, kmean, hb):
    nh, dh, t = qt.shape
    nb = t // MOBA_BLOCK
    once = pl.Buffered(1)
    return pl.pallas_call(
        _moba_kernel,
        grid=(nh // hb, nb),
        in_specs=[pl.BlockSpec((hb, dh, MOBA_BLOCK), lambda g, i: (g, 0, i)),
                  pl.BlockSpec((hb, t, dh), lambda g, i: (g, 0, 0), pipeline_mode=once),
                  pl.BlockSpec((hb, dh, t), lambda g, i: (g, 0, 0), pipeline_mode=once),
                  pl.BlockSpec((hb, nb, dh), lambda g, i: (g, 0, 0))],
        out_specs=pl.BlockSpec((hb, dh, MOBA_BLOCK), lambda g, i: (g, 0, i)),
        out_shape=jax.ShapeDtypeStruct((nh, dh, t), F32),
        scratch_shapes=[pltpu.VMEM((hb, nb, MOBA_BLOCK), F32),
                        pltpu.VMEM((2, hb, MOBA_BLOCK, MOBA_BLOCK), F32)],
        compiler_params=_params("parallel", "arbitrary"),
    )(qt, k, vt, kmean)


def _row(a):
    return a.reshape(1, -1).astype(F32)


def _rwkv_layer(x, p, ln_g, ln_b, hs, ffn, tm):
    r, lw, k, v, kk, b, g, bonus = _rwkv_proj(x, p, hs, tm)
    t = x.shape[0]
    y = _wkv_scan(r, lw, k, v, kk, b, tb_local=min(1024, t), tb_state=min(256, t))
    return _rwkv_out_ffn(y, g, bonus, x, p['gn_g'], p['gn_b'], p['w_o'], ln_g, ln_b, hs, ffn, tm)


def _rope_tables(t):
    half = HEAD_DIM // 2
    inv = ROPE_THETA ** (-jnp.arange(0, HEAD_DIM, 2, dtype=F32) / HEAD_DIM)
    ang = jnp.arange(t, dtype=F32)[:, None] * inv[None, :]
    cos, sin = jnp.cos(ang), jnp.sin(ang)
    reps = LANES // HEAD_DIM
    cos_k = jnp.tile(jnp.concatenate([cos, cos], axis=1), (1, reps))
    sin_k = jnp.tile(jnp.concatenate([-sin, sin], axis=1), (1, reps))
    del half
    return cos.T, sin.T, cos_k, sin_k


def _moba_layer(x, w_qkv, w_o, ln_g, ln_b, ffn, tm):
    t, c = x.shape
    nh = c // HEAD_DIM
    wq_t = w_qkv[:, :c].T.astype(BF16)
    wk = w_qkv[:, c:2 * c].astype(BF16)
    wv_t = w_qkv[:, 2 * c:].T.astype(BF16)
    cos_t, sin_t, cos_k, sin_k = _rope_tables(t)
    qt, k, vt, kmean = _qkv_rope(x, wq_t, wk, wv_t, cos_t, sin_t, cos_k, sin_k, tm)
    kmean = jnp.transpose(kmean.reshape(t // MOBA_BLOCK, nh, HEAD_DIM), (1, 0, 2))
    at = _moba_attention(qt, k, vt, kmean, hb=min(MOBA_HEADS_PER_STEP, nh))
    return _attn_out_ffn(at.reshape(c, t), x, w_o.astype(BF16), ln_g, ln_b, ffn, tm)


def kernel(x, rwkv_mu, rwkv_w_rkv, rwkv_w0, rwkv_w1, rwkv_w2, rwkv_a0, rwkv_a1, rwkv_a2, rwkv_g1, rwkv_g2,
           rwkv_k_k, rwkv_k_a, rwkv_r_k, rwkv_gn_g, rwkv_gn_b, rwkv_w_o, moba_w_qkv, moba_w_o,
           ffn_w_in, ffn_w_down, ln_mix_g, ln_mix_b, ln_ffn_g, ln_ffn_b):
    bsz, t, c = x.shape
    assert c % WKV_GROUP == 0 and t % MOBA_BLOCK == 0
    tm = min(512, t)
    hs = _head_ones()
    outs = []
    for bi in range(bsz):
        h = x[bi]
        for i in range(DEPTH):
            j = i // 2
            ffn = (ffn_w_in[i].astype(BF16), ffn_w_down[i].astype(BF16), _row(ln_ffn_g[i]), _row(ln_ffn_b[i]))
            if i % 2 == 0:
                p = dict(mu=rwkv_mu[j], w_rkv=rwkv_w_rkv[j].astype(BF16), w0=_row(rwkv_w0[j]),
                         w1=rwkv_w1[j].astype(BF16), w2=rwkv_w2[j].astype(BF16), a0=_row(rwkv_a0[j]),
                         a1=rwkv_a1[j].astype(BF16), a2=rwkv_a2[j].astype(BF16),
                         g1=rwkv_g1[j].astype(BF16), g2=rwkv_g2[j].astype(BF16),
                         k_k=_row(rwkv_k_k[j]), k_a=_row(rwkv_k_a[j]), r_k=_row(rwkv_r_k[j]),
                         gn_g=_row(rwkv_gn_g[j]), gn_b=_row(rwkv_gn_b[j]), w_o=rwkv_w_o[j].astype(BF16))
                h = _rwkv_layer(h, p, _row(ln_mix_g[i]), _row(ln_mix_b[i]), hs, ffn, tm)
            else:
                h = _moba_layer(h, moba_w_qkv[j], moba_w_o[j], _row(ln_mix_g[i]), _row(ln_mix_b[i]), ffn, tm)
        outs.append(h)
    return jnp.stack(outs, axis=0)
```

```python
import jax
import jax.numpy as jnp
from jax import lax
from jax.experimental import pallas as pl
from jax.experimental.pallas import tpu as pltpu

F32 = jnp.float32
BF16 = jnp.bfloat16

HEAD_DIM = 64
DEPTH = 2
ALPHA = (2 * DEPTH) ** 0.25
LN_EPS = 1e-5
GN_EPS = HEAD_DIM * 1e-5
MOBA_BLOCK = 256
MOBA_TOPK = 3
ROPE_THETA = 10000.0
LOG2_E = 1.4426950408889634
MOBA_HEADS_PER_STEP = 4
MOBA_BLOCKS_PER_ITER = 8

LANES = 128
MXU_DIM = 256
WKV_CHUNK = 64
WKV_GROUP = MXU_DIM
NEG_BIG = -1e30
VMEM_LIMIT = 56 * 1024 * 1024


def _params(*sem):
    return pltpu.CompilerParams(dimension_semantics=sem, vmem_limit_bytes=VMEM_LIMIT)


def _full(shape):
    n = len(shape)
    return pl.BlockSpec(shape, lambda *_: (0,) * n)


def _bdot(a, b):
    return jnp.dot(a.astype(BF16), b.astype(BF16), preferred_element_type=F32)


def _nt_dot(a, b):
    return lax.dot_general(a.astype(BF16), b.astype(BF16), (((1,), (1,)), ((), ())),
                           preferred_element_type=F32)


def _tn_dot(a, b):
    return lax.dot_general(a.astype(BF16), b.astype(BF16), (((0,), (0,)), ((), ())),
                           preferred_element_type=F32)


def _split2(a):
    hi = a.astype(BF16)
    lo = (a - hi.astype(F32)).astype(BF16)
    return hi, lo


def _dot_exact_rhs(a, b_exact):
    if a.dtype == BF16:
        return jnp.dot(a, b_exact, preferred_element_type=F32)
    hi, lo = _split2(a)
    return (jnp.dot(hi, b_exact, preferred_element_type=F32)
            + jnp.dot(lo, b_exact, preferred_element_type=F32))


def _head_sum(a, ones_bd):
    w = ones_bd.shape[0]
    return jnp.concatenate([_dot_exact_rhs(a[:, s:s + w], ones_bd) for s in range(0, a.shape[1], w)],
                           axis=1)


def _layer_norm(z, g, b):
    mu = jnp.mean(z, axis=-1, keepdims=True)
    d = z - mu
    var = jnp.mean(d * d, axis=-1, keepdims=True)
    return d * lax.rsqrt(var + LN_EPS) * g + b


def _head_ones():
    h = jnp.arange(MXU_DIM) // HEAD_DIM
    return (h[:, None] == h[None, :]).astype(BF16)


def _rwkv_proj_kernel(x_ref, xp_ref, mu_ref, wrkv_ref, w0_ref, w1_ref, w2_ref, a0_ref, a1_ref, a2_ref,
                      g1_ref, g2_ref, kk_w_ref, ka_w_ref, rk_w_ref, hs_ref,
                      r_out, lw_out, k_out, v_out, kk_out, b_out, g_out, bonus_out):
    i = pl.program_id(0)
    x = x_ref[...]
    prev_last = jnp.where(i > 0, xp_ref[7:8, :], 0.0)
    row = lax.broadcasted_iota(jnp.int32, x.shape, 0)
    shifted = jnp.where(row == 0, prev_last, pltpu.roll(x, 1, 0))
    xx = shifted - x
    mix = lambda n: x + xx * mu_ref[n:n + 1, :]
    hs = hs_ref[...]

    r = _bdot(mix(0), wrkv_ref[0])
    k = _bdot(mix(2), wrkv_ref[1])
    v = _bdot(mix(3), wrkv_ref[2])
    zw = w0_ref[...] + _bdot(jnp.tanh(_bdot(mix(1), w1_ref[...])), w2_ref[...])
    u = -zw
    w_log = -(jnp.maximum(u, 0.0) + jnp.log(1.0 + jnp.exp(-jnp.abs(u)))) - 0.5
    a = jax.nn.sigmoid(a0_ref[...] + _bdot(_bdot(mix(4), a1_ref[...]), a2_ref[...]))
    g = _bdot(jax.nn.sigmoid(_bdot(mix(5), g1_ref[...])), g2_ref[...])

    kk = k * kk_w_ref[...]
    norm = jnp.sqrt(_head_sum(kk * kk, hs))
    kk = kk / jnp.maximum(norm, 1e-12)
    k = k * (1.0 + (a - 1.0) * ka_w_ref[...])

    r_out[...] = r.astype(BF16)
    lw_out[...] = -jnp.exp(w_log)
    k_out[...] = k.astype(BF16)
    v_out[...] = v.astype(BF16)
    kk_out[...] = kk.astype(BF16)
    b_out[...] = (kk * a).astype(BF16)
    g_out[...] = g.astype(BF16)
    bonus_out[...] = (_head_sum(r * k * rk_w_ref[...], hs) * v).astype(BF16)


def _rwkv_proj(x, p, hs, tm):
    t, c = x.shape
    row = lambda: pl.BlockSpec((tm, c), lambda i: (i, 0))
    prev = pl.BlockSpec((8, c), lambda i: (jnp.maximum(i * (tm // 8) - 1, 0), 0))
    ins = [x, x, p['mu'], p['w_rkv'], p['w0'], p['w1'], p['w2'], p['a0'], p['a1'], p['a2'],
           p['g1'], p['g2'], p['k_k'], p['k_a'], p['r_k'], hs]
    in_specs = [row(), prev] + [_full(a.shape) for a in ins[2:]]
    f32 = jax.ShapeDtypeStruct((t, c), F32)
    bf16 = jax.ShapeDtypeStruct((t, c), BF16)
    return pl.pallas_call(
        _rwkv_proj_kernel,
        grid=(t // tm,),
        in_specs=in_specs,
        out_specs=[row() for _ in range(8)],
        out_shape=[bf16, f32, bf16, bf16, bf16, bf16, bf16, bf16],
        compiler_params=_params("parallel"),
    )(*ins)


def _bmm(a, b):
    return lax.dot_general(a.astype(BF16), b.astype(BF16), (((2,), (1,)), ((0,), (0,))),
                           preferred_element_type=F32)


def _wkv_local_kernel(r_ref, lw_ref, k_ref, v_ref, kk_ref, b_ref,
                      al_out, rt_out, u0_out, y0_out, bh_out, kh_out, pl_out):
    tb, gw = r_ref.shape
    L = WKV_CHUNK
    nc = tb // L
    nh = gw // HEAD_DIM
    n = nh * L

    lane = lax.broadcasted_iota(jnp.int32, (1, 1, gw), 2)
    head_masks = [lane // HEAD_DIM == h for h in range(nh)]
    col = lax.broadcasted_iota(jnp.int32, (1, 1, n), 2)
    col_masks = [col // L == h for h in range(nh)]
    ri = lax.broadcasted_iota(jnp.int32, (1, L, n), 1)
    ci = lax.broadcasted_iota(jnp.int32, (1, L, n), 2) % L
    strict = ci < ri
    incl = ci <= ri
    eye_cat = (ci == ri).astype(F32)
    tt = min(tb, MXU_DIM)
    rb = lax.broadcasted_iota(jnp.int32, (tt, tt), 0)
    cb = lax.broadcasted_iota(jnp.int32, (tt, tt), 1)
    tri = jnp.logical_and(cb <= rb, cb // L == rb // L).astype(BF16)

    def stack(a, masks):
        a16 = a.astype(BF16)
        return jnp.concatenate([jnp.where(m, a16, 0) for m in masks], axis=1)

    lw2 = lw_ref[...]
    hi, lo = _split2(lw2)
    cum2 = jnp.concatenate(
        [jnp.dot(tri, hi[s:s + tt], preferred_element_type=F32)
         + jnp.dot(tri, lo[s:s + tt], preferred_element_type=F32) for s in range(0, tb, tt)], axis=0)
    to3 = lambda a: a.reshape(nc, L, gw)
    cum, lw = to3(cum2), to3(lw2)
    r, k, v, kk, b = (to3(ref[...]) for ref in (r_ref, k_ref, v_ref, kk_ref, b_ref))
    cum_l = cum[:, L - 1:L, :]
    p_inv = jnp.exp(-cum)
    al = -(kk * jnp.exp(cum - lw))
    bt = b * p_inv
    kt = k * p_inv
    rt = r * jnp.exp(cum)
    dec = jnp.exp(cum_l - cum)

    xs = jnp.concatenate([al, rt], axis=1)
    ys = jnp.concatenate([stack(bt, head_masks), stack(kt, head_masks)], axis=1)
    res = lax.dot_general(xs.astype(BF16), ys.astype(BF16), (((2,), (2,)), ((0,), (0,))),
                          preferred_element_type=F32)
    a_ab = jnp.where(strict, res[:, :L, :n], 0.0)
    a_ak = jnp.where(strict, res[:, :L, n:], 0.0)
    a_rb = jnp.where(incl, res[:, L:, :n], 0.0)
    a_rk = jnp.where(incl, res[:, L:, n:], 0.0)

    tinv = eye_cat + jnp.where(jnp.logical_and(ri // 2 == ci // 2, ci < ri), a_ab, 0.0)
    s = 4
    while s <= L:
        quad = jnp.logical_and(ri // s == ci // s, jnp.logical_and(ri % s >= s // 2, ci % s < s // 2))
        x = _bmm(jnp.where(quad, a_ab, 0.0), stack(tinv, col_masks))
        tinv = tinv + _bmm(tinv, stack(x, col_masks))
        s *= 2

    v_st = stack(v, head_masks)
    al_p = _bmm(tinv, stack(al, head_masks))
    u0 = _bmm(tinv, stack(_bmm(a_ak, v_st), head_masks))
    rt_p = rt + _bmm(a_rb, stack(al_p, head_masks))
    y0 = _bmm(a_rb, stack(u0, head_masks)) + _bmm(a_rk, v_st)

    to2 = lambda a: a.reshape(tb, gw)
    al_out[...] = to2(al_p).astype(BF16)
    rt_out[...] = to2(rt_p).astype(BF16)
    u0_out[...] = to2(u0).astype(BF16)
    y0_out[...] = to2(y0).astype(BF16)
    bh_out[...] = to2(b * dec).astype(BF16)
    kh_out[...] = to2(k * dec).astype(BF16)
    pl_out[...] = jnp.exp(cum_l)


def _wkv_local(r, lw, k, v, kk, b, tb):
    t, c = r.shape
    gw = WKV_GROUP
    nc = tb // WKV_CHUNK
    spec = lambda: pl.BlockSpec((tb, gw), lambda i, g: (i, g))
    f32 = jax.ShapeDtypeStruct((t, c), F32)
    bf16 = jax.ShapeDtypeStruct((t, c), BF16)
    return pl.pallas_call(
        _wkv_local_kernel,
        grid=(t // tb, c // gw),
        in_specs=[spec() for _ in range(6)],
        out_specs=[spec() for _ in range(6)] + [pl.BlockSpec((nc, 1, gw), lambda i, g: (i, 0, g))],
        out_shape=[bf16, bf16, bf16, bf16, bf16, bf16,
                   jax.ShapeDtypeStruct((t // WKV_CHUNK, 1, c), F32)],
        compiler_params=_params("parallel", "parallel"),
    )(r, lw, k, v, kk, b)


def _wkv_state_kernel(al_ref, rt_ref, u0_ref, y0_ref, bh_ref, kh_ref, v_ref, pl_ref, y_ref, sv_ref):
    tb, c = v_ref.shape
    L = WKV_CHUNK
    gw = WKV_GROUP
    groups = range(c // gw)

    @pl.when(pl.program_id(0) == 0)
    def _():
        sv_ref[...] = jnp.zeros_like(sv_ref)

    bd_mask = (lax.broadcasted_iota(jnp.int32, (gw, gw), 0) // HEAD_DIM
               == lax.broadcasted_iota(jnp.int32, (gw, gw), 1) // HEAD_DIM)

    def chunk(ci, carry):
        rows = pl.ds(pl.multiple_of(ci * L, L), L)
        lanes = [slice(g * gw, (g + 1) * gw) for g in groups]
        svs = [sv_ref[g] for g in groups]
        uys = [_nt_dot(jnp.concatenate([al_ref[rows, lanes[g]], rt_ref[rows, lanes[g]]], axis=0), svs[g])
               for g in groups]
        us = [uys[g][:L] + u0_ref[rows, lanes[g]] for g in groups]
        for g in groups:
            y_ref[rows, lanes[g]] = (uys[g][L:] + y0_ref[rows, lanes[g]]).astype(BF16)
        upds = [_tn_dot(jnp.concatenate([us[g].astype(BF16), v_ref[rows, lanes[g]].astype(BF16)], axis=0),
                        jnp.concatenate([bh_ref[rows, lanes[g]], kh_ref[rows, lanes[g]]], axis=0))
                for g in groups]
        for g in groups:
            sv_ref[g] = svs[g] * pl_ref[ci, :, lanes[g]] + jnp.where(bd_mask, upds[g], 0.0)
        return carry

    lax.fori_loop(0, tb // L, chunk, 0)


def _wkv_state(al_p, rt_p, u0, y0, bh, kh, v, p_l, tb):
    t, c = v.shape
    nc = tb // WKV_CHUNK
    row = lambda: pl.BlockSpec((tb, c), lambda i: (i, 0))
    return pl.pallas_call(
        _wkv_state_kernel,
        grid=(t // tb,),
        in_specs=[row() for _ in range(7)] + [pl.BlockSpec((nc, 1, c), lambda i: (i, 0, 0))],
        out_specs=row(),
        out_shape=jax.ShapeDtypeStruct((t, c), BF16),
        scratch_shapes=[pltpu.VMEM((c // WKV_GROUP, WKV_GROUP, WKV_GROUP), F32)],
        compiler_params=_params("arbitrary"),
    )(al_p, rt_p, u0, y0, bh, kh, v, p_l)


def _wkv_scan(r, lw, k, v, kk, b, tb_local, tb_state):
    al_p, rt_p, u0, y0, bh, kh, p_l = _wkv_local(r, lw, k, v, kk, b, tb_local)
    return _wkv_state(al_p, rt_p, u0, y0, bh, kh, v, p_l, tb_state)


def _once(shape):
    n = len(shape)
    return pl.BlockSpec(shape, lambda *_: (0,) * n, pipeline_mode=pl.Buffered(1))


def _ffn_bounds(f):
    mid = min(f, -(-(f // MXU_DIM) // 2) * MXU_DIM) if f % MXU_DIM == 0 else f
    return tuple(b for b in ((0, mid), (mid, f)) if b[1] > b[0])


def _swiglu_ln(x, win_ref, wdown_ref, lng_ref, lnb_ref):
    xb = x.astype(BF16)
    f = wdown_ref.shape[0]
    acc = jnp.zeros(x.shape, F32)
    for lo, hi in _ffn_bounds(f):
        gate = jnp.dot(xb, win_ref[:, lo:hi], preferred_element_type=F32)
        up = jnp.dot(xb, win_ref[:, f + lo:f + hi], preferred_element_type=F32)
        act = (gate * jax.nn.sigmoid(gate) * up).astype(BF16)
        acc = acc + jnp.dot(act, wdown_ref[lo:hi, :], preferred_element_type=F32)
    return _layer_norm(ALPHA * x + acc, lng_ref[...], lnb_ref[...])


def _rwkv_out_ffn_kernel(y_ref, g_ref, bonus_ref, x_ref, gng_ref, gnb_ref, wo_ref, lng_ref, lnb_ref, hs_ref,
                         win_ref, wdown_ref, lng2_ref, lnb2_ref, o_ref):
    hs = hs_ref[...]
    y = y_ref[...]
    mean = _head_sum(y, hs) * (1.0 / HEAD_DIM)
    d = y - mean
    var = _head_sum(d * d, hs) * (1.0 / HEAD_DIM)
    yn = d * lax.rsqrt(var + GN_EPS) * gng_ref[...] + gnb_ref[...]
    z = (yn + bonus_ref[...]) * g_ref[...]
    h = _bdot(z, wo_ref[...])
    x1 = _layer_norm(ALPHA * x_ref[...] + h, lng_ref[...], lnb_ref[...])
    o_ref[...] = _swiglu_ln(x1, win_ref, wdown_ref, lng2_ref, lnb2_ref)


def _rwkv_out_ffn(y, g, bonus, x, gn_g, gn_b, w_o, ln_g, ln_b, hs, ffn, tm):
    t, c = x.shape
    row = lambda: pl.BlockSpec((tm, c), lambda i: (i, 0))
    consts = [gn_g, gn_b, w_o, ln_g, ln_b, hs, *ffn]
    return pl.pallas_call(
        _rwkv_out_ffn_kernel,
        grid=(t // tm,),
        in_specs=[row() for _ in range(4)] + [_once(a.shape) for a in consts],
        out_specs=row(),
        out_shape=jax.ShapeDtypeStruct((t, c), F32),
        compiler_params=_params("parallel"),
    )(y, g, bonus, x, *consts)


def _attn_out_ffn_kernel(at_ref, x_ref, wo_ref, lng_ref, lnb_ref, win_ref, wdown_ref, lng2_ref, lnb2_ref,
                         o_ref):
    h = _tn_dot(at_ref[...], wo_ref[...])
    x1 = _layer_norm(ALPHA * x_ref[...] + h, lng_ref[...], lnb_ref[...])
    o_ref[...] = _swiglu_ln(x1, win_ref, wdown_ref, lng2_ref, lnb2_ref)


def _attn_out_ffn(at, x, w_o, ln_g, ln_b, ffn, tm):
    t, c = x.shape
    consts = [w_o, ln_g, ln_b, *ffn]
    return pl.pallas_call(
        _attn_out_ffn_kernel,
        grid=(t // tm,),
        in_specs=[pl.BlockSpec((c, tm), lambda i: (0, i)),
                  pl.BlockSpec((tm, c), lambda i: (i, 0))] + [_once(a.shape) for a in consts],
        out_specs=pl.BlockSpec((tm, c), lambda i: (i, 0)),
        out_shape=jax.ShapeDtypeStruct((t, c), F32),
        compiler_params=_params("parallel"),
    )(at, x, *consts)


def _qkv_kernel(x_ref, wqt_ref, wk_ref, wvt_ref, cos_t_ref, sin_t_ref, cos_k_ref, sin_k_ref,
                qt_out, k_out, vt_out, kmean_out):
    xb = x_ref[...].astype(BF16)
    tm, c = xb.shape
    nh = c // HEAD_DIM
    half = HEAD_DIM // 2

    qt = _nt_dot(wqt_ref[...], xb)
    cos_t, sin_t = cos_t_ref[...], sin_t_ref[...]
    scale = HEAD_DIM ** -0.5 * LOG2_E
    for h in range(nh):
        q1 = qt[h * HEAD_DIM:h * HEAD_DIM + half]
        q2 = qt[h * HEAD_DIM + half:(h + 1) * HEAD_DIM]
        qt_out[h, 0:half, :] = ((q1 * cos_t - q2 * sin_t) * scale).astype(BF16)
        qt_out[h, half:HEAD_DIM, :] = ((q2 * cos_t + q1 * sin_t) * scale).astype(BF16)

    vt = _nt_dot(wvt_ref[...], xb)
    for h in range(nh):
        vt_out[h] = vt[h * HEAD_DIM:(h + 1) * HEAD_DIM].astype(BF16)

    k = jnp.dot(xb, wk_ref[...], preferred_element_type=F32)
    reps = c // cos_k_ref.shape[1]
    cos_k = jnp.concatenate([cos_k_ref[...]] * reps, axis=1)
    sin_k = jnp.concatenate([sin_k_ref[...]] * reps, axis=1)
    lane = lax.broadcasted_iota(jnp.int32, k.shape, 1)
    first = (lane % HEAD_DIM) < half
    rot = jnp.where(first, pltpu.roll(k, c - half, 1), pltpu.roll(k, half, 1))
    k = k * cos_k + rot * sin_k
    for h in range(nh):
        k_out[h] = k[:, h * HEAD_DIM:(h + 1) * HEAD_DIM].astype(BF16)
    for blk in range(tm // MOBA_BLOCK):
        kmean_out[blk] = jnp.mean(k[blk * MOBA_BLOCK:(blk + 1) * MOBA_BLOCK], axis=0, keepdims=True)


def _qkv_rope(x, wq_t, wk, wv_t, cos_t, sin_t, cos_k, sin_k, tm):
    t, c = x.shape
    nh = c // HEAD_DIM
    half = HEAD_DIM // 2
    nb = t // MOBA_BLOCK
    return pl.pallas_call(
        _qkv_kernel,
        grid=(t // tm,),
        in_specs=[pl.BlockSpec((tm, c), lambda i: (i, 0)),
                  _full(wq_t.shape), _full(wk.shape), _full(wv_t.shape),
                  pl.BlockSpec((half, tm), lambda i: (0, i)),
                  pl.BlockSpec((half, tm), lambda i: (0, i)),
                  pl.BlockSpec((tm, cos_k.shape[1]), lambda i: (i, 0)),
                  pl.BlockSpec((tm, sin_k.shape[1]), lambda i: (i, 0))],
        out_specs=[pl.BlockSpec((nh, HEAD_DIM, tm), lambda i: (0, 0, i)),
                   pl.BlockSpec((nh, tm, HEAD_DIM), lambda i: (0, i, 0)),
                   pl.BlockSpec((nh, HEAD_DIM, tm), lambda i: (0, 0, i)),
                   pl.BlockSpec((tm // MOBA_BLOCK, 1, c), lambda i: (i, 0, 0))],
        out_shape=[jax.ShapeDtypeStruct((nh, HEAD_DIM, t), BF16),
                   jax.ShapeDtypeStruct((nh, t, HEAD_DIM), BF16),
                   jax.ShapeDtypeStruct((nh, HEAD_DIM, t), BF16),
                   jax.ShapeDtypeStruct((nb, 1, c), F32)],
        compiler_params=_params("parallel"),
    )(x, wq_t, wk, wv_t, cos_t, sin_t, cos_k, sin_k)


def _moba_kernel(qt_ref, k_ref, vt_ref, kmean_ref, o_ref, sel_ref, s_ref):
    i = pl.program_id(1)
    hb, _, bq = qt_ref.shape
    nb = kmean_ref.shape[1]
    blk = MOBA_BLOCK

    qts = [qt_ref[h] for h in range(hb)]
    heads = range(hb)
    ones = jnp.ones((16, blk), BF16)

    def select_blocks():
        bidx = lax.broadcasted_iota(jnp.int32, (nb, bq), 0)
        gates = []
        for h in heads:
            km_hi, km_lo = _split2(kmean_ref[h])
            gate = (jnp.dot(km_hi, qts[h], preferred_element_type=F32)
                    + jnp.dot(km_lo, qts[h], preferred_element_type=F32))
            gates.append(jnp.where(bidx < i, gate, -jnp.inf))
        sels = [jnp.zeros((nb, bq), F32) for _ in heads]
        for _ in range(MOBA_TOPK):
            ms = [jnp.max(g, axis=0, keepdims=True) for g in gates]
            firsts = [jnp.min(jnp.where(jnp.logical_and(gates[h] == ms[h], ms[h] > -jnp.inf), bidx, nb),
                              axis=0, keepdims=True) for h in heads]
            picks = [bidx == f for f in firsts]
            sels = [jnp.where(picks[h], 1.0, sels[h]) for h in heads]
            gates = [jnp.where(picks[h], -jnp.inf, gates[h]) for h in heads]
        for h in heads:
            sel_ref[h] = sels[h]

    def put_scores(j, slot):
        koff = pl.multiple_of(j * blk, blk)
        for h in heads:
            s_ref[slot, h] = jnp.dot(k_ref[h, pl.ds(koff, blk), :], qts[h],
                                     preferred_element_type=F32)

    def attend(j, slot, carry, causal):
        koff = pl.multiple_of(j * blk, blk)
        ss = [s_ref[slot, h] for h in heads]
        if causal:
            key = lax.broadcasted_iota(jnp.int32, (blk, bq), 0)
            qry = lax.broadcasted_iota(jnp.int32, (blk, bq), 1)
            ss = [jnp.where(key <= qry, s, -jnp.inf) for s in ss]
        else:
            ons = [sel_ref[h, pl.ds(j, 1), :] > 0.0 for h in heads]
        cmaxs = [jnp.max(s, axis=0, keepdims=True) for s in ss]
        ms = [carry[h][0] for h in heads]
        if causal:
            m_news = [jnp.maximum(ms[h], cmaxs[h]) for h in heads]
            m_ps = m_news
        else:
            m_news = [jnp.where(ons[h], jnp.maximum(ms[h], cmaxs[h]), ms[h]) for h in heads]
            m_ps = [jnp.where(ons[h], m_news[h], cmaxs[h]) for h in heads]
        ps = [jnp.exp2(ss[h] - m_ps[h]).astype(BF16) for h in heads]
        os = [jnp.dot(jnp.concatenate([vt_ref[h, :, pl.ds(koff, blk)], ones], axis=0), ps[h],
                      preferred_element_type=F32) for h in heads]
        if not causal:
            os = [jnp.where(ons[h], os[h], 0.0) for h in heads]
        return tuple((m_news[h], carry[h][1] * jnp.exp2(ms[h] - m_news[h]) + os[h]) for h in heads)

    init = tuple((jnp.full((1, bq), NEG_BIG, F32), jnp.zeros((HEAD_DIM + 16, bq), F32)) for _ in heads)

    def blocks(j0, count, carry):
        for d in range(count):
            put_scores(j0 + d + 1, (d + 1) % 2)
            carry = attend(j0 + d, d % 2, carry, False)
        return carry

    def finish(carry):
        for h in heads:
            acc = carry[h][1]
            o_ref[h] = acc[:HEAD_DIM] / acc[HEAD_DIM:HEAD_DIM + 1]

    unroll = MOBA_BLOCKS_PER_ITER
    put_scores(0, 0)
    select_blocks()
    carry = lax.fori_loop(0, i // unroll, lambda jj, c: blocks(unroll * jj, unroll, c), init)
    done = (i // unroll) * unroll
    carry = lax.fori_loop(0, (i - done) // 2, lambda jj, c: blocks(done + 2 * jj, 2, c), carry)

    @pl.when(i % 2 == 0)
    def _():
        finish(attend(i, 0, carry, True))

    @pl.when(i % 2 == 1)
    def _():
        put_scores(i, 1)
        finish(attend(i, 1, attend(i - 1, 0, carry, False), True))


def _moba_attention(qt, k, vt, kmean, hb):
    nh, dh, t = qt.shape
    nb = t // MOBA_BLOCK
    once = pl.Buffered(1)
    return pl.pallas_call(
        _moba_kernel,
        grid=(nh // hb, nb),
        in_specs=[pl.BlockSpec((hb, dh, MOBA_BLOCK), lambda g, i: (g, 0, i)),
                  pl.BlockSpec((hb, t, dh), lambda g, i: (g, 0, 0), pipeline_mode=once),
                  pl.BlockSpec((hb, dh, t), lambda g, i: (g, 0, 0), pipeline_mode=once),
                  pl.BlockSpec((hb, nb, dh), lambda g, i: (g, 0, 0))],
        out_specs=pl.BlockSpec((hb, dh, MOBA_BLOCK), lambda g, i: (g, 0, i)),
        out_shape=jax.ShapeDtypeStruct((nh, dh, t), F32),
        scratch_shapes=[pltpu.VMEM((hb, nb, MOBA_BLOCK), F32),
                        pltpu.VMEM((2, hb, MOBA_BLOCK, MOBA_BLOCK), F32)],
        compiler_params=_params("parallel", "arbitrary"),
    )(qt, k, vt, kmean)


def _row(a):
    return a.reshape(1, -1).astype(F32)


def _rwkv_layer(x, p, ln_g, ln_b, hs, ffn, tm):
    r, lw, k, v, kk, b, g, bonus = _rwkv_proj(x, p, hs, tm)
    t = x.shape[0]
    y = _wkv_scan(r, lw, k, v, kk, b, tb_local=min(1024, t), tb_state=min(256, t))
    return _rwkv_out_ffn(y, g, bonus, x, p['gn_g'], p['gn_b'], p['w_o'], ln_g, ln_b, hs, ffn, tm)


def _rope_tables(t):
    half = HEAD_DIM // 2
    inv = ROPE_THETA ** (-jnp.arange(0, HEAD_DIM, 2, dtype=F32) / HEAD_DIM)
    ang = jnp.arange(t, dtype=F32)[:, None] * inv[None, :]
    cos, sin = jnp.cos(ang), jnp.sin(ang)
    reps = LANES // HEAD_DIM
    cos_k = jnp.tile(jnp.concatenate([cos, cos], axis=1), (1, reps))
    sin_k = jnp.tile(jnp.concatenate([-sin, sin], axis=1), (1, reps))
    del half
    return cos.T, sin.T, cos_k, sin_k


def _moba_layer(x, w_qkv, w_o, ln_g, ln_b, ffn, tm):
    t, c = x.shape
    nh = c // HEAD_DIM
    wq_t = w_qkv[:, :c].T.astype(BF16)
    wk = w_qkv[:, c:2 * c].astype(BF16)
    wv_t = w_qkv[:, 2 * c:].T.astype(BF16)
    cos_t, sin_t, cos_k, sin_k = _rope_tables(t)
    qt, k, vt, kmean = _qkv_rope(x, wq_t, wk, wv_t, cos_t, sin_t, cos_k, sin_k, tm)
    kmean = jnp.transpose(kmean.reshape(t // MOBA_BLOCK, nh, HEAD_DIM), (1, 0, 2))
    at = _moba_attention(qt, k, vt, kmean, hb=min(MOBA_HEADS_PER_STEP, nh))
    return _attn_out_ffn(at.reshape(c, t), x, w_o.astype(BF16), ln_g, ln_b, ffn, tm)


def kernel(x, rwkv_mu, rwkv_w_rkv, rwkv_w0, rwkv_w1, rwkv_w2, rwkv_a0, rwkv_a1, rwkv_a2, rwkv_g1, rwkv_g2,
           rwkv_k_k, rwkv_k_a, rwkv_r_k, rwkv_gn_g, rwkv_gn_b, rwkv_w_o, moba_w_qkv, moba_w_o,
           ffn_w_in, ffn_w_down, ln_mix_g, ln_mix_b, ln_ffn_g, ln_ffn_b):
    bsz, t, c = x.shape
    assert c % WKV_GROUP == 0 and t % MOBA_BLOCK == 0
    tm = min(512, t)
    hs = _head_ones()
    outs = []
    for bi in range(bsz):
        h = x[bi]
        for i in range(DEPTH):
            j = i // 2
            ffn = (ffn_w_in[i].astype(BF16), ffn_w_down[i].astype(BF16), _row(ln_ffn_g[i]), _row(ln_ffn_b[i]))
            if i % 2 == 0:
                p = dict(mu=rwkv_mu[j], w_rkv=rwkv_w_rkv[j].astype(BF16), w0=_row(rwkv_w0[j]),
                         w1=rwkv_w1[j].astype(BF16), w2=rwkv_w2[j].astype(BF16), a0=_row(rwkv_a0[j]),
                         a1=rwkv_a1[j].astype(BF16), a2=rwkv_a2[j].astype(BF16),
                         g1=rwkv_g1[j].astype(BF16), g2=rwkv_g2[j].astype(BF16),
                         k_k=_row(rwkv_k_k[j]), k_a=_row(rwkv_k_a[j]), r_k=_row(rwkv_r_k[j]),
                         gn_g=_row(rwkv_gn_g[j]), gn_b=_row(rwkv_gn_b[j]), w_o=rwkv_w_o[j].astype(BF16))
                h = _rwkv_layer(h, p, _row(ln_mix_g[i]), _row(ln_mix_b[i]), hs, ffn, tm)
            else:
                h = _moba_layer(h, moba_w_qkv[j], moba_w_o[j], _row(ln_mix_g[i]), _row(ln_mix_b[i]), ffn, tm)
        outs.append(h)
    return jnp.stack(outs, axis=0)
```

```python
import jax
import jax.numpy as jnp
from jax import lax
from jax.experimental import pallas as pl
from jax.experimental.pallas import tpu as pltpu

F32 = jnp.float32
BF16 = jnp.bfloat16

HEAD_DIM = 64
DEPTH = 2
ALPHA = (2 * DEPTH) ** 0.25
LN_EPS = 1e-5
GN_EPS = HEAD_DIM * 1e-5
MOBA_BLOCK = 256
MOBA_TOPK = 3
ROPE_THETA = 10000.0
LOG2_E = 1.4426950408889634
MOBA_HEADS_PER_STEP = 4
MOBA_BLOCKS_PER_ITER = 8

LANES = 128
MXU_DIM = 256
WKV_CHUNK = 64
WKV_GROUP = MXU_DIM
WKV_LOCAL_TILE = 1024
WKV_STATE_TILE = 256
ROW_TILE = 512
NEG_BIG = -1e30
VMEM_LIMIT = 56 * 1024 * 1024


def _params(*sem):
    return pltpu.CompilerParams(dimension_semantics=sem, vmem_limit_bytes=VMEM_LIMIT)


def _full(shape):
    n = len(shape)
    return pl.BlockSpec(shape, lambda *_: (0,) * n)


def _bdot(a, b):
    return jnp.dot(a.astype(BF16), b.astype(BF16), preferred_element_type=F32)


def _nt_dot(a, b):
    return lax.dot_general(a.astype(BF16), b.astype(BF16), (((1,), (1,)), ((), ())),
                           preferred_element_type=F32)


def _tn_dot(a, b):
    return lax.dot_general(a.astype(BF16), b.astype(BF16), (((0,), (0,)), ((), ())),
                           preferred_element_type=F32)


def _split2(a):
    hi = a.astype(BF16)
    lo = (a - hi.astype(F32)).astype(BF16)
    return hi, lo


def _dot_exact_rhs(a, b_exact):
    if a.dtype == BF16:
        return jnp.dot(a, b_exact, preferred_element_type=F32)
    hi, lo = _split2(a)
    return (jnp.dot(hi, b_exact, preferred_element_type=F32)
            + jnp.dot(lo, b_exact, preferred_element_type=F32))


def _head_sum(a, ones_bd):
    w = ones_bd.shape[0]
    return jnp.concatenate([_dot_exact_rhs(a[:, s:s + w], ones_bd) for s in range(0, a.shape[1], w)],
                           axis=1)


def _layer_norm(z, g, b):
    mu = jnp.mean(z, axis=-1, keepdims=True)
    d = z - mu
    var = jnp.mean(d * d, axis=-1, keepdims=True)
    return d * lax.rsqrt(var + LN_EPS) * g + b


def _head_ones():
    h = jnp.arange(MXU_DIM) // HEAD_DIM
    return (h[:, None] == h[None, :]).astype(BF16)


def _rwkv_proj_kernel(x_ref, xp_ref, mu_ref, wrkv_ref, w0_ref, w1_ref, w2_ref, a0_ref, a1_ref, a2_ref,
                      g1_ref, g2_ref, kk_w_ref, ka_w_ref, rk_w_ref, hs_ref,
                      r_out, lw_out, k_out, v_out, kk_out, b_out, g_out, bonus_out):
    i = pl.program_id(0)
    x = x_ref[...]
    prev_last = jnp.where(i > 0, xp_ref[7:8, :], 0.0)
    row = lax.broadcasted_iota(jnp.int32, x.shape, 0)
    shifted = jnp.where(row == 0, prev_last, pltpu.roll(x, 1, 0))
    xx = shifted - x
    mix = lambda n: x + xx * mu_ref[n:n + 1, :]
    hs = hs_ref[...]

    r = _bdot(mix(0), wrkv_ref[0])
    k = _bdot(mix(2), wrkv_ref[1])
    v = _bdot(mix(3), wrkv_ref[2])
    zw = w0_ref[...] + _bdot(jnp.tanh(_bdot(mix(1), w1_ref[...])), w2_ref[...])
    u = -zw
    w_log = -(jnp.maximum(u, 0.0) + jnp.log(1.0 + jnp.exp(-jnp.abs(u)))) - 0.5
    a = jax.nn.sigmoid(a0_ref[...] + _bdot(_bdot(mix(4), a1_ref[...]), a2_ref[...]))
    g = _bdot(jax.nn.sigmoid(_bdot(mix(5), g1_ref[...])), g2_ref[...])

    kk = k * kk_w_ref[...]
    norm = jnp.sqrt(_head_sum(kk * kk, hs))
    kk = kk / jnp.maximum(norm, 1e-12)
    k = k * (1.0 + (a - 1.0) * ka_w_ref[...])

    r_out[...] = r.astype(BF16)
    lw_out[...] = -jnp.exp(w_log)
    k_out[...] = k.astype(BF16)
    v_out[...] = v.astype(BF16)
    kk_out[...] = kk.astype(BF16)
    b_out[...] = (kk * a).astype(BF16)
    g_out[...] = g.astype(BF16)
    bonus_out[...] = (_head_sum(r * k * rk_w_ref[...], hs) * v).astype(BF16)


def _rwkv_proj(x, p, hs, tm):
    t, c = x.shape
    row = lambda: pl.BlockSpec((tm, c), lambda i: (i, 0))
    prev = pl.BlockSpec((8, c), lambda i: (jnp.maximum(i * (tm // 8) - 1, 0), 0))
    ins = [x, x, p['mu'], p['w_rkv'], p['w0'], p['w1'], p['w2'], p['a0'], p['a1'], p['a2'],
           p['g1'], p['g2'], p['k_k'], p['k_a'], p['r_k'], hs]
    in_specs = [row(), prev] + [_full(a.shape) for a in ins[2:]]
    f32 = jax.ShapeDtypeStruct((t, c), F32)
    bf16 = jax.ShapeDtypeStruct((t, c), BF16)
    return pl.pallas_call(
        _rwkv_proj_kernel,
        grid=(t // tm,),
        in_specs=in_specs,
        out_specs=[row() for _ in range(8)],
        out_shape=[bf16, f32, bf16, bf16, bf16, bf16, bf16, bf16],
        compiler_params=_params("parallel"),
    )(*ins)


def _bmm(a, b):
    return lax.dot_general(a.astype(BF16), b.astype(BF16), (((2,), (1,)), ((0,), (0,))),
                           preferred_element_type=F32)


def _wkv_local_kernel(r_ref, lw_ref, k_ref, v_ref, kk_ref, b_ref,
                      al_out, rt_out, u0_out, y0_out, bh_out, kh_out, pl_out):
    tb, gw = r_ref.shape
    L = WKV_CHUNK
    nc = tb // L
    nh = gw // HEAD_DIM
    n = nh * L

    lane = lax.broadcasted_iota(jnp.int32, (1, 1, gw), 2)
    head_masks = [lane // HEAD_DIM == h for h in range(nh)]
    col = lax.broadcasted_iota(jnp.int32, (1, 1, n), 2)
    col_masks = [col // L == h for h in range(nh)]
    ri = lax.broadcasted_iota(jnp.int32, (1, L, n), 1)
    ci = lax.broadcasted_iota(jnp.int32, (1, L, n), 2) % L
    strict = ci < ri
    incl = ci <= ri
    eye_cat = (ci == ri).astype(F32)
    tt = min(tb, MXU_DIM)
    rb = lax.broadcasted_iota(jnp.int32, (tt, tt), 0)
    cb = lax.broadcasted_iota(jnp.int32, (tt, tt), 1)
    tri = jnp.logical_and(cb <= rb, cb // L == rb // L).astype(BF16)

    def stack(a, masks):
        a16 = a.astype(BF16)
        return jnp.concatenate([jnp.where(m, a16, 0) for m in masks], axis=1)

    lw2 = lw_ref[...]
    hi, lo = _split2(lw2)
    cum2 = jnp.concatenate(
        [jnp.dot(tri, hi[s:s + tt], preferred_element_type=F32)
         + jnp.dot(tri, lo[s:s + tt], preferred_element_type=F32) for s in range(0, tb, tt)], axis=0)
    to3 = lambda a: a.reshape(nc, L, gw)
    cum, lw = to3(cum2), to3(lw2)
    r, k, v, kk, b = (to3(ref[...]) for ref in (r_ref, k_ref, v_ref, kk_ref, b_ref))
    cum_l = cum[:, L - 1:L, :]
    p_inv = jnp.exp(-cum)
    al = -(kk * jnp.exp(cum - lw))
    bt = b * p_inv
    kt = k * p_inv
    rt = r * jnp.exp(cum)
    dec = jnp.exp(cum_l - cum)

    xs = jnp.concatenate([al, rt], axis=1)
    ys = jnp.concatenate([stack(bt, head_masks), stack(kt, head_masks)], axis=1)
    res = lax.dot_general(xs.astype(BF16), ys.astype(BF16), (((2,), (2,)), ((0,), (0,))),
                          preferred_element_type=F32)
    a_ab = jnp.where(strict, res[:, :L, :n], 0.0)
    a_ak = jnp.where(strict, res[:, :L, n:], 0.0)
    a_rb = jnp.where(incl, res[:, L:, :n], 0.0)
    a_rk = jnp.where(incl, res[:, L:, n:], 0.0)

    tinv = eye_cat + jnp.where(jnp.logical_and(ri // 2 == ci // 2, ci < ri), a_ab, 0.0)
    s = 4
    while s <= L:
        quad = jnp.logical_and(ri // s == ci // s, jnp.logical_and(ri % s >= s // 2, ci % s < s // 2))
        x = _bmm(jnp.where(quad, a_ab, 0.0), stack(tinv, col_masks))
        tinv = tinv + _bmm(tinv, stack(x, col_masks))
        s *= 2

    v_st = stack(v, head_masks)
    al_p = _bmm(tinv, stack(al, head_masks))
    u0 = _bmm(tinv, stack(_bmm(a_ak, v_st), head_masks))
    rt_p = rt + _bmm(a_rb, stack(al_p, head_masks))
    y0 = _bmm(a_rb, stack(u0, head_masks)) + _bmm(a_rk, v_st)

    to2 = lambda a: a.reshape(tb, gw)
    al_out[...] = to2(al_p).astype(BF16)
    rt_out[...] = to2(rt_p).astype(BF16)
    u0_out[...] = to2(u0).astype(BF16)
    y0_out[...] = to2(y0).astype(BF16)
    bh_out[...] = to2(b * dec).astype(BF16)
    kh_out[...] = to2(k * dec).astype(BF16)
    pl_out[...] = jnp.exp(cum_l)


def _wkv_local(r, lw, k, v, kk, b, tb):
    t, c = r.shape
    gw = WKV_GROUP
    nc = tb // WKV_CHUNK
    spec = lambda: pl.BlockSpec((tb, gw), lambda i, g: (i, g))
    f32 = jax.ShapeDtypeStruct((t, c), F32)
    bf16 = jax.ShapeDtypeStruct((t, c), BF16)
    return pl.pallas_call(
        _wkv_local_kernel,
        grid=(t // tb, c // gw),
        in_specs=[spec() for _ in range(6)],
        out_specs=[spec() for _ in range(6)] + [pl.BlockSpec((nc, 1, gw), lambda i, g: (i, 0, g))],
        out_shape=[bf16, bf16, bf16, bf16, bf16, bf16,
                   jax.ShapeDtypeStruct((t // WKV_CHUNK, 1, c), F32)],
        compiler_params=_params("parallel", "parallel"),
    )(r, lw, k, v, kk, b)


def _wkv_state_kernel(al_ref, rt_ref, u0_ref, y0_ref, bh_ref, kh_ref, v_ref, pl_ref, y_ref, sv_ref):
    tb, c = v_ref.shape
    L = WKV_CHUNK
    gw = WKV_GROUP
    groups = range(c // gw)

    @pl.when(pl.program_id(0) == 0)
    def _():
        sv_ref[...] = jnp.zeros_like(sv_ref)

    bd_mask = (lax.broadcasted_iota(jnp.int32, (gw, gw), 0) // HEAD_DIM
               == lax.broadcasted_iota(jnp.int32, (gw, gw), 1) // HEAD_DIM)

    def chunk(ci, carry):
        rows = pl.ds(pl.multiple_of(ci * L, L), L)
        lanes = [slice(g * gw, (g + 1) * gw) for g in groups]
        svs = [sv_ref[g] for g in groups]
        uys = [_nt_dot(jnp.concatenate([al_ref[rows, lanes[g]], rt_ref[rows, lanes[g]]], axis=0), svs[g])
               for g in groups]
        us = [uys[g][:L] + u0_ref[rows, lanes[g]] for g in groups]
        for g in groups:
            y_ref[rows, lanes[g]] = (uys[g][L:] + y0_ref[rows, lanes[g]]).astype(BF16)
        upds = [_tn_dot(jnp.concatenate([us[g].astype(BF16), v_ref[rows, lanes[g]].astype(BF16)], axis=0),
                        jnp.concatenate([bh_ref[rows, lanes[g]], kh_ref[rows, lanes[g]]], axis=0))
                for g in groups]
        for g in groups:
            sv_ref[g] = svs[g] * pl_ref[ci, :, lanes[g]] + jnp.where(bd_mask, upds[g], 0.0)
        return carry

    lax.fori_loop(0, tb // L, chunk, 0)


def _wkv_state(al_p, rt_p, u0, y0, bh, kh, v, p_l, tb):
    t, c = v.shape
    nc = tb // WKV_CHUNK
    row = lambda: pl.BlockSpec((tb, c), lambda i: (i, 0))
    return pl.pallas_call(
        _wkv_state_kernel,
        grid=(t // tb,),
        in_specs=[row() for _ in range(7)] + [pl.BlockSpec((nc, 1, c), lambda i: (i, 0, 0))],
        out_specs=row(),
        out_shape=jax.ShapeDtypeStruct((t, c), BF16),
        scratch_shapes=[pltpu.VMEM((c // WKV_GROUP, WKV_GROUP, WKV_GROUP), F32)],
        compiler_params=_params("arbitrary"),
    )(al_p, rt_p, u0, y0, bh, kh, v, p_l)


def _wkv_scan(r, lw, k, v, kk, b, tb_local, tb_state):
    al_p, rt_p, u0, y0, bh, kh, p_l = _wkv_local(r, lw, k, v, kk, b, tb_local)
    return _wkv_state(al_p, rt_p, u0, y0, bh, kh, v, p_l, tb_state)


def _once(shape):
    n = len(shape)
    return pl.BlockSpec(shape, lambda *_: (0,) * n, pipeline_mode=pl.Buffered(1))


def _ffn_bounds(f):
    mid = min(f, -(-(f // MXU_DIM) // 2) * MXU_DIM) if f % MXU_DIM == 0 else f
    return tuple(b for b in ((0, mid), (mid, f)) if b[1] > b[0])


def _swiglu_ln(x, win_ref, wdown_ref, lng_ref, lnb_ref):
    xb = x.astype(BF16)
    f = wdown_ref.shape[0]
    acc = jnp.zeros(x.shape, F32)
    for lo, hi in _ffn_bounds(f):
        gate = jnp.dot(xb, win_ref[:, lo:hi], preferred_element_type=F32)
        up = jnp.dot(xb, win_ref[:, f + lo:f + hi], preferred_element_type=F32)
        act = (gate * jax.nn.sigmoid(gate) * up).astype(BF16)
        acc = acc + jnp.dot(act, wdown_ref[lo:hi, :], preferred_element_type=F32)
    return _layer_norm(ALPHA * x + acc, lng_ref[...], lnb_ref[...])


def _rwkv_out_ffn_kernel(y_ref, g_ref, bonus_ref, x_ref, gng_ref, gnb_ref, wo_ref, lng_ref, lnb_ref, hs_ref,
                         win_ref, wdown_ref, lng2_ref, lnb2_ref, o_ref):
    hs = hs_ref[...]
    y = y_ref[...]
    mean = _head_sum(y, hs) * (1.0 / HEAD_DIM)
    d = y - mean
    var = _head_sum(d * d, hs) * (1.0 / HEAD_DIM)
    yn = d * lax.rsqrt(var + GN_EPS) * gng_ref[...] + gnb_ref[...]
    z = (yn + bonus_ref[...]) * g_ref[...]
    h = _bdot(z, wo_ref[...])
    x1 = _layer_norm(ALPHA * x_ref[...] + h, lng_ref[...], lnb_ref[...])
    o_ref[...] = _swiglu_ln(x1, win_ref, wdown_ref, lng2_ref, lnb2_ref)


def _rwkv_out_ffn(y, g, bonus, x, gn_g, gn_b, w_o, ln_g, ln_b, hs, ffn, tm):
    t, c = x.shape
    row = lambda: pl.BlockSpec((tm, c), lambda i: (i, 0))
    consts = [gn_g, gn_b, w_o, ln_g, ln_b, hs, *ffn]
    return pl.pallas_call(
        _rwkv_out_ffn_kernel,
        grid=(t // tm,),
        in_specs=[row() for _ in range(4)] + [_once(a.shape) for a in consts],
        out_specs=row(),
        out_shape=jax.ShapeDtypeStruct((t, c), F32),
        compiler_params=_params("parallel"),
    )(y, g, bonus, x, *consts)


def _attn_out_ffn_kernel(at_ref, x_ref, wo_ref, lng_ref, lnb_ref, win_ref, wdown_ref, lng2_ref, lnb2_ref,
                         o_ref):
    h = _tn_dot(at_ref[...], wo_ref[...])
    x1 = _layer_norm(ALPHA * x_ref[...] + h, lng_ref[...], lnb_ref[...])
    o_ref[...] = _swiglu_ln(x1, win_ref, wdown_ref, lng2_ref, lnb2_ref)


def _attn_out_ffn(at, x, w_o, ln_g, ln_b, ffn, tm):
    t, c = x.shape
    consts = [w_o, ln_g, ln_b, *ffn]
    return pl.pallas_call(
        _attn_out_ffn_kernel,
        grid=(t // tm,),
        in_specs=[pl.BlockSpec((c, tm), lambda i: (0, i)),
                  pl.BlockSpec((tm, c), lambda i: (i, 0))] + [_once(a.shape) for a in consts],
        out_specs=pl.BlockSpec((tm, c), lambda i: (i, 0)),
        out_shape=jax.ShapeDtypeStruct((t, c), F32),
        compiler_params=_params("parallel"),
    )(at, x, *consts)


def _qkv_kernel(x_ref, wqt_ref, wk_ref, wvt_ref, cos_t_ref, sin_t_ref, cos_k_ref, sin_k_ref,
                qt_out, k_out, vt_out, kmean_out):
    xb = x_ref[...].astype(BF16)
    tm, c = xb.shape
    nh = c // HEAD_DIM
    half = HEAD_DIM // 2

    qt = _nt_dot(wqt_ref[...], xb)
    cos_t, sin_t = cos_t_ref[...], sin_t_ref[...]
    scale = HEAD_DIM ** -0.5 * LOG2_E
    for h in range(nh):
        q1 = qt[h * HEAD_DIM:h * HEAD_DIM + half]
        q2 = qt[h * HEAD_DIM + half:(h + 1) * HEAD_DIM]
        qt_out[h, 0:half, :] = ((q1 * cos_t - q2 * sin_t) * scale).astype(BF16)
        qt_out[h, half:HEAD_DIM, :] = ((q2 * cos_t + q1 * sin_t) * scale).astype(BF16)

    vt = _nt_dot(wvt_ref[...], xb)
    for h in range(nh):
        vt_out[h] = vt[h * HEAD_DIM:(h + 1) * HEAD_DIM].astype(BF16)

    k = jnp.dot(xb, wk_ref[...], preferred_element_type=F32)
    reps = c // cos_k_ref.shape[1]
    cos_k = jnp.concatenate([cos_k_ref[...]] * reps, axis=1)
    sin_k = jnp.concatenate([sin_k_ref[...]] * reps, axis=1)
    lane = lax.broadcasted_iota(jnp.int32, k.shape, 1)
    first = (lane % HEAD_DIM) < half
    rot = jnp.where(first, pltpu.roll(k, c - half, 1), pltpu.roll(k, half, 1))
    k = k * cos_k + rot * sin_k
    for h in range(nh):
        k_out[h] = k[:, h * HEAD_DIM:(h + 1) * HEAD_DIM].astype(BF16)
    for blk in range(tm // MOBA_BLOCK):
        kmean_out[blk] = jnp.mean(k[blk * MOBA_BLOCK:(blk + 1) * MOBA_BLOCK], axis=0, keepdims=True)


def _qkv_rope(x, wq_t, wk, wv_t, cos_t, sin_t, cos_k, sin_k, tm):
    t, c = x.shape
    nh = c // HEAD_DIM
    half = HEAD_DIM // 2
    nb = t // MOBA_BLOCK
    return pl.pallas_call(
        _qkv_kernel,
        grid=(t // tm,),
        in_specs=[pl.BlockSpec((tm, c), lambda i: (i, 0)),
                  _full(wq_t.shape), _full(wk.shape), _full(wv_t.shape),
                  pl.BlockSpec((half, tm), lambda i: (0, i)),
                  pl.BlockSpec((half, tm), lambda i: (0, i)),
                  pl.BlockSpec((tm, cos_k.shape[1]), lambda i: (i, 0)),
                  pl.BlockSpec((tm, sin_k.shape[1]), lambda i: (i, 0))],
        out_specs=[pl.BlockSpec((nh, HEAD_DIM, tm), lambda i: (0, 0, i)),
                   pl.BlockSpec((nh, tm, HEAD_DIM), lambda i: (0, i, 0)),
                   pl.BlockSpec((nh, HEAD_DIM, tm), lambda i: (0, 0, i)),
                   pl.BlockSpec((tm // MOBA_BLOCK, 1, c), lambda i: (i, 0, 0))],
        out_shape=[jax.ShapeDtypeStruct((nh, HEAD_DIM, t), BF16),
                   jax.ShapeDtypeStruct((nh, t, HEAD_DIM), BF16),
                   jax.ShapeDtypeStruct((nh, HEAD_DIM, t), BF16),
                   jax.ShapeDtypeStruct((nb, 1, c), F32)],
        compiler_params=_params("parallel"),
    )(x, wq_t, wk, wv_t, cos_t, sin_t, cos_k, sin_k)


def _moba_kernel(qt_ref, k_ref, vt_ref, kmean_ref, o_ref, sel_ref, s_ref):
    i = pl.program_id(1)
    hb, _, bq = qt_ref.shape
    nb = kmean_ref.shape[1]
    blk = MOBA_BLOCK

    qts = [qt_ref[h] for h in range(hb)]
    heads = range(hb)
    ones = jnp.ones((16, blk), BF16)

    def select_blocks():
        bidx = lax.broadcasted_iota(jnp.int32, (nb, bq), 0)
        gates = []
        for h in heads:
            km_hi, km_lo = _split2(kmean_ref[h])
            gate = (jnp.dot(km_hi, qts[h], preferred_element_type=F32)
                    + jnp.dot(km_lo, qts[h], preferred_element_type=F32))
            gates.append(jnp.where(bidx < i, gate, -jnp.inf))
        sels = [jnp.zeros((nb, bq), F32) for _ in heads]
        for _ in range(MOBA_TOPK):
            ms = [jnp.max(g, axis=0, keepdims=True) for g in gates]
            firsts = [jnp.min(jnp.where(jnp.logical_and(gates[h] == ms[h], ms[h] > -jnp.inf), bidx, nb),
                              axis=0, keepdims=True) for h in heads]
            picks = [bidx == f for f in firsts]
            sels = [jnp.where(picks[h], 1.0, sels[h]) for h in heads]
            gates = [jnp.where(picks[h], -jnp.inf, gates[h]) for h in heads]
        for h in heads:
            sel_ref[h] = sels[h]

    def put_scores(j, slot):
        koff = pl.multiple_of(j * blk, blk)
        for h in heads:
            s_ref[slot, h] = jnp.dot(k_ref[h, pl.ds(koff, blk), :], qts[h],
                                     preferred_element_type=F32)

    def attend(j, slot, carry, causal):
        koff = pl.multiple_of(j * blk, blk)
        ss = [s_ref[slot, h] for h in heads]
        if causal:
            key = lax.broadcasted_iota(jnp.int32, (blk, bq), 0)
            qry = lax.broadcasted_iota(jnp.int32, (blk, bq), 1)
            ss = [jnp.where(key <= qry, s, -jnp.inf) for s in ss]
        else:
            ons = [sel_ref[h, pl.ds(j, 1), :] > 0.0 for h in heads]
        cmaxs = [jnp.max(s, axis=0, keepdims=True) for s in ss]
        ms = [carry[h][0] for h in heads]
        if causal:
            m_news = [jnp.maximum(ms[h], cmaxs[h]) for h in heads]
            m_ps = m_news
        else:
            m_news = [jnp.where(ons[h], jnp.maximum(ms[h], cmaxs[h]), ms[h]) for h in heads]
            m_ps = [jnp.where(ons[h], m_news[h], cmaxs[h]) for h in heads]
        ps = [jnp.exp2(ss[h] - m_ps[h]).astype(BF16) for h in heads]
        os = [jnp.dot(jnp.concatenate([vt_ref[h, :, pl.ds(koff, blk)], ones], axis=0), ps[h],
                      preferred_element_type=F32) for h in heads]
        if not causal:
            os = [jnp.where(ons[h], os[h], 0.0) for h in heads]
        return tuple((m_news[h], carry[h][1] * jnp.exp2(ms[h] - m_news[h]) + os[h]) for h in heads)

    init = tuple((jnp.full((1, bq), NEG_BIG, F32), jnp.zeros((HEAD_DIM + 16, bq), F32)) for _ in heads)

    def blocks(j0, count, carry):
        for d in range(count):
            put_scores(j0 + d + 1, (d + 1) % 2)
            carry = attend(j0 + d, d % 2, carry, False)
        return carry

    def finish(carry):
        for h in heads:
            acc = carry[h][1]
            o_ref[h] = acc[:HEAD_DIM] / acc[HEAD_DIM:HEAD_DIM + 1]

    unroll = MOBA_BLOCKS_PER_ITER
    put_scores(0, 0)
    select_blocks()
    carry = lax.fori_loop(0, i // unroll, lambda jj, c: blocks(unroll * jj, unroll, c), init)
    done = (i // unroll) * unroll
    carry = lax.fori_loop(0, (i - done) // 2, lambda jj, c: blocks(done + 2 * jj, 2, c), carry)

    @pl.when(i % 2 == 0)
    def _():
        finish(attend(i, 0, carry, True))

    @pl.when(i % 2 == 1)
    def _():
        put_scores(i, 1)
        finish(attend(i, 1, attend(i - 1, 0, carry, False), True))


def _moba_attention(qt, k, vt, kmean, hb):
    nh, dh, t = qt.shape
    nb = t // MOBA_BLOCK
    once = pl.Buffered(1)
    return pl.pallas_call(
        _moba_kernel,
        grid=(nh // hb, nb),
        in_specs=[pl.BlockSpec((hb, dh, MOBA_BLOCK), lambda g, i: (g, 0, i)),
                  pl.BlockSpec((hb, t, dh), lambda g, i: (g, 0, 0), pipeline_mode=once),
                  pl.BlockSpec((hb, dh, t), lambda g, i: (g, 0, 0), pipeline_mode=once),
                  pl.BlockSpec((hb, nb, dh), lambda g, i: (g, 0, 0))],
        out_specs=pl.BlockSpec((hb, dh, MOBA_BLOCK), lambda g, i: (g, 0, i)),
        out_shape=jax.ShapeDtypeStruct((nh, dh, t), F32),
        scratch_shapes=[pltpu.VMEM((hb, nb, MOBA_BLOCK), F32),
                        pltpu.VMEM((2, hb, MOBA_BLOCK, MOBA_BLOCK), F32)],
        compiler_params=_params("parallel", "arbitrary"),
    )(qt, k, vt, kmean)


def _row(a):
    return a.reshape(1, -1).astype(F32)


def _rwkv_layer(x, p, ln_g, ln_b, hs, ffn, tm):
    r, lw, k, v, kk, b, g, bonus = _rwkv_proj(x, p, hs, tm)
    t = x.shape[0]
    y = _wkv_scan(r, lw, k, v, kk, b, tb_local=min(WKV_LOCAL_TILE, t), tb_state=min(WKV_STATE_TILE, t))
    return _rwkv_out_ffn(y, g, bonus, x, p['gn_g'], p['gn_b'], p['w_o'], ln_g, ln_b, hs, ffn, tm)


def _rope_tables(t):
    inv = ROPE_THETA ** (-jnp.arange(0, HEAD_DIM, 2, dtype=F32) / HEAD_DIM)
    ang = jnp.arange(t, dtype=F32)[:, None] * inv[None, :]
    cos, sin = jnp.cos(ang), jnp.sin(ang)
    reps = LANES // HEAD_DIM
    cos_k = jnp.tile(jnp.concatenate([cos, cos], axis=1), (1, reps))
    sin_k = jnp.tile(jnp.concatenate([-sin, sin], axis=1), (1, reps))
    return cos.T, sin.T, cos_k, sin_k


def _moba_layer(x, w_qkv, w_o, ln_g, ln_b, ffn, tm):
    t, c = x.shape
    nh = c // HEAD_DIM
    wq_t = w_qkv[:, :c].T.astype(BF16)
    wk = w_qkv[:, c:2 * c].astype(BF16)
    wv_t = w_qkv[:, 2 * c:].T.astype(BF16)
    cos_t, sin_t, cos_k, sin_k = _rope_tables(t)
    qt, k, vt, kmean = _qkv_rope(x, wq_t, wk, wv_t, cos_t, sin_t, cos_k, sin_k, tm)
    kmean = jnp.transpose(kmean.reshape(t // MOBA_BLOCK, nh, HEAD_DIM), (1, 0, 2))
    at = _moba_attention(qt, k, vt, kmean, hb=min(MOBA_HEADS_PER_STEP, nh))
    return _attn_out_ffn(at.reshape(c, t), x, w_o.astype(BF16), ln_g, ln_b, ffn, tm)


def kernel(x, rwkv_mu, rwkv_w_rkv, rwkv_w0, rwkv_w1, rwkv_w2, rwkv_a0, rwkv_a1, rwkv_a2, rwkv_g1, rwkv_g2,
           rwkv_k_k, rwkv_k_a, rwkv_r_k, rwkv_gn_g, rwkv_gn_b, rwkv_w_o, moba_w_qkv, moba_w_o,
           ffn_w_in, ffn_w_down, ln_mix_g, ln_mix_b, ln_ffn_g, ln_ffn_b):
    bsz, t, c = x.shape
    assert c % WKV_GROUP == 0 and t % MOBA_BLOCK == 0
    tm = min(ROW_TILE, t)
    hs = _head_ones()
    outs = []
    for bi in range(bsz):
        h = x[bi]
        for i in range(DEPTH):
            j = i // 2
            ffn = (ffn_w_in[i].astype(BF16), ffn_w_down[i].astype(BF16), _row(ln_ffn_g[i]), _row(ln_ffn_b[i]))
            if i % 2 == 0:
                p = dict(mu=rwkv_mu[j], w_rkv=rwkv_w_rkv[j].astype(BF16), w0=_row(rwkv_w0[j]),
                         w1=rwkv_w1[j].astype(BF16), w2=rwkv_w2[j].astype(BF16), a0=_row(rwkv_a0[j]),
                         a1=rwkv_a1[j].astype(BF16), a2=rwkv_a2[j].astype(BF16),
                         g1=rwkv_g1[j].astype(BF16), g2=rwkv_g2[j].astype(BF16),
                         k_k=_row(rwkv_k_k[j]), k_a=_row(rwkv_k_a[j]), r_k=_row(rwkv_r_k[j]),
                         gn_g=_row(rwkv_gn_g[j]), gn_b=_row(rwkv_gn_b[j]), w_o=rwkv_w_o[j].astype(BF16))
                h = _rwkv_layer(h, p, _row(ln_mix_g[i]), _row(ln_mix_b[i]), hs, ffn, tm)
            else:
                h = _moba_layer(h, moba_w_qkv[j], moba_w_o[j], _row(ln_mix_g[i]), _row(ln_mix_b[i]), ffn, tm)
        outs.append(h)
    return jnp.stack(outs, axis=0)
```

```python
import jax
import jax.numpy as jnp
from jax import lax
from jax.experimental import pallas as pl
from jax.experimental.pallas import tpu as pltpu

F32 = jnp.float32
BF16 = jnp.bfloat16

HEAD_DIM = 64
DEPTH = 2
ALPHA = (2 * DEPTH) ** 0.25
LN_EPS = 1e-5
GN_EPS = HEAD_DIM * 1e-5
MOBA_BLOCK = 256
MOBA_TOPK = 3
ROPE_THETA = 10000.0
LOG2_E = 1.4426950408889634
MOBA_HEADS_PER_STEP = 4
MOBA_BLOCKS_PER_ITER = (16, 4, 2)

LANES = 128
MXU_DIM = 256
WKV_CHUNK = 64
WKV_GROUP = MXU_DIM
WKV_LOCAL_TILE = 1024
WKV_STATE_TILE = 256
ROW_TILE = 512
NEG_BIG = -1e30
VMEM_LIMIT = 56 * 1024 * 1024


def _params(*sem):
    return pltpu.CompilerParams(dimension_semantics=sem, vmem_limit_bytes=VMEM_LIMIT)


def _full(shape):
    n = len(shape)
    return pl.BlockSpec(shape, lambda *_: (0,) * n)


def _bdot(a, b):
    return jnp.dot(a.astype(BF16), b.astype(BF16), preferred_element_type=F32)


def _nt_dot(a, b):
    return lax.dot_general(a.astype(BF16), b.astype(BF16), (((1,), (1,)), ((), ())),
                           preferred_element_type=F32)


def _tn_dot(a, b):
    return lax.dot_general(a.astype(BF16), b.astype(BF16), (((0,), (0,)), ((), ())),
                           preferred_element_type=F32)


def _split2(a):
    hi = a.astype(BF16)
    lo = (a - hi.astype(F32)).astype(BF16)
    return hi, lo


def _dot_exact_rhs(a, b_exact):
    if a.dtype == BF16:
        return jnp.dot(a, b_exact, preferred_element_type=F32)
    hi, lo = _split2(a)
    return (jnp.dot(hi, b_exact, preferred_element_type=F32)
            + jnp.dot(lo, b_exact, preferred_element_type=F32))


def _head_sum(a, ones_bd):
    w = ones_bd.shape[0]
    return jnp.concatenate([_dot_exact_rhs(a[:, s:s + w], ones_bd) for s in range(0, a.shape[1], w)],
                           axis=1)


def _layer_norm(z, g, b):
    mu = jnp.mean(z, axis=-1, keepdims=True)
    d = z - mu
    var = jnp.mean(d * d, axis=-1, keepdims=True)
    return d * lax.rsqrt(var + LN_EPS) * g + b


def _head_ones():
    h = jnp.arange(MXU_DIM) // HEAD_DIM
    return (h[:, None] == h[None, :]).astype(BF16)


def _rwkv_proj_kernel(x_ref, xp_ref, mu_ref, wrkv_ref, w0_ref, w1_ref, w2_ref, a0_ref, a1_ref, a2_ref,
                      g1_ref, g2_ref, kk_w_ref, ka_w_ref, rk_w_ref, hs_ref,
                      r_out, lw_out, k_out, v_out, kk_out, b_out, g_out, bonus_out):
    i = pl.program_id(0)
    x = x_ref[...]
    prev_last = jnp.where(i > 0, xp_ref[7:8, :], 0.0)
    row = lax.broadcasted_iota(jnp.int32, x.shape, 0)
    shifted = jnp.where(row == 0, prev_last, pltpu.roll(x, 1, 0))
    xx = shifted - x
    mix = lambda n: x + xx * mu_ref[n:n + 1, :]
    hs = hs_ref[...]

    r = _bdot(mix(0), wrkv_ref[0])
    k = _bdot(mix(2), wrkv_ref[1])
    v = _bdot(mix(3), wrkv_ref[2])
    zw = w0_ref[...] + _bdot(jnp.tanh(_bdot(mix(1), w1_ref[...])), w2_ref[...])
    u = -zw
    w_log = -(jnp.maximum(u, 0.0) + jnp.log(1.0 + jnp.exp(-jnp.abs(u)))) - 0.5
    a = jax.nn.sigmoid(a0_ref[...] + _bdot(_bdot(mix(4), a1_ref[...]), a2_ref[...]))
    g = _bdot(jax.nn.sigmoid(_bdot(mix(5), g1_ref[...])), g2_ref[...])

    kk = k * kk_w_ref[...]
    norm = jnp.sqrt(_head_sum(kk * kk, hs))
    kk = kk / jnp.maximum(norm, 1e-12)
    k = k * (1.0 + (a - 1.0) * ka_w_ref[...])

    r_out[...] = r.astype(BF16)
    lw_out[...] = -jnp.exp(w_log)
    k_out[...] = k.astype(BF16)
    v_out[...] = v.astype(BF16)
    kk_out[...] = kk.astype(BF16)
    b_out[...] = (kk * a).astype(BF16)
    g_out[...] = g.astype(BF16)
    bonus_out[...] = (_head_sum(r * k * rk_w_ref[...], hs) * v).astype(BF16)


def _rwkv_proj(x, p, hs, tm):
    t, c = x.shape
    row = lambda: pl.BlockSpec((tm, c), lambda i: (i, 0))
    prev = pl.BlockSpec((8, c), lambda i: (jnp.maximum(i * (tm // 8) - 1, 0), 0))
    ins = [x, x, p['mu'], p['w_rkv'], p['w0'], p['w1'], p['w2'], p['a0'], p['a1'], p['a2'],
           p['g1'], p['g2'], p['k_k'], p['k_a'], p['r_k'], hs]
    in_specs = [row(), prev] + [_full(a.shape) for a in ins[2:]]
    f32 = jax.ShapeDtypeStruct((t, c), F32)
    bf16 = jax.ShapeDtypeStruct((t, c), BF16)
    return pl.pallas_call(
        _rwkv_proj_kernel,
        grid=(t // tm,),
        in_specs=in_specs,
        out_specs=[row() for _ in range(8)],
        out_shape=[bf16, f32, bf16, bf16, bf16, bf16, bf16, bf16],
        compiler_params=_params("parallel"),
    )(*ins)


def _bmm(a, b):
    return lax.dot_general(a.astype(BF16), b.astype(BF16), (((2,), (1,)), ((0,), (0,))),
                           preferred_element_type=F32)


def _wkv_local_kernel(r_ref, lw_ref, k_ref, v_ref, kk_ref, b_ref,
                      al_out, rt_out, u0_out, y0_out, bh_out, kh_out, pl_out):
    tb, gw = r_ref.shape
    L = WKV_CHUNK
    nc = tb // L
    nh = gw // HEAD_DIM
    n = nh * L

    lane = lax.broadcasted_iota(jnp.int32, (1, 1, gw), 2)
    head_masks = [lane // HEAD_DIM == h for h in range(nh)]
    col = lax.broadcasted_iota(jnp.int32, (1, 1, n), 2)
    col_masks = [col // L == h for h in range(nh)]
    ri = lax.broadcasted_iota(jnp.int32, (1, L, n), 1)
    ci = lax.broadcasted_iota(jnp.int32, (1, L, n), 2) % L
    strict = ci < ri
    incl = ci <= ri
    eye_cat = (ci == ri).astype(F32)
    tt = min(tb, MXU_DIM)
    rb = lax.broadcasted_iota(jnp.int32, (tt, tt), 0)
    cb = lax.broadcasted_iota(jnp.int32, (tt, tt), 1)
    tri = jnp.logical_and(cb <= rb, cb // L == rb // L).astype(BF16)

    def stack(a, masks):
        a16 = a.astype(BF16)
        return jnp.concatenate([jnp.where(m, a16, 0) for m in masks], axis=1)

    lw2 = lw_ref[...]
    hi, lo = _split2(lw2)
    cum2 = jnp.concatenate(
        [jnp.dot(tri, hi[s:s + tt], preferred_element_type=F32)
         + jnp.dot(tri, lo[s:s + tt], preferred_element_type=F32) for s in range(0, tb, tt)], axis=0)
    to3 = lambda a: a.reshape(nc, L, gw)
    cum, lw = to3(cum2), to3(lw2)
    r, k, v, kk, b = (to3(ref[...]) for ref in (r_ref, k_ref, v_ref, kk_ref, b_ref))
    cum_l = cum[:, L - 1:L, :]
    p_inv = jnp.exp(-cum)
    al = -(kk * jnp.exp(cum - lw))
    bt = b * p_inv
    kt = k * p_inv
    rt = r * jnp.exp(cum)
    dec = jnp.exp(cum_l - cum)

    xs = jnp.concatenate([al, rt], axis=1)
    ys = jnp.concatenate([stack(bt, head_masks), stack(kt, head_masks)], axis=1)
    res = lax.dot_general(xs.astype(BF16), ys.astype(BF16), (((2,), (2,)), ((0,), (0,))),
                          preferred_element_type=F32)
    a_ab = jnp.where(strict, res[:, :L, :n], 0.0)
    a_ak = jnp.where(strict, res[:, :L, n:], 0.0)
    a_rb = jnp.where(incl, res[:, L:, :n], 0.0)
    a_rk = jnp.where(incl, res[:, L:, n:], 0.0)

    tinv = eye_cat + jnp.where(jnp.logical_and(ri // 2 == ci // 2, ci < ri), a_ab, 0.0)
    s = 4
    while s <= L:
        quad = jnp.logical_and(ri // s == ci // s, jnp.logical_and(ri % s >= s // 2, ci % s < s // 2))
        x = _bmm(jnp.where(quad, a_ab, 0.0), stack(tinv, col_masks))
        tinv = tinv + _bmm(tinv, stack(x, col_masks))
        s *= 2

    v_st = stack(v, head_masks)
    al_p = _bmm(tinv, stack(al, head_masks))
    u0 = _bmm(tinv, stack(_bmm(a_ak, v_st), head_masks))
    rt_p = rt + _bmm(a_rb, stack(al_p, head_masks))
    y0 = _bmm(a_rb, stack(u0, head_masks)) + _bmm(a_rk, v_st)

    to2 = lambda a: a.reshape(tb, gw)
    al_out[...] = to2(al_p).astype(BF16)
    rt_out[...] = to2(rt_p).astype(BF16)
    u0_out[...] = to2(u0).astype(BF16)
    y0_out[...] = to2(y0).astype(BF16)
    bh_out[...] = to2(b * dec).astype(BF16)
    kh_out[...] = to2(k * dec).astype(BF16)
    pl_out[...] = jnp.exp(cum_l)


def _wkv_local(r, lw, k, v, kk, b, tb):
    t, c = r.shape
    gw = WKV_GROUP
    nc = tb // WKV_CHUNK
    spec = lambda: pl.BlockSpec((tb, gw), lambda i, g: (i, g))
    f32 = jax.ShapeDtypeStruct((t, c), F32)
    bf16 = jax.ShapeDtypeStruct((t, c), BF16)
    return pl.pallas_call(
        _wkv_local_kernel,
        grid=(t // tb, c // gw),
        in_specs=[spec() for _ in range(6)],
        out_specs=[spec() for _ in range(6)] + [pl.BlockSpec((nc, 1, gw), lambda i, g: (i, 0, g))],
        out_shape=[bf16, bf16, bf16, bf16, bf16, bf16,
                   jax.ShapeDtypeStruct((t // WKV_CHUNK, 1, c), F32)],
        compiler_params=_params("parallel", "parallel"),
    )(r, lw, k, v, kk, b)


def _wkv_state_kernel(al_ref, rt_ref, u0_ref, y0_ref, bh_ref, kh_ref, v_ref, pl_ref, y_ref, sv_ref):
    tb, c = v_ref.shape
    L = WKV_CHUNK
    gw = WKV_GROUP
    groups = range(c // gw)

    @pl.when(pl.program_id(0) == 0)
    def _():
        sv_ref[...] = jnp.zeros_like(sv_ref)

    bd_mask = (lax.broadcasted_iota(jnp.int32, (gw, gw), 0) // HEAD_DIM
               == lax.broadcasted_iota(jnp.int32, (gw, gw), 1) // HEAD_DIM)

    def chunk(ci, carry):
        rows = pl.ds(pl.multiple_of(ci * L, L), L)
        lanes = [slice(g * gw, (g + 1) * gw) for g in groups]
        svs = [sv_ref[g] for g in groups]
        uys = [_nt_dot(jnp.concatenate([al_ref[rows, lanes[g]], rt_ref[rows, lanes[g]]], axis=0), svs[g])
               for g in groups]
        us = [uys[g][:L] + u0_ref[rows, lanes[g]] for g in groups]
        for g in groups:
            y_ref[rows, lanes[g]] = (uys[g][L:] + y0_ref[rows, lanes[g]]).astype(BF16)
        upds = [_tn_dot(jnp.concatenate([us[g].astype(BF16), v_ref[rows, lanes[g]].astype(BF16)], axis=0),
                        jnp.concatenate([bh_ref[rows, lanes[g]], kh_ref[rows, lanes[g]]], axis=0))
                for g in groups]
        for g in groups:
            sv_ref[g] = svs[g] * pl_ref[ci, :, lanes[g]] + jnp.where(bd_mask, upds[g], 0.0)
        return carry

    lax.fori_loop(0, tb // L, chunk, 0)


def _wkv_state(al_p, rt_p, u0, y0, bh, kh, v, p_l, tb):
    t, c = v.shape
    nc = tb // WKV_CHUNK
    row = lambda: pl.BlockSpec((tb, c), lambda i: (i, 0))
    return pl.pallas_call(
        _wkv_state_kernel,
        grid=(t // tb,),
        in_specs=[row() for _ in range(7)] + [pl.BlockSpec((nc, 1, c), lambda i: (i, 0, 0))],
        out_specs=row(),
        out_shape=jax.ShapeDtypeStruct((t, c), BF16),
        scratch_shapes=[pltpu.VMEM((c // WKV_GROUP, WKV_GROUP, WKV_GROUP), F32)],
        compiler_params=_params("arbitrary"),
    )(al_p, rt_p, u0, y0, bh, kh, v, p_l)


def _wkv_scan(r, lw, k, v, kk, b, tb_local, tb_state):
    al_p, rt_p, u0, y0, bh, kh, p_l = _wkv_local(r, lw, k, v, kk, b, tb_local)
    return _wkv_state(al_p, rt_p, u0, y0, bh, kh, v, p_l, tb_state)


def _once(shape):
    n = len(shape)
    return pl.BlockSpec(shape, lambda *_: (0,) * n, pipeline_mode=pl.Buffered(1))


def _ffn_bounds(f):
    mid = min(f, -(-(f // MXU_DIM) // 2) * MXU_DIM) if f % MXU_DIM == 0 else f
    return tuple(b for b in ((0, mid), (mid, f)) if b[1] > b[0])


def _swiglu_ln(x, win_ref, wdown_ref, lng_ref, lnb_ref):
    xb = x.astype(BF16)
    f = wdown_ref.shape[0]
    acc = jnp.zeros(x.shape, F32)
    for lo, hi in _ffn_bounds(f):
        gate = jnp.dot(xb, win_ref[:, lo:hi], preferred_element_type=F32)
        up = jnp.dot(xb, win_ref[:, f + lo:f + hi], preferred_element_type=F32)
        act = (gate * jax.nn.sigmoid(gate) * up).astype(BF16)
        acc = acc + jnp.dot(act, wdown_ref[lo:hi, :], preferred_element_type=F32)
    return _layer_norm(ALPHA * x + acc, lng_ref[...], lnb_ref[...])


def _rwkv_out_ffn_kernel(y_ref, g_ref, bonus_ref, x_ref, gng_ref, gnb_ref, wo_ref, lng_ref, lnb_ref, hs_ref,
                         win_ref, wdown_ref, lng2_ref, lnb2_ref, o_ref):
    hs = hs_ref[...]
    y = y_ref[...]
    mean = _head_sum(y, hs) * (1.0 / HEAD_DIM)
    d = y - mean
    var = _head_sum(d * d, hs) * (1.0 / HEAD_DIM)
    yn = d * lax.rsqrt(var + GN_EPS) * gng_ref[...] + gnb_ref[...]
    z = (yn + bonus_ref[...]) * g_ref[...]
    h = _bdot(z, wo_ref[...])
    x1 = _layer_norm(ALPHA * x_ref[...] + h, lng_ref[...], lnb_ref[...])
    o_ref[...] = _swiglu_ln(x1, win_ref, wdown_ref, lng2_ref, lnb2_ref)


def _rwkv_out_ffn(y, g, bonus, x, gn_g, gn_b, w_o, ln_g, ln_b, hs, ffn, tm):
    t, c = x.shape
    row = lambda: pl.BlockSpec((tm, c), lambda i: (i, 0))
    consts = [gn_g, gn_b, w_o, ln_g, ln_b, hs, *ffn]
    return pl.pallas_call(
        _rwkv_out_ffn_kernel,
        grid=(t // tm,),
        in_specs=[row() for _ in range(4)] + [_once(a.shape) for a in consts],
        out_specs=row(),
        out_shape=jax.ShapeDtypeStruct((t, c), F32),
        compiler_params=_params("parallel"),
    )(y, g, bonus, x, *consts)


def _attn_out_ffn_kernel(at_ref, x_ref, wo_ref, lng_ref, lnb_ref, win_ref, wdown_ref, lng2_ref, lnb2_ref,
                         o_ref):
    h = _tn_dot(at_ref[...], wo_ref[...])
    x1 = _layer_norm(ALPHA * x_ref[...] + h, lng_ref[...], lnb_ref[...])
    o_ref[...] = _swiglu_ln(x1, win_ref, wdown_ref, lng2_ref, lnb2_ref)


def _attn_out_ffn(at, x, w_o, ln_g, ln_b, ffn, tm):
    t, c = x.shape
    consts = [w_o, ln_g, ln_b, *ffn]
    return pl.pallas_call(
        _attn_out_ffn_kernel,
        grid=(t // tm,),
        in_specs=[pl.BlockSpec((c, tm), lambda i: (0, i)),
                  pl.BlockSpec((tm, c), lambda i: (i, 0))] + [_once(a.shape) for a in consts],
        out_specs=pl.BlockSpec((tm, c), lambda i: (i, 0)),
        out_shape=jax.ShapeDtypeStruct((t, c), F32),
        compiler_params=_params("parallel"),
    )(at, x, *consts)


def _qkv_kernel(x_ref, wqt_ref, wk_ref, wvt_ref, cos_t_ref, sin_t_ref, cos_k_ref, sin_k_ref,
                qt_out, k_out, vt_out, kmean_out):
    xb = x_ref[...].astype(BF16)
    tm, c = xb.shape
    nh = c // HEAD_DIM
    half = HEAD_DIM // 2

    qt = _nt_dot(wqt_ref[...], xb)
    cos_t, sin_t = cos_t_ref[...], sin_t_ref[...]
    scale = HEAD_DIM ** -0.5 * LOG2_E
    for h in range(nh):
        q1 = qt[h * HEAD_DIM:h * HEAD_DIM + half]
        q2 = qt[h * HEAD_DIM + half:(h + 1) * HEAD_DIM]
        qt_out[h, 0:half, :] = ((q1 * cos_t - q2 * sin_t) * scale).astype(BF16)
        qt_out[h, half:HEAD_DIM, :] = ((q2 * cos_t + q1 * sin_t) * scale).astype(BF16)

    vt = _nt_dot(wvt_ref[...], xb)
    for h in range(nh):
        vt_out[h] = vt[h * HEAD_DIM:(h + 1) * HEAD_DIM].astype(BF16)

    k = jnp.dot(xb, wk_ref[...], preferred_element_type=F32)
    reps = c // cos_k_ref.shape[1]
    cos_k = jnp.concatenate([cos_k_ref[...]] * reps, axis=1)
    sin_k = jnp.concatenate([sin_k_ref[...]] * reps, axis=1)
    lane = lax.broadcasted_iota(jnp.int32, k.shape, 1)
    first = (lane % HEAD_DIM) < half
    rot = jnp.where(first, pltpu.roll(k, c - half, 1), pltpu.roll(k, half, 1))
    k = k * cos_k + rot * sin_k
    for h in range(nh):
        k_out[h] = k[:, h * HEAD_DIM:(h + 1) * HEAD_DIM].astype(BF16)
    for blk in range(tm // MOBA_BLOCK):
        kmean_out[blk] = jnp.mean(k[blk * MOBA_BLOCK:(blk + 1) * MOBA_BLOCK], axis=0, keepdims=True)


def _qkv_rope(x, wq_t, wk, wv_t, cos_t, sin_t, cos_k, sin_k, tm):
    t, c = x.shape
    nh = c // HEAD_DIM
    half = HEAD_DIM // 2
    nb = t // MOBA_BLOCK
    return pl.pallas_call(
        _qkv_kernel,
        grid=(t // tm,),
        in_specs=[pl.BlockSpec((tm, c), lambda i: (i, 0)),
                  _full(wq_t.shape), _full(wk.shape), _full(wv_t.shape),
                  pl.BlockSpec((half, tm), lambda i: (0, i)),
                  pl.BlockSpec((half, tm), lambda i: (0, i)),
                  pl.BlockSpec((tm, cos_k.shape[1]), lambda i: (i, 0)),
                  pl.BlockSpec((tm, sin_k.shape[1]), lambda i: (i, 0))],
        out_specs=[pl.BlockSpec((nh, HEAD_DIM, tm), lambda i: (0, 0, i)),
                   pl.BlockSpec((nh, tm, HEAD_DIM), lambda i: (0, i, 0)),
                   pl.BlockSpec((nh, HEAD_DIM, tm), lambda i: (0, 0, i)),
                   pl.BlockSpec((tm // MOBA_BLOCK, 1, c), lambda i: (i, 0, 0))],
        out_shape=[jax.ShapeDtypeStruct((nh, HEAD_DIM, t), BF16),
                   jax.ShapeDtypeStruct((nh, t, HEAD_DIM), BF16),
                   jax.ShapeDtypeStruct((nh, HEAD_DIM, t), BF16),
                   jax.ShapeDtypeStruct((nb, 1, c), F32)],
        compiler_params=_params("parallel"),
    )(x, wq_t, wk, wv_t, cos_t, sin_t, cos_k, sin_k)


def _moba_kernel(qt_ref, k_ref, vt_ref, kmean_ref, o_ref, sel_ref, s_ref):
    i = pl.program_id(1)
    hb, _, bq = qt_ref.shape
    nb = kmean_ref.shape[1]
    blk = MOBA_BLOCK

    qts = [qt_ref[h] for h in range(hb)]
    heads = range(hb)
    ones = jnp.ones((16, blk), BF16)

    def select_blocks():
        bidx = lax.broadcasted_iota(jnp.int32, (nb, bq), 0)
        gates = []
        for h in heads:
            km_hi, km_lo = _split2(kmean_ref[h])
            gate = (jnp.dot(km_hi, qts[h], preferred_element_type=F32)
                    + jnp.dot(km_lo, qts[h], preferred_element_type=F32))
            gates.append(jnp.where(bidx < i, gate, -jnp.inf))
        sels = [jnp.zeros((nb, bq), F32) for _ in heads]
        for _ in range(MOBA_TOPK):
            ms = [jnp.max(g, axis=0, keepdims=True) for g in gates]
            firsts = [jnp.min(jnp.where(jnp.logical_and(gates[h] == ms[h], ms[h] > -jnp.inf), bidx, nb),
                              axis=0, keepdims=True) for h in heads]
            picks = [bidx == f for f in firsts]
            sels = [jnp.where(picks[h], 1.0, sels[h]) for h in heads]
            gates = [jnp.where(picks[h], -jnp.inf, gates[h]) for h in heads]
        for h in heads:
            sel_ref[h] = sels[h]

    def put_scores(j, slot):
        koff = pl.multiple_of(j * blk, blk)
        for h in heads:
            s_ref[slot, h] = jnp.dot(k_ref[h, pl.ds(koff, blk), :], qts[h],
                                     preferred_element_type=F32)

    def attend(j, slot, carry, causal):
        koff = pl.multiple_of(j * blk, blk)
        ss = [s_ref[slot, h] for h in heads]
        if causal:
            key = lax.broadcasted_iota(jnp.int32, (blk, bq), 0)
            qry = lax.broadcasted_iota(jnp.int32, (blk, bq), 1)
            ss = [jnp.where(key <= qry, s, -jnp.inf) for s in ss]
        else:
            ons = [sel_ref[h, pl.ds(j, 1), :] > 0.0 for h in heads]
        cmaxs = [jnp.max(s, axis=0, keepdims=True) for s in ss]
        ms = [carry[h][0] for h in heads]
        if causal:
            m_news = [jnp.maximum(ms[h], cmaxs[h]) for h in heads]
            m_ps = m_news
        else:
            m_news = [jnp.where(ons[h], jnp.maximum(ms[h], cmaxs[h]), ms[h]) for h in heads]
            m_ps = [jnp.where(ons[h], m_news[h], cmaxs[h]) for h in heads]
        ps = [jnp.exp2(ss[h] - m_ps[h]).astype(BF16) for h in heads]
        os = [jnp.dot(jnp.concatenate([vt_ref[h, :, pl.ds(koff, blk)], ones], axis=0), ps[h],
                      preferred_element_type=F32) for h in heads]
        if not causal:
            os = [jnp.where(ons[h], os[h], 0.0) for h in heads]
        return tuple((m_news[h], carry[h][1] * jnp.exp2(ms[h] - m_news[h]) + os[h]) for h in heads)

    init = tuple((jnp.full((1, bq), NEG_BIG, F32), jnp.zeros((HEAD_DIM + 16, bq), F32)) for _ in heads)

    def blocks(j0, count, carry):
        for d in range(count):
            put_scores(j0 + d + 1, (d + 1) % 2)
            carry = attend(j0 + d, d % 2, carry, False)
        return carry

    def finish(carry):
        for h in heads:
            acc = carry[h][1]
            o_ref[h] = acc[:HEAD_DIM] / acc[HEAD_DIM:HEAD_DIM + 1]

    put_scores(0, 0)
    select_blocks()
    carry, done = init, 0
    for u in MOBA_BLOCKS_PER_ITER:
        trips = (i - done) // u
        carry = lax.fori_loop(0, trips, lambda jj, c, u=u, done=done: blocks(done + u * jj, u, c), carry)
        done = done + trips * u

    @pl.when(i % 2 == 0)
    def _():
        finish(attend(i, 0, carry, True))

    @pl.when(i % 2 == 1)
    def _():
        put_scores(i, 1)
        finish(attend(i, 1, attend(i - 1, 0, carry, False), True))


def _moba_attention(qt, k, vt, kmean, hb):
    nh, dh, t = qt.shape
    nb = t // MOBA_BLOCK
    once = pl.Buffered(1)
    return pl.pallas_call(
        _moba_kernel,
        grid=(nh // hb, nb),
        in_specs=[pl.BlockSpec((hb, dh, MOBA_BLOCK), lambda g, i: (g, 0, i)),
                  pl.BlockSpec((hb, t, dh), lambda g, i: (g, 0, 0), pipeline_mode=once),
                  pl.BlockSpec((hb, dh, t), lambda g, i: (g, 0, 0), pipeline_mode=once),
                  pl.BlockSpec((hb, nb, dh), lambda g, i: (g, 0, 0))],
        out_specs=pl.BlockSpec((hb, dh, MOBA_BLOCK), lambda g, i: (g, 0, i)),
        out_shape=jax.ShapeDtypeStruct((nh, dh, t), F32),
        scratch_shapes=[pltpu.VMEM((hb, nb, MOBA_BLOCK), F32),
                        pltpu.VMEM((2, hb, MOBA_BLOCK, MOBA_BLOCK), F32)],
        compiler_params=_params("parallel", "arbitrary"),
    )(qt, k, vt, kmean)


def _row(a):
    return a.reshape(1, -1).astype(F32)


def _rwkv_layer(x, p, ln_g, ln_b, hs, ffn, tm):
    r, lw, k, v, kk, b, g, bonus = _rwkv_proj(x, p, hs, tm)
    t = x.shape[0]
    y = _wkv_scan(r, lw, k, v, kk, b, tb_local=min(WKV_LOCAL_TILE, t), tb_state=min(WKV_STATE_TILE, t))
    return _rwkv_out_ffn(y, g, bonus, x, p['gn_g'], p['gn_b'], p['w_o'], ln_g, ln_b, hs, ffn, tm)


def _rope_tables(t):
    inv = ROPE_THETA ** (-jnp.arange(0, HEAD_DIM, 2, dtype=F32) / HEAD_DIM)
    ang = jnp.arange(t, dtype=F32)[:, None] * inv[None, :]
    cos, sin = jnp.cos(ang), jnp.sin(ang)
    reps = LANES // HEAD_DIM
    cos_k = jnp.tile(jnp.concatenate([cos, cos], axis=1), (1, reps))
    sin_k = jnp.tile(jnp.concatenate([-sin, sin], axis=1), (1, reps))
    return cos.T, sin.T, cos_k, sin_k


def _moba_layer(x, w_qkv, w_o, ln_g, ln_b, ffn, tm):
    t, c = x.shape
    nh = c // HEAD_DIM
    wq_t = w_qkv[:, :c].T.astype(BF16)
    wk = w_qkv[:, c:2 * c].astype(BF16)
    wv_t = w_qkv[:, 2 * c:].T.astype(BF16)
    cos_t, sin_t, cos_k, sin_k = _rope_tables(t)
    qt, k, vt, kmean = _qkv_rope(x, wq_t, wk, wv_t, cos_t, sin_t, cos_k, sin_k, tm)
    kmean = jnp.transpose(kmean.reshape(t // MOBA_BLOCK, nh, HEAD_DIM), (1, 0, 2))
    at = _moba_attention(qt, k, vt, kmean, hb=min(MOBA_HEADS_PER_STEP, nh))
    return _attn_out_ffn(at.reshape(c, t), x, w_o.astype(BF16), ln_g, ln_b, ffn, tm)


def kernel(x, rwkv_mu, rwkv_w_rkv, rwkv_w0, rwkv_w1, rwkv_w2, rwkv_a0, rwkv_a1, rwkv_a2, rwkv_g1, rwkv_g2,
           rwkv_k_k, rwkv_k_a, rwkv_r_k, rwkv_gn_g, rwkv_gn_b, rwkv_w_o, moba_w_qkv, moba_w_o,
           ffn_w_in, ffn_w_down, ln_mix_g, ln_mix_b, ln_ffn_g, ln_ffn_b):
    bsz, t, c = x.shape
    assert c % WKV_GROUP == 0 and t % MOBA_BLOCK == 0
    tm = min(ROW_TILE, t)
    hs = _head_ones()
    outs = []
    for bi in range(bsz):
        h = x[bi]
        for i in range(DEPTH):
            j = i // 2
            ffn = (ffn_w_in[i].astype(BF16), ffn_w_down[i].astype(BF16), _row(ln_ffn_g[i]), _row(ln_ffn_b[i]))
            if i % 2 == 0:
                p = dict(mu=rwkv_mu[j], w_rkv=rwkv_w_rkv[j].astype(BF16), w0=_row(rwkv_w0[j]),
                         w1=rwkv_w1[j].astype(BF16), w2=rwkv_w2[j].astype(BF16), a0=_row(rwkv_a0[j]),
                         a1=rwkv_a1[j].astype(BF16), a2=rwkv_a2[j].astype(BF16),
                         g1=rwkv_g1[j].astype(BF16), g2=rwkv_g2[j].astype(BF16),
                         k_k=_row(rwkv_k_k[j]), k_a=_row(rwkv_k_a[j]), r_k=_row(rwkv_r_k[j]),
                         gn_g=_row(rwkv_gn_g[j]), gn_b=_row(rwkv_gn_b[j]), w_o=rwkv_w_o[j].astype(BF16))
                h = _rwkv_layer(h, p, _row(ln_mix_g[i]), _row(ln_mix_b[i]), hs, ffn, tm)
            else:
                h = _moba_layer(h, moba_w_qkv[j], moba_w_o[j], _row(ln_mix_g[i]), _row(ln_mix_b[i]), ffn, tm)
        outs.append(h)
    return jnp.stack(outs, axis=0)
```

```python
import jax
import jax.numpy as jnp
from jax import lax
from jax.experimental import pallas as pl
from jax.experimental.pallas import tpu as pltpu

F32 = jnp.float32
BF16 = jnp.bfloat16

HEAD_DIM = 64
DEPTH = 2
ALPHA = (2 * DEPTH) ** 0.25
LN_EPS = 1e-5
GN_EPS = HEAD_DIM * 1e-5
MOBA_BLOCK = 256
MOBA_TOPK = 3
ROPE_THETA = 10000.0
LOG2_E = 1.4426950408889634
MOBA_HEADS_PER_STEP = 4
MOBA_BLOCKS_PER_ITER = (16, 4, 2)

LANES = 128
MXU_DIM = 256
WKV_CHUNK = 64
WKV_GROUP = MXU_DIM
WKV_LOCAL_TILE = 1024
WKV_STATE_TILE = 256
ROW_TILE = 512
NEG_BIG = -1e30
VMEM_LIMIT = 56 * 1024 * 1024


def _params(*sem):
    return pltpu.CompilerParams(dimension_semantics=sem, vmem_limit_bytes=VMEM_LIMIT)


def _full(shape):
    n = len(shape)
    return pl.BlockSpec(shape, lambda *_: (0,) * n)


def _bdot(a, b):
    return jnp.dot(a.astype(BF16), b.astype(BF16), preferred_element_type=F32)


def _nt_dot(a, b):
    return lax.dot_general(a.astype(BF16), b.astype(BF16), (((1,), (1,)), ((), ())),
                           preferred_element_type=F32)


def _tn_dot(a, b):
    return lax.dot_general(a.astype(BF16), b.astype(BF16), (((0,), (0,)), ((), ())),
                           preferred_element_type=F32)


def _split2(a):
    hi = a.astype(BF16)
    lo = (a - hi.astype(F32)).astype(BF16)
    return hi, lo


def _dot_exact_rhs(a, b_exact):
    if a.dtype == BF16:
        return jnp.dot(a, b_exact, preferred_element_type=F32)
    hi, lo = _split2(a)
    return (jnp.dot(hi, b_exact, preferred_element_type=F32)
            + jnp.dot(lo, b_exact, preferred_element_type=F32))


def _head_sum(a, ones_bd):
    w = ones_bd.shape[0]
    return jnp.concatenate([_dot_exact_rhs(a[:, s:s + w], ones_bd) for s in range(0, a.shape[1], w)],
                           axis=1)


def _layer_norm(z, g, b):
    mu = jnp.mean(z, axis=-1, keepdims=True)
    d = z - mu
    var = jnp.mean(d * d, axis=-1, keepdims=True)
    return d * lax.rsqrt(var + LN_EPS) * g + b


def _head_ones():
    h = jnp.arange(MXU_DIM) // HEAD_DIM
    return (h[:, None] == h[None, :]).astype(BF16)


def _rwkv_proj_kernel(x_ref, xp_ref, mu_ref, wrkv_ref, w0_ref, w1_ref, w2_ref, a0_ref, a1_ref, a2_ref,
                      g1_ref, g2_ref, kk_w_ref, ka_w_ref, rk_w_ref, hs_ref,
                      r_out, lw_out, k_out, v_out, kk_out, b_out, g_out, bonus_out):
    i = pl.program_id(0)
    x = x_ref[...]
    prev_last = jnp.where(i > 0, xp_ref[7:8, :], 0.0)
    row = lax.broadcasted_iota(jnp.int32, x.shape, 0)
    shifted = jnp.where(row == 0, prev_last, pltpu.roll(x, 1, 0))
    xx = shifted - x
    mix = lambda n: x + xx * mu_ref[n:n + 1, :]
    hs = hs_ref[...]

    r = _bdot(mix(0), wrkv_ref[0])
    k = _bdot(mix(2), wrkv_ref[1])
    v = _bdot(mix(3), wrkv_ref[2])
    zw = w0_ref[...] + _bdot(jnp.tanh(_bdot(mix(1), w1_ref[...])), w2_ref[...])
    u = -zw
    w_log = -(jnp.maximum(u, 0.0) + jnp.log(1.0 + jnp.exp(-jnp.abs(u)))) - 0.5
    a = jax.nn.sigmoid(a0_ref[...] + _bdot(_bdot(mix(4), a1_ref[...]), a2_ref[...]))
    g = _bdot(jax.nn.sigmoid(_bdot(mix(5), g1_ref[...])), g2_ref[...])

    kk = k * kk_w_ref[...]
    norm = jnp.sqrt(_head_sum(kk * kk, hs))
    kk = kk / jnp.maximum(norm, 1e-12)
    k = k * (1.0 + (a - 1.0) * ka_w_ref[...])

    r_out[...] = r.astype(BF16)
    lw_out[...] = -jnp.exp(w_log)
    k_out[...] = k.astype(BF16)
    v_out[...] = v.astype(BF16)
    kk_out[...] = kk.astype(BF16)
    b_out[...] = (kk * a).astype(BF16)
    g_out[...] = g.astype(BF16)
    bonus_out[...] = (_head_sum(r * k * rk_w_ref[...], hs) * v).astype(BF16)


def _rwkv_proj(x, p, hs, tm):
    t, c = x.shape
    row = lambda: pl.BlockSpec((tm, c), lambda i: (i, 0))
    prev = pl.BlockSpec((8, c), lambda i: (jnp.maximum(i * (tm // 8) - 1, 0), 0))
    ins = [x, x, p['mu'], p['w_rkv'], p['w0'], p['w1'], p['w2'], p['a0'], p['a1'], p['a2'],
           p['g1'], p['g2'], p['k_k'], p['k_a'], p['r_k'], hs]
    in_specs = [row(), prev] + [_full(a.shape) for a in ins[2:]]
    f32 = jax.ShapeDtypeStruct((t, c), F32)
    bf16 = jax.ShapeDtypeStruct((t, c), BF16)
    return pl.pallas_call(
        _rwkv_proj_kernel,
        grid=(t // tm,),
        in_specs=in_specs,
        out_specs=[row() for _ in range(8)],
        out_shape=[bf16, f32, bf16, bf16, bf16, bf16, bf16, bf16],
        compiler_params=_params("parallel"),
    )(*ins)


def _bmm(a, b):
    return lax.dot_general(a.astype(BF16), b.astype(BF16), (((2,), (1,)), ((0,), (0,))),
                           preferred_element_type=F32)


def _wkv_local_kernel(r_ref, lw_ref, k_ref, v_ref, kk_ref, b_ref,
                      al_out, rt_out, u0_out, y0_out, bh_out, kh_out, pl_out):
    tb, gw = r_ref.shape
    L = WKV_CHUNK
    nc = tb // L
    nh = gw // HEAD_DIM
    n = nh * L

    lane = lax.broadcasted_iota(jnp.int32, (1, 1, gw), 2)
    head_masks = [lane // HEAD_DIM == h for h in range(nh)]
    col = lax.broadcasted_iota(jnp.int32, (1, 1, n), 2)
    col_masks = [col // L == h for h in range(nh)]
    ri = lax.broadcasted_iota(jnp.int32, (1, L, n), 1)
    ci = lax.broadcasted_iota(jnp.int32, (1, L, n), 2) % L
    strict = ci < ri
    incl = ci <= ri
    eye_cat = (ci == ri).astype(F32)
    tt = min(tb, MXU_DIM)
    rb = lax.broadcasted_iota(jnp.int32, (tt, tt), 0)
    cb = lax.broadcasted_iota(jnp.int32, (tt, tt), 1)
    tri = jnp.logical_and(cb <= rb, cb // L == rb // L).astype(BF16)

    def stack(a, masks):
        a16 = a.astype(BF16)
        return jnp.concatenate([jnp.where(m, a16, 0) for m in masks], axis=1)

    lw2 = lw_ref[...]
    hi, lo = _split2(lw2)
    cum2 = jnp.concatenate(
        [jnp.dot(tri, hi[s:s + tt], preferred_element_type=F32)
         + jnp.dot(tri, lo[s:s + tt], preferred_element_type=F32) for s in range(0, tb, tt)], axis=0)
    to3 = lambda a: a.reshape(nc, L, gw)
    cum, lw = to3(cum2), to3(lw2)
    r, k, v, kk, b = (to3(ref[...]) for ref in (r_ref, k_ref, v_ref, kk_ref, b_ref))
    cum_l = cum[:, L - 1:L, :]
    p_inv = jnp.exp(-cum)
    al = -(kk * jnp.exp(cum - lw))
    bt = b * p_inv
    kt = k * p_inv
    rt = r * jnp.exp(cum)
    dec = jnp.exp(cum_l - cum)

    xs = jnp.concatenate([al, rt], axis=1)
    ys = jnp.concatenate([stack(bt, head_masks), stack(kt, head_masks)], axis=1)
    res = lax.dot_general(xs.astype(BF16), ys.astype(BF16), (((2,), (2,)), ((0,), (0,))),
                          preferred_element_type=F32)
    a_ab = jnp.where(strict, res[:, :L, :n], 0.0)
    a_ak = jnp.where(strict, res[:, :L, n:], 0.0)
    a_rb = jnp.where(incl, res[:, L:, :n], 0.0)
    a_rk = jnp.where(incl, res[:, L:, n:], 0.0)

    tinv = eye_cat + jnp.where(jnp.logical_and(ri // 2 == ci // 2, ci < ri), a_ab, 0.0)
    s = 4
    while s <= L:
        quad = jnp.logical_and(ri // s == ci // s, jnp.logical_and(ri % s >= s // 2, ci % s < s // 2))
        x = _bmm(jnp.where(quad, a_ab, 0.0), stack(tinv, col_masks))
        tinv = tinv + _bmm(tinv, stack(x, col_masks))
        s *= 2

    v_st = stack(v, head_masks)
    al_p = _bmm(tinv, stack(al, head_masks))
    u0 = _bmm(tinv, stack(_bmm(a_ak, v_st), head_masks))
    rt_p = rt + _bmm(a_rb, stack(al_p, head_masks))
    y0 = _bmm(a_rb, stack(u0, head_masks)) + _bmm(a_rk, v_st)

    to2 = lambda a: a.reshape(tb, gw)
    al_out[...] = to2(al_p).astype(BF16)
    rt_out[...] = to2(rt_p).astype(BF16)
    u0_out[...] = to2(u0).astype(BF16)
    y0_out[...] = to2(y0).astype(BF16)
    bh_out[...] = to2(b * dec).astype(BF16)
    kh_out[...] = to2(k * dec).astype(BF16)
    pl_out[...] = jnp.exp(cum_l)


def _wkv_local(r, lw, k, v, kk, b, tb):
    t, c = r.shape
    gw = WKV_GROUP
    nc = tb // WKV_CHUNK
    spec = lambda: pl.BlockSpec((tb, gw), lambda i, g: (i, g))
    f32 = jax.ShapeDtypeStruct((t, c), F32)
    bf16 = jax.ShapeDtypeStruct((t, c), BF16)
    return pl.pallas_call(
        _wkv_local_kernel,
        grid=(t // tb, c // gw),
        in_specs=[spec() for _ in range(6)],
        out_specs=[spec() for _ in range(6)] + [pl.BlockSpec((nc, 1, gw), lambda i, g: (i, 0, g))],
        out_shape=[bf16, bf16, bf16, bf16, bf16, bf16,
                   jax.ShapeDtypeStruct((t // WKV_CHUNK, 1, c), F32)],
        compiler_params=_params("parallel", "parallel"),
    )(r, lw, k, v, kk, b)


def _wkv_state_kernel(al_ref, rt_ref, u0_ref, y0_ref, bh_ref, kh_ref, v_ref, pl_ref, y_ref, sv_ref):
    tb, c = v_ref.shape
    L = WKV_CHUNK
    gw = WKV_GROUP
    groups = range(c // gw)

    @pl.when(pl.program_id(0) == 0)
    def _():
        sv_ref[...] = jnp.zeros_like(sv_ref)

    bd_mask = (lax.broadcasted_iota(jnp.int32, (gw, gw), 0) // HEAD_DIM
               == lax.broadcasted_iota(jnp.int32, (gw, gw), 1) // HEAD_DIM)

    def chunk(ci, carry):
        rows = pl.ds(pl.multiple_of(ci * L, L), L)
        lanes = [slice(g * gw, (g + 1) * gw) for g in groups]
        svs = [sv_ref[g] for g in groups]
        uys = [_nt_dot(jnp.concatenate([al_ref[rows, lanes[g]], rt_ref[rows, lanes[g]]], axis=0), svs[g])
               for g in groups]
        us = [uys[g][:L] + u0_ref[rows, lanes[g]] for g in groups]
        for g in groups:
            y_ref[rows, lanes[g]] = (uys[g][L:] + y0_ref[rows, lanes[g]]).astype(BF16)
        upds = [_tn_dot(jnp.concatenate([us[g].astype(BF16), v_ref[rows, lanes[g]].astype(BF16)], axis=0),
                        jnp.concatenate([bh_ref[rows, lanes[g]], kh_ref[rows, lanes[g]]], axis=0))
                for g in groups]
        for g in groups:
            sv_ref[g] = svs[g] * pl_ref[ci, :, lanes[g]] + jnp.where(bd_mask, upds[g], 0.0)
        return carry

    lax.fori_loop(0, tb // L, chunk, 0)


def _wkv_state(al_p, rt_p, u0, y0, bh, kh, v, p_l, tb):
    t, c = v.shape
    nc = tb // WKV_CHUNK
    row = lambda: pl.BlockSpec((tb, c), lambda i: (i, 0))
    return pl.pallas_call(
        _wkv_state_kernel,
        grid=(t // tb,),
        in_specs=[row() for _ in range(7)] + [pl.BlockSpec((nc, 1, c), lambda i: (i, 0, 0))],
        out_specs=row(),
        out_shape=jax.ShapeDtypeStruct((t, c), BF16),
        scratch_shapes=[pltpu.VMEM((c // WKV_GROUP, WKV_GROUP, WKV_GROUP), F32)],
        compiler_params=_params("arbitrary"),
    )(al_p, rt_p, u0, y0, bh, kh, v, p_l)


def _wkv_scan(r, lw, k, v, kk, b, tb_local, tb_state):
    al_p, rt_p, u0, y0, bh, kh, p_l = _wkv_local(r, lw, k, v, kk, b, tb_local)
    return _wkv_state(al_p, rt_p, u0, y0, bh, kh, v, p_l, tb_state)


def _once(shape):
    n = len(shape)
    return pl.BlockSpec(shape, lambda *_: (0,) * n, pipeline_mode=pl.Buffered(1))


def _ffn_bounds(f):
    mid = min(f, -(-(f // MXU_DIM) // 2) * MXU_DIM) if f % MXU_DIM == 0 else f
    return tuple(b for b in ((0, mid), (mid, f)) if b[1] > b[0])


def _swiglu_ln(x, win_ref, wdown_ref, lng_ref, lnb_ref):
    xb = x.astype(BF16)
    f = wdown_ref.shape[0]
    acc = jnp.zeros(x.shape, F32)
    for lo, hi in _ffn_bounds(f):
        gate = jnp.dot(xb, win_ref[:, lo:hi], preferred_element_type=F32)
        up = jnp.dot(xb, win_ref[:, f + lo:f + hi], preferred_element_type=F32)
        act = (gate * jax.nn.sigmoid(gate) * up).astype(BF16)
        acc = acc + jnp.dot(act, wdown_ref[lo:hi, :], preferred_element_type=F32)
    return _layer_norm(ALPHA * x + acc, lng_ref[...], lnb_ref[...])


def _rwkv_out_ffn_kernel(y_ref, g_ref, bonus_ref, x_ref, gng_ref, gnb_ref, wo_ref, lng_ref, lnb_ref, hs_ref,
                         win_ref, wdown_ref, lng2_ref, lnb2_ref, o_ref):
    hs = hs_ref[...]
    y = y_ref[...]
    mean = _head_sum(y, hs) * (1.0 / HEAD_DIM)
    d = y - mean
    var = _head_sum(d * d, hs) * (1.0 / HEAD_DIM)
    yn = d * lax.rsqrt(var + GN_EPS) * gng_ref[...] + gnb_ref[...]
    z = (yn + bonus_ref[...]) * g_ref[...]
    h = _bdot(z, wo_ref[...])
    x1 = _layer_norm(ALPHA * x_ref[...] + h, lng_ref[...], lnb_ref[...])
    o_ref[...] = _swiglu_ln(x1, win_ref, wdown_ref, lng2_ref, lnb2_ref)


def _rwkv_out_ffn(y, g, bonus, x, gn_g, gn_b, w_o, ln_g, ln_b, hs, ffn, tm):
    t, c = x.shape
    row = lambda: pl.BlockSpec((tm, c), lambda i: (i, 0))
    consts = [gn_g, gn_b, w_o, ln_g, ln_b, hs, *ffn]
    return pl.pallas_call(
        _rwkv_out_ffn_kernel,
        grid=(t // tm,),
        in_specs=[row() for _ in range(4)] + [_once(a.shape) for a in consts],
        out_specs=row(),
        out_shape=jax.ShapeDtypeStruct((t, c), F32),
        compiler_params=_params("parallel"),
    )(y, g, bonus, x, *consts)


def _attn_out_ffn_kernel(at_ref, x_ref, wo_ref, lng_ref, lnb_ref, win_ref, wdown_ref, lng2_ref, lnb2_ref,
                         o_ref):
    h = _tn_dot(at_ref[...], wo_ref[...])
    x1 = _layer_norm(ALPHA * x_ref[...] + h, lng_ref[...], lnb_ref[...])
    o_ref[...] = _swiglu_ln(x1, win_ref, wdown_ref, lng2_ref, lnb2_ref)


def _attn_out_ffn(at, x, w_o, ln_g, ln_b, ffn, tm):
    t, c = x.shape
    consts = [w_o, ln_g, ln_b, *ffn]
    return pl.pallas_call(
        _attn_out_ffn_kernel,
        grid=(t // tm,),
        in_specs=[pl.BlockSpec((c, tm), lambda i: (0, i)),
                  pl.BlockSpec((tm, c), lambda i: (i, 0))] + [_once(a.shape) for a in consts],
        out_specs=pl.BlockSpec((tm, c), lambda i: (i, 0)),
        out_shape=jax.ShapeDtypeStruct((t, c), F32),
        compiler_params=_params("parallel"),
    )(at, x, *consts)


def _qkv_kernel(x_ref, wqt_ref, wk_ref, wvt_ref, cos_t_ref, sin_t_ref, cos_k_ref, sin_k_ref,
                qt_out, k_out, vt_out, kmean_out):
    xb = x_ref[...].astype(BF16)
    tm, c = xb.shape
    nh = c // HEAD_DIM
    half = HEAD_DIM // 2

    qt = _nt_dot(wqt_ref[...], xb)
    cos_t, sin_t = cos_t_ref[...], sin_t_ref[...]
    scale = HEAD_DIM ** -0.5 * LOG2_E
    for h in range(nh):
        q1 = qt[h * HEAD_DIM:h * HEAD_DIM + half]
        q2 = qt[h * HEAD_DIM + half:(h + 1) * HEAD_DIM]
        qt_out[h, 0:half, :] = ((q1 * cos_t - q2 * sin_t) * scale).astype(BF16)
        qt_out[h, half:HEAD_DIM, :] = ((q2 * cos_t + q1 * sin_t) * scale).astype(BF16)

    vt = _nt_dot(wvt_ref[...], xb)
    for h in range(nh):
        vt_out[h] = vt[h * HEAD_DIM:(h + 1) * HEAD_DIM].astype(BF16)

    k = jnp.dot(xb, wk_ref[...], preferred_element_type=F32)
    reps = c // cos_k_ref.shape[1]
    cos_k = jnp.concatenate([cos_k_ref[...]] * reps, axis=1)
    sin_k = jnp.concatenate([sin_k_ref[...]] * reps, axis=1)
    lane = lax.broadcasted_iota(jnp.int32, k.shape, 1)
    first = (lane % HEAD_DIM) < half
    rot = jnp.where(first, pltpu.roll(k, c - half, 1), pltpu.roll(k, half, 1))
    k = k * cos_k + rot * sin_k
    for h in range(nh):
        k_out[h] = k[:, h * HEAD_DIM:(h + 1) * HEAD_DIM].astype(BF16)
    for blk in range(tm // MOBA_BLOCK):
        kmean_out[blk] = jnp.mean(k[blk * MOBA_BLOCK:(blk + 1) * MOBA_BLOCK], axis=0, keepdims=True)


def _qkv_rope(x, wq_t, wk, wv_t, cos_t, sin_t, cos_k, sin_k, tm):
    t, c = x.shape
    nh = c // HEAD_DIM
    half = HEAD_DIM // 2
    nb = t // MOBA_BLOCK
    return pl.pallas_call(
        _qkv_kernel,
        grid=(t // tm,),
        in_specs=[pl.BlockSpec((tm, c), lambda i: (i, 0)),
                  _full(wq_t.shape), _full(wk.shape), _full(wv_t.shape),
                  pl.BlockSpec((half, tm), lambda i: (0, i)),
                  pl.BlockSpec((half, tm), lambda i: (0, i)),
                  pl.BlockSpec((tm, cos_k.shape[1]), lambda i: (i, 0)),
                  pl.BlockSpec((tm, sin_k.shape[1]), lambda i: (i, 0))],
        out_specs=[pl.BlockSpec((nh, HEAD_DIM, tm), lambda i: (0, 0, i)),
                   pl.BlockSpec((nh, tm, HEAD_DIM), lambda i: (0, i, 0)),
                   pl.BlockSpec((nh, HEAD_DIM, tm), lambda i: (0, 0, i)),
                   pl.BlockSpec((tm // MOBA_BLOCK, 1, c), lambda i: (i, 0, 0))],
        out_shape=[jax.ShapeDtypeStruct((nh, HEAD_DIM, t), BF16),
                   jax.ShapeDtypeStruct((nh, t, HEAD_DIM), BF16),
                   jax.ShapeDtypeStruct((nh, HEAD_DIM, t), BF16),
                   jax.ShapeDtypeStruct((nb, 1, c), F32)],
        compiler_params=_params("parallel"),
    )(x, wq_t, wk, wv_t, cos_t, sin_t, cos_k, sin_k)


def _moba_kernel(qt_ref, k_ref, vt_ref, kmean_ref, o_ref, sel_ref, s_ref, m_ref, acc_ref):
    i = pl.program_id(1)
    hb, _, bq = qt_ref.shape
    nb = kmean_ref.shape[1]
    blk = MOBA_BLOCK

    qts = [qt_ref[h] for h in range(hb)]
    heads = range(hb)
    ones = jnp.ones((16, blk), BF16)

    def select_blocks():
        bidx = lax.broadcasted_iota(jnp.int32, (nb, bq), 0)
        gates = []
        for h in heads:
            km_hi, km_lo = _split2(kmean_ref[h])
            gate = (jnp.dot(km_hi, qts[h], preferred_element_type=F32)
                    + jnp.dot(km_lo, qts[h], preferred_element_type=F32))
            gates.append(jnp.where(bidx < i, gate, -jnp.inf))
        sels = [jnp.zeros((nb, bq), F32) for _ in heads]
        for _ in range(MOBA_TOPK):
            ms = [jnp.max(g, axis=0, keepdims=True) for g in gates]
            firsts = [jnp.min(jnp.where(jnp.logical_and(gates[h] == ms[h], ms[h] > -jnp.inf), bidx, nb),
                              axis=0, keepdims=True) for h in heads]
            picks = [bidx == f for f in firsts]
            sels = [jnp.where(picks[h], 1.0, sels[h]) for h in heads]
            gates = [jnp.where(picks[h], -jnp.inf, gates[h]) for h in heads]
        for h in heads:
            sel_ref[h] = sels[h]

    def put_scores(j, slot):
        koff = pl.multiple_of(j * blk, blk)
        for h in heads:
            s_ref[slot, h] = jnp.dot(k_ref[h, pl.ds(koff, blk), :], qts[h],
                                     preferred_element_type=F32)

    def attend(j, slot, causal):
        koff = pl.multiple_of(j * blk, blk)
        ss = [s_ref[slot, h] for h in heads]
        if causal:
            key = lax.broadcasted_iota(jnp.int32, (blk, bq), 0)
            qry = lax.broadcasted_iota(jnp.int32, (blk, bq), 1)
            ss = [jnp.where(key <= qry, s, -jnp.inf) for s in ss]
        else:
            ons = [sel_ref[h, pl.ds(j, 1), :] > 0.0 for h in heads]
        cmaxs = [jnp.max(s, axis=0, keepdims=True) for s in ss]
        ms = [m_ref[h] for h in heads]
        if causal:
            m_news = [jnp.maximum(ms[h], cmaxs[h]) for h in heads]
            m_ps = m_news
        else:
            m_news = [jnp.where(ons[h], jnp.maximum(ms[h], cmaxs[h]), ms[h]) for h in heads]
            m_ps = [jnp.where(ons[h], m_news[h], cmaxs[h]) for h in heads]
        ps = [jnp.exp2(ss[h] - m_ps[h]).astype(BF16) for h in heads]
        os = [jnp.dot(jnp.concatenate([vt_ref[h, :, pl.ds(koff, blk)], ones], axis=0), ps[h],
                      preferred_element_type=F32) for h in heads]
        if not causal:
            os = [jnp.where(ons[h], os[h], 0.0) for h in heads]
        for h in heads:
            m_ref[h] = m_news[h]
            acc_ref[h] = acc_ref[h] * jnp.exp2(ms[h] - m_news[h]) + os[h]

    m_ref[...] = jnp.full(m_ref.shape, NEG_BIG, F32)
    acc_ref[...] = jnp.zeros(acc_ref.shape, F32)

    def blocks(j0, count):
        for d in range(count):
            put_scores(j0 + d + 1, (d + 1) % 2)
            attend(j0 + d, d % 2, False)

    def finish():
        for h in heads:
            acc = acc_ref[h]
            o_ref[h] = acc[:HEAD_DIM] / acc[HEAD_DIM:HEAD_DIM + 1]

    put_scores(0, 0)
    select_blocks()
    done = 0
    for u in MOBA_BLOCKS_PER_ITER:
        trips = (i - done) // u

        @pl.loop(0, trips)
        def _(jj, u=u, done=done):
            blocks(done + u * jj, u)

        done = done + trips * u

    @pl.when(i % 2 == 0)
    def _():
        attend(i, 0, True)
        finish()

    @pl.when(i % 2 == 1)
    def _():
        put_scores(i, 1)
        attend(i - 1, 0, False)
        attend(i, 1, True)
        finish()


def _moba_attention(qt, k, vt, kmean, hb):
    nh, dh, t = qt.shape
    nb = t // MOBA_BLOCK
    once = pl.Buffered(1)
    return pl.pallas_call(
        _moba_kernel,
        grid=(nh // hb, nb),
        in_specs=[pl.BlockSpec((hb, dh, MOBA_BLOCK), lambda g, i: (g, 0, i)),
                  pl.BlockSpec((hb, t, dh), lambda g, i: (g, 0, 0), pipeline_mode=once),
                  pl.BlockSpec((hb, dh, t), lambda g, i: (g, 0, 0), pipeline_mode=once),
                  pl.BlockSpec((hb, nb, dh), lambda g, i: (g, 0, 0))],
        out_specs=pl.BlockSpec((hb, dh, MOBA_BLOCK), lambda g, i: (g, 0, i)),
        out_shape=jax.ShapeDtypeStruct((nh, dh, t), F32),
        scratch_shapes=[pltpu.VMEM((hb, nb, MOBA_BLOCK), F32),
                        pltpu.VMEM((2, hb, MOBA_BLOCK, MOBA_BLOCK), F32),
                        pltpu.VMEM((hb, 1, MOBA_BLOCK), F32),
                        pltpu.VMEM((hb, HEAD_DIM + 16, MOBA_BLOCK), F32)],
        compiler_params=_params("parallel", "arbitrary"),
    )(qt, k, vt, kmean)


def _row(a):
    return a.reshape(1, -1).astype(F32)


def _rwkv_layer(x, p, ln_g, ln_b, hs, ffn, tm):
    r, lw, k, v, kk, b, g, bonus = _rwkv_proj(x, p, hs, tm)
    t = x.shape[0]
    y = _wkv_scan(r, lw, k, v, kk, b, tb_local=min(WKV_LOCAL_TILE, t), tb_state=min(WKV_STATE_TILE, t))
    return _rwkv_out_ffn(y, g, bonus, x, p['gn_g'], p['gn_b'], p['w_o'], ln_g, ln_b, hs, ffn, tm)


def _rope_tables(t):
    inv = ROPE_THETA ** (-jnp.arange(0, HEAD_DIM, 2, dtype=F32) / HEAD_DIM)
    ang = jnp.arange(t, dtype=F32)[:, None] * inv[None, :]
    cos, sin = jnp.cos(ang), jnp.sin(ang)
    reps = LANES // HEAD_DIM
    cos_k = jnp.tile(jnp.concatenate([cos, cos], axis=1), (1, reps))
    sin_k = jnp.tile(jnp.concatenate([-sin, sin], axis=1), (1, reps))
    return cos.T, sin.T, cos_k, sin_k


def _moba_layer(x, w_qkv, w_o, ln_g, ln_b, ffn, tm):
    t, c = x.shape
    nh = c // HEAD_DIM
    wq_t = w_qkv[:, :c].T.astype(BF16)
    wk = w_qkv[:, c:2 * c].astype(BF16)
    wv_t = w_qkv[:, 2 * c:].T.astype(BF16)
    cos_t, sin_t, cos_k, sin_k = _rope_tables(t)
    qt, k, vt, kmean = _qkv_rope(x, wq_t, wk, wv_t, cos_t, sin_t, cos_k, sin_k, tm)
    kmean = jnp.transpose(kmean.reshape(t // MOBA_BLOCK, nh, HEAD_DIM), (1, 0, 2))
    at = _moba_attention(qt, k, vt, kmean, hb=min(MOBA_HEADS_PER_STEP, nh))
    return _attn_out_ffn(at.reshape(c, t), x, w_o.astype(BF16), ln_g, ln_b, ffn, tm)


def kernel(x, rwkv_mu, rwkv_w_rkv, rwkv_w0, rwkv_w1, rwkv_w2, rwkv_a0, rwkv_a1, rwkv_a2, rwkv_g1, rwkv_g2,
           rwkv_k_k, rwkv_k_a, rwkv_r_k, rwkv_gn_g, rwkv_gn_b, rwkv_w_o, moba_w_qkv, moba_w_o,
           ffn_w_in, ffn_w_down, ln_mix_g, ln_mix_b, ln_ffn_g, ln_ffn_b):
    bsz, t, c = x.shape
    assert c % WKV_GROUP == 0 and t % MOBA_BLOCK == 0
    tm = min(ROW_TILE, t)
    hs = _head_ones()
    outs = []
    for bi in range(bsz):
        h = x[bi]
        for i in range(DEPTH):
            j = i // 2
            ffn = (ffn_w_in[i].astype(BF16), ffn_w_down[i].astype(BF16), _row(ln_ffn_g[i]), _row(ln_ffn_b[i]))
            if i % 2 == 0:
                p = dict(mu=rwkv_mu[j], w_rkv=rwkv_w_rkv[j].astype(BF16), w0=_row(rwkv_w0[j]),
                         w1=rwkv_w1[j].astype(BF16), w2=rwkv_w2[j].astype(BF16), a0=_row(rwkv_a0[j]),
                         a1=rwkv_a1[j].astype(BF16), a2=rwkv_a2[j].astype(BF16),
                         g1=rwkv_g1[j].astype(BF16), g2=rwkv_g2[j].astype(BF16),
                         k_k=_row(rwkv_k_k[j]), k_a=_row(rwkv_k_a[j]), r_k=_row(rwkv_r_k[j]),
                         gn_g=_row(rwkv_gn_g[j]), gn_b=_row(rwkv_gn_b[j]), w_o=rwkv_w_o[j].astype(BF16))
                h = _rwkv_layer(h, p, _row(ln_mix_g[i]), _row(ln_mix_b[i]), hs, ffn, tm)
            else:
                h = _moba_layer(h, moba_w_qkv[j], moba_w_o[j], _row(ln_mix_g[i]), _row(ln_mix_b[i]), ffn, tm)
        outs.append(h)
    return jnp.stack(outs, axis=0)
```

```python
import jax
import jax.numpy as jnp
from jax import lax
from jax.experimental import pallas as pl
from jax.experimental.pallas import tpu as pltpu

F32 = jnp.float32
BF16 = jnp.bfloat16

HEAD_DIM = 64
DEPTH = 2
ALPHA = (2 * DEPTH) ** 0.25
LN_EPS = 1e-5
GN_EPS = HEAD_DIM * 1e-5
MOBA_BLOCK = 256
MOBA_TOPK = 3
ROPE_THETA = 10000.0
LOG2_E = 1.4426950408889634
MOBA_HEADS_PER_STEP = 4
MOBA_BLOCKS_PER_ITER = (32, 8, 2)

LANES = 128
MXU_DIM = 256
WKV_CHUNK = 64
WKV_GROUP = MXU_DIM
WKV_LOCAL_TILE = 1024
WKV_STATE_TILE = 256
ROW_TILE = 512
NEG_BIG = -1e30
VMEM_LIMIT = 56 * 1024 * 1024


def _params(*sem):
    return pltpu.CompilerParams(dimension_semantics=sem, vmem_limit_bytes=VMEM_LIMIT)


def _full(shape):
    n = len(shape)
    return pl.BlockSpec(shape, lambda *_: (0,) * n)


def _bdot(a, b):
    return jnp.dot(a.astype(BF16), b.astype(BF16), preferred_element_type=F32)


def _nt_dot(a, b):
    return lax.dot_general(a.astype(BF16), b.astype(BF16), (((1,), (1,)), ((), ())),
                           preferred_element_type=F32)


def _tn_dot(a, b):
    return lax.dot_general(a.astype(BF16), b.astype(BF16), (((0,), (0,)), ((), ())),
                           preferred_element_type=F32)


def _split2(a):
    hi = a.astype(BF16)
    lo = (a - hi.astype(F32)).astype(BF16)
    return hi, lo


def _dot_exact_rhs(a, b_exact):
    if a.dtype == BF16:
        return jnp.dot(a, b_exact, preferred_element_type=F32)
    hi, lo = _split2(a)
    return (jnp.dot(hi, b_exact, preferred_element_type=F32)
            + jnp.dot(lo, b_exact, preferred_element_type=F32))


def _head_sum(a, ones_bd):
    w = ones_bd.shape[0]
    return jnp.concatenate([_dot_exact_rhs(a[:, s:s + w], ones_bd) for s in range(0, a.shape[1], w)],
                           axis=1)


def _layer_norm(z, g, b):
    mu = jnp.mean(z, axis=-1, keepdims=True)
    d = z - mu
    var = jnp.mean(d * d, axis=-1, keepdims=True)
    return d * lax.rsqrt(var + LN_EPS) * g + b


def _head_ones():
    h = jnp.arange(MXU_DIM) // HEAD_DIM
    return (h[:, None] == h[None, :]).astype(BF16)


def _rwkv_proj_kernel(x_ref, xp_ref, mu_ref, wrkv_ref, w0_ref, w1_ref, w2_ref, a0_ref, a1_ref, a2_ref,
                      g1_ref, g2_ref, kk_w_ref, ka_w_ref, rk_w_ref, hs_ref,
                      r_out, lw_out, k_out, v_out, kk_out, b_out, g_out, bonus_out):
    i = pl.program_id(0)
    x = x_ref[...]
    prev_last = jnp.where(i > 0, xp_ref[7:8, :], 0.0)
    row = lax.broadcasted_iota(jnp.int32, x.shape, 0)
    shifted = jnp.where(row == 0, prev_last, pltpu.roll(x, 1, 0))
    xx = shifted - x
    mix = lambda n: x + xx * mu_ref[n:n + 1, :]
    hs = hs_ref[...]

    r = _bdot(mix(0), wrkv_ref[0])
    k = _bdot(mix(2), wrkv_ref[1])
    v = _bdot(mix(3), wrkv_ref[2])
    zw = w0_ref[...] + _bdot(jnp.tanh(_bdot(mix(1), w1_ref[...])), w2_ref[...])
    u = -zw
    w_log = -(jnp.maximum(u, 0.0) + jnp.log(1.0 + jnp.exp(-jnp.abs(u)))) - 0.5
    a = jax.nn.sigmoid(a0_ref[...] + _bdot(_bdot(mix(4), a1_ref[...]), a2_ref[...]))
    g = _bdot(jax.nn.sigmoid(_bdot(mix(5), g1_ref[...])), g2_ref[...])

    kk = k * kk_w_ref[...]
    norm = jnp.sqrt(_head_sum(kk * kk, hs))
    kk = kk / jnp.maximum(norm, 1e-12)
    k = k * (1.0 + (a - 1.0) * ka_w_ref[...])

    r_out[...] = r.astype(BF16)
    lw_out[...] = -jnp.exp(w_log)
    k_out[...] = k.astype(BF16)
    v_out[...] = v.astype(BF16)
    kk_out[...] = kk.astype(BF16)
    b_out[...] = (kk * a).astype(BF16)
    g_out[...] = g.astype(BF16)
    bonus_out[...] = (_head_sum(r * k * rk_w_ref[...], hs) * v).astype(BF16)


def _rwkv_proj(x, p, hs, tm):
    t, c = x.shape
    row = lambda: pl.BlockSpec((tm, c), lambda i: (i, 0))
    prev = pl.BlockSpec((8, c), lambda i: (jnp.maximum(i * (tm // 8) - 1, 0), 0))
    ins = [x, x, p['mu'], p['w_rkv'], p['w0'], p['w1'], p['w2'], p['a0'], p['a1'], p['a2'],
           p['g1'], p['g2'], p['k_k'], p['k_a'], p['r_k'], hs]
    in_specs = [row(), prev] + [_full(a.shape) for a in ins[2:]]
    f32 = jax.ShapeDtypeStruct((t, c), F32)
    bf16 = jax.ShapeDtypeStruct((t, c), BF16)
    return pl.pallas_call(
        _rwkv_proj_kernel,
        grid=(t // tm,),
        in_specs=in_specs,
        out_specs=[row() for _ in range(8)],
        out_shape=[bf16, f32, bf16, bf16, bf16, bf16, bf16, bf16],
        compiler_params=_params("parallel"),
    )(*ins)


def _bmm(a, b):
    return lax.dot_general(a.astype(BF16), b.astype(BF16), (((2,), (1,)), ((0,), (0,))),
                           preferred_element_type=F32)


def _wkv_local_kernel(r_ref, lw_ref, k_ref, v_ref, kk_ref, b_ref,
                      al_out, rt_out, u0_out, y0_out, bh_out, kh_out, pl_out):
    tb, gw = r_ref.shape
    L = WKV_CHUNK
    nc = tb // L
    nh = gw // HEAD_DIM
    n = nh * L

    lane = lax.broadcasted_iota(jnp.int32, (1, 1, gw), 2)
    head_masks = [lane // HEAD_DIM == h for h in range(nh)]
    col = lax.broadcasted_iota(jnp.int32, (1, 1, n), 2)
    col_masks = [col // L == h for h in range(nh)]
    ri = lax.broadcasted_iota(jnp.int32, (1, L, n), 1)
    ci = lax.broadcasted_iota(jnp.int32, (1, L, n), 2) % L
    strict = ci < ri
    incl = ci <= ri
    eye_cat = (ci == ri).astype(F32)
    tt = min(tb, MXU_DIM)
    rb = lax.broadcasted_iota(jnp.int32, (tt, tt), 0)
    cb = lax.broadcasted_iota(jnp.int32, (tt, tt), 1)
    tri = jnp.logical_and(cb <= rb, cb // L == rb // L).astype(BF16)

    def stack(a, masks):
        a16 = a.astype(BF16)
        return jnp.concatenate([jnp.where(m, a16, 0) for m in masks], axis=1)

    lw2 = lw_ref[...]
    hi, lo = _split2(lw2)
    cum2 = jnp.concatenate(
        [jnp.dot(tri, hi[s:s + tt], preferred_element_type=F32)
         + jnp.dot(tri, lo[s:s + tt], preferred_element_type=F32) for s in range(0, tb, tt)], axis=0)
    to3 = lambda a: a.reshape(nc, L, gw)
    cum, lw = to3(cum2), to3(lw2)
    r, k, v, kk, b = (to3(ref[...]) for ref in (r_ref, k_ref, v_ref, kk_ref, b_ref))
    cum_l = cum[:, L - 1:L, :]
    p_inv = jnp.exp(-cum)
    al = -(kk * jnp.exp(cum - lw))
    bt = b * p_inv
    kt = k * p_inv
    rt = r * jnp.exp(cum)
    dec = jnp.exp(cum_l - cum)

    xs = jnp.concatenate([al, rt], axis=1)
    ys = jnp.concatenate([stack(bt, head_masks), stack(kt, head_masks)], axis=1)
    res = lax.dot_general(xs.astype(BF16), ys.astype(BF16), (((2,), (2,)), ((0,), (0,))),
                          preferred_element_type=F32)
    a_ab = jnp.where(strict, res[:, :L, :n], 0.0)
    a_ak = jnp.where(strict, res[:, :L, n:], 0.0)
    a_rb = jnp.where(incl, res[:, L:, :n], 0.0)
    a_rk = jnp.where(incl, res[:, L:, n:], 0.0)

    tinv = eye_cat + jnp.where(jnp.logical_and(ri // 2 == ci // 2, ci < ri), a_ab, 0.0)
    s = 4
    while s <= L:
        quad = jnp.logical_and(ri // s == ci // s, jnp.logical_and(ri % s >= s // 2, ci % s < s // 2))
        x = _bmm(jnp.where(quad, a_ab, 0.0), stack(tinv, col_masks))
        tinv = tinv + _bmm(tinv, stack(x, col_masks))
        s *= 2

    v_st = stack(v, head_masks)
    al_p = _bmm(tinv, stack(al, head_masks))
    u0 = _bmm(tinv, stack(_bmm(a_ak, v_st), head_masks))
    rt_p = rt + _bmm(a_rb, stack(al_p, head_masks))
    y0 = _bmm(a_rb, stack(u0, head_masks)) + _bmm(a_rk, v_st)

    to2 = lambda a: a.reshape(tb, gw)
    al_out[...] = to2(al_p).astype(BF16)
    rt_out[...] = to2(rt_p).astype(BF16)
    u0_out[...] = to2(u0).astype(BF16)
    y0_out[...] = to2(y0).astype(BF16)
    bh_out[...] = to2(b * dec).astype(BF16)
    kh_out[...] = to2(k * dec).astype(BF16)
    pl_out[...] = jnp.exp(cum_l)


def _wkv_local(r, lw, k, v, kk, b, tb):
    t, c = r.shape
    gw = WKV_GROUP
    nc = tb // WKV_CHUNK
    spec = lambda: pl.BlockSpec((tb, gw), lambda i, g: (i, g))
    f32 = jax.ShapeDtypeStruct((t, c), F32)
    bf16 = jax.ShapeDtypeStruct((t, c), BF16)
    return pl.pallas_call(
        _wkv_local_kernel,
        grid=(t // tb, c // gw),
        in_specs=[spec() for _ in range(6)],
        out_specs=[spec() for _ in range(6)] + [pl.BlockSpec((nc, 1, gw), lambda i, g: (i, 0, g))],
        out_shape=[bf16, bf16, bf16, bf16, bf16, bf16,
                   jax.ShapeDtypeStruct((t // WKV_CHUNK, 1, c), F32)],
        compiler_params=_params("parallel", "parallel"),
    )(r, lw, k, v, kk, b)


def _wkv_state_kernel(al_ref, rt_ref, u0_ref, y0_ref, bh_ref, kh_ref, v_ref, pl_ref, y_ref, sv_ref):
    tb, c = v_ref.shape
    L = WKV_CHUNK
    gw = WKV_GROUP
    groups = range(c // gw)

    @pl.when(pl.program_id(0) == 0)
    def _():
        sv_ref[...] = jnp.zeros_like(sv_ref)

    bd_mask = (lax.broadcasted_iota(jnp.int32, (gw, gw), 0) // HEAD_DIM
               == lax.broadcasted_iota(jnp.int32, (gw, gw), 1) // HEAD_DIM)

    def chunk(ci, carry):
        rows = pl.ds(pl.multiple_of(ci * L, L), L)
        lanes = [slice(g * gw, (g + 1) * gw) for g in groups]
        svs = [sv_ref[g] for g in groups]
        uys = [_nt_dot(jnp.concatenate([al_ref[rows, lanes[g]], rt_ref[rows, lanes[g]]], axis=0), svs[g])
               for g in groups]
        us = [uys[g][:L] + u0_ref[rows, lanes[g]] for g in groups]
        for g in groups:
            y_ref[rows, lanes[g]] = (uys[g][L:] + y0_ref[rows, lanes[g]]).astype(BF16)
        upds = [_tn_dot(jnp.concatenate([us[g].astype(BF16), v_ref[rows, lanes[g]].astype(BF16)], axis=0),
                        jnp.concatenate([bh_ref[rows, lanes[g]], kh_ref[rows, lanes[g]]], axis=0))
                for g in groups]
        for g in groups:
            sv_ref[g] = svs[g] * pl_ref[ci, :, lanes[g]] + jnp.where(bd_mask, upds[g], 0.0)
        return carry

    lax.fori_loop(0, tb // L, chunk, 0)


def _wkv_state(al_p, rt_p, u0, y0, bh, kh, v, p_l, tb):
    t, c = v.shape
    nc = tb // WKV_CHUNK
    row = lambda: pl.BlockSpec((tb, c), lambda i: (i, 0))
    return pl.pallas_call(
        _wkv_state_kernel,
        grid=(t // tb,),
        in_specs=[row() for _ in range(7)] + [pl.BlockSpec((nc, 1, c), lambda i: (i, 0, 0))],
        out_specs=row(),
        out_shape=jax.ShapeDtypeStruct((t, c), BF16),
        scratch_shapes=[pltpu.VMEM((c // WKV_GROUP, WKV_GROUP, WKV_GROUP), F32)],
        compiler_params=_params("arbitrary"),
    )(al_p, rt_p, u0, y0, bh, kh, v, p_l)


def _wkv_scan(r, lw, k, v, kk, b, tb_local, tb_state):
    al_p, rt_p, u0, y0, bh, kh, p_l = _wkv_local(r, lw, k, v, kk, b, tb_local)
    return _wkv_state(al_p, rt_p, u0, y0, bh, kh, v, p_l, tb_state)


def _once(shape):
    n = len(shape)
    return pl.BlockSpec(shape, lambda *_: (0,) * n, pipeline_mode=pl.Buffered(1))


def _ffn_bounds(f):
    mid = min(f, -(-(f // MXU_DIM) // 2) * MXU_DIM) if f % MXU_DIM == 0 else f
    return tuple(b for b in ((0, mid), (mid, f)) if b[1] > b[0])


def _swiglu_ln(x, win_ref, wdown_ref, lng_ref, lnb_ref):
    xb = x.astype(BF16)
    f = wdown_ref.shape[0]
    acc = jnp.zeros(x.shape, F32)
    for lo, hi in _ffn_bounds(f):
        gate = jnp.dot(xb, win_ref[:, lo:hi], preferred_element_type=F32)
        up = jnp.dot(xb, win_ref[:, f + lo:f + hi], preferred_element_type=F32)
        act = (gate * jax.nn.sigmoid(gate) * up).astype(BF16)
        acc = acc + jnp.dot(act, wdown_ref[lo:hi, :], preferred_element_type=F32)
    return _layer_norm(ALPHA * x + acc, lng_ref[...], lnb_ref[...])


def _rwkv_out_ffn_kernel(y_ref, g_ref, bonus_ref, x_ref, gng_ref, gnb_ref, wo_ref, lng_ref, lnb_ref, hs_ref,
                         win_ref, wdown_ref, lng2_ref, lnb2_ref, o_ref):
    hs = hs_ref[...]
    y = y_ref[...]
    mean = _head_sum(y, hs) * (1.0 / HEAD_DIM)
    d = y - mean
    var = _head_sum(d * d, hs) * (1.0 / HEAD_DIM)
    yn = d * lax.rsqrt(var + GN_EPS) * gng_ref[...] + gnb_ref[...]
    z = (yn + bonus_ref[...]) * g_ref[...]
    h = _bdot(z, wo_ref[...])
    x1 = _layer_norm(ALPHA * x_ref[...] + h, lng_ref[...], lnb_ref[...])
    o_ref[...] = _swiglu_ln(x1, win_ref, wdown_ref, lng2_ref, lnb2_ref)


def _rwkv_out_ffn(y, g, bonus, x, gn_g, gn_b, w_o, ln_g, ln_b, hs, ffn, tm):
    t, c = x.shape
    row = lambda: pl.BlockSpec((tm, c), lambda i: (i, 0))
    consts = [gn_g, gn_b, w_o, ln_g, ln_b, hs, *ffn]
    return pl.pallas_call(
        _rwkv_out_ffn_kernel,
        grid=(t // tm,),
        in_specs=[row() for _ in range(4)] + [_once(a.shape) for a in consts],
        out_specs=row(),
        out_shape=jax.ShapeDtypeStruct((t, c), F32),
        compiler_params=_params("parallel"),
    )(y, g, bonus, x, *consts)


def _attn_out_ffn_kernel(at_ref, x_ref, wo_ref, lng_ref, lnb_ref, win_ref, wdown_ref, lng2_ref, lnb2_ref,
                         o_ref):
    h = _tn_dot(at_ref[...], wo_ref[...])
    x1 = _layer_norm(ALPHA * x_ref[...] + h, lng_ref[...], lnb_ref[...])
    o_ref[...] = _swiglu_ln(x1, win_ref, wdown_ref, lng2_ref, lnb2_ref)


def _attn_out_ffn(at, x, w_o, ln_g, ln_b, ffn, tm):
    t, c = x.shape
    consts = [w_o, ln_g, ln_b, *ffn]
    return pl.pallas_call(
        _attn_out_ffn_kernel,
        grid=(t // tm,),
        in_specs=[pl.BlockSpec((c, tm), lambda i: (0, i)),
                  pl.BlockSpec((tm, c), lambda i: (i, 0))] + [_once(a.shape) for a in consts],
        out_specs=pl.BlockSpec((tm, c), lambda i: (i, 0)),
        out_shape=jax.ShapeDtypeStruct((t, c), F32),
        compiler_params=_params("parallel"),
    )(at, x, *consts)


def _qkv_kernel(x_ref, wqt_ref, wk_ref, wvt_ref, cos_t_ref, sin_t_ref, cos_k_ref, sin_k_ref,
                qt_out, k_out, vt_out, kmean_out):
    xb = x_ref[...].astype(BF16)
    tm, c = xb.shape
    nh = c // HEAD_DIM
    half = HEAD_DIM // 2

    qt = _nt_dot(wqt_ref[...], xb)
    cos_t, sin_t = cos_t_ref[...], sin_t_ref[...]
    scale = HEAD_DIM ** -0.5 * LOG2_E
    for h in range(nh):
        q1 = qt[h * HEAD_DIM:h * HEAD_DIM + half]
        q2 = qt[h * HEAD_DIM + half:(h + 1) * HEAD_DIM]
        qt_out[h, 0:half, :] = ((q1 * cos_t - q2 * sin_t) * scale).astype(BF16)
        qt_out[h, half:HEAD_DIM, :] = ((q2 * cos_t + q1 * sin_t) * scale).astype(BF16)

    vt = _nt_dot(wvt_ref[...], xb)
    for h in range(nh):
        vt_out[h] = vt[h * HEAD_DIM:(h + 1) * HEAD_DIM].astype(BF16)

    k = jnp.dot(xb, wk_ref[...], preferred_element_type=F32)
    reps = c // cos_k_ref.shape[1]
    cos_k = jnp.concatenate([cos_k_ref[...]] * reps, axis=1)
    sin_k = jnp.concatenate([sin_k_ref[...]] * reps, axis=1)
    lane = lax.broadcasted_iota(jnp.int32, k.shape, 1)
    first = (lane % HEAD_DIM) < half
    rot = jnp.where(first, pltpu.roll(k, c - half, 1), pltpu.roll(k, half, 1))
    k = k * cos_k + rot * sin_k
    for h in range(nh):
        k_out[h] = k[:, h * HEAD_DIM:(h + 1) * HEAD_DIM].astype(BF16)
    for blk in range(tm // MOBA_BLOCK):
        kmean_out[blk] = jnp.mean(k[blk * MOBA_BLOCK:(blk + 1) * MOBA_BLOCK], axis=0, keepdims=True)


def _qkv_rope(x, wq_t, wk, wv_t, cos_t, sin_t, cos_k, sin_k, tm):
    t, c = x.shape
    nh = c // HEAD_DIM
    half = HEAD_DIM // 2
    nb = t // MOBA_BLOCK
    return pl.pallas_call(
        _qkv_kernel,
        grid=(t // tm,),
        in_specs=[pl.BlockSpec((tm, c), lambda i: (i, 0)),
                  _full(wq_t.shape), _full(wk.shape), _full(wv_t.shape),
                  pl.BlockSpec((half, tm), lambda i: (0, i)),
                  pl.BlockSpec((half, tm), lambda i: (0, i)),
                  pl.BlockSpec((tm, cos_k.shape[1]), lambda i: (i, 0)),
                  pl.BlockSpec((tm, sin_k.shape[1]), lambda i: (i, 0))],
        out_specs=[pl.BlockSpec((nh, HEAD_DIM, tm), lambda i: (0, 0, i)),
                   pl.BlockSpec((nh, tm, HEAD_DIM), lambda i: (0, i, 0)),
                   pl.BlockSpec((nh, HEAD_DIM, tm), lambda i: (0, 0, i)),
                   pl.BlockSpec((tm // MOBA_BLOCK, 1, c), lambda i: (i, 0, 0))],
        out_shape=[jax.ShapeDtypeStruct((nh, HEAD_DIM, t), BF16),
                   jax.ShapeDtypeStruct((nh, t, HEAD_DIM), BF16),
                   jax.ShapeDtypeStruct((nh, HEAD_DIM, t), BF16),
                   jax.ShapeDtypeStruct((nb, 1, c), F32)],
        compiler_params=_params("parallel"),
    )(x, wq_t, wk, wv_t, cos_t, sin_t, cos_k, sin_k)


def _moba_kernel(qt_ref, k_ref, vt_ref, kmean_ref, o_ref, sel_ref, s_ref):
    i = pl.program_id(1)
    hb, _, bq = qt_ref.shape
    nb = kmean_ref.shape[1]
    blk = MOBA_BLOCK

    qts = [qt_ref[h] for h in range(hb)]
    heads = range(hb)
    ones = jnp.ones((16, blk), BF16)

    def select_blocks():
        bidx = lax.broadcasted_iota(jnp.int32, (nb, bq), 0)
        gates = []
        for h in heads:
            km_hi, km_lo = _split2(kmean_ref[h])
            gate = (jnp.dot(km_hi, qts[h], preferred_element_type=F32)
                    + jnp.dot(km_lo, qts[h], preferred_element_type=F32))
            gates.append(jnp.where(bidx < i, gate, -jnp.inf))
        sels = [jnp.zeros((nb, bq), F32) for _ in heads]
        for _ in range(MOBA_TOPK):
            ms = [jnp.max(g, axis=0, keepdims=True) for g in gates]
            firsts = [jnp.min(jnp.where(jnp.logical_and(gates[h] == ms[h], ms[h] > -jnp.inf), bidx, nb),
                              axis=0, keepdims=True) for h in heads]
            picks = [bidx == f for f in firsts]
            sels = [jnp.where(picks[h], 1.0, sels[h]) for h in heads]
            gates = [jnp.where(picks[h], -jnp.inf, gates[h]) for h in heads]
        for h in heads:
            sel_ref[h] = sels[h]

    def put_scores(j, slot):
        koff = pl.multiple_of(j * blk, blk)
        for h in heads:
            s_ref[slot, h] = jnp.dot(k_ref[h, pl.ds(koff, blk), :], qts[h],
                                     preferred_element_type=F32)

    def attend(j, slot, carry, causal):
        koff = pl.multiple_of(j * blk, blk)
        ss = [s_ref[slot, h] for h in heads]
        if causal:
            key = lax.broadcasted_iota(jnp.int32, (blk, bq), 0)
            qry = lax.broadcasted_iota(jnp.int32, (blk, bq), 1)
            ss = [jnp.where(key <= qry, s, -jnp.inf) for s in ss]
        else:
            ons = [sel_ref[h, pl.ds(j, 1), :] > 0.0 for h in heads]
        cmaxs = [jnp.max(s, axis=0, keepdims=True) for s in ss]
        ms = [carry[h][0] for h in heads]
        if causal:
            m_news = [jnp.maximum(ms[h], cmaxs[h]) for h in heads]
            m_ps = m_news
        else:
            m_news = [jnp.where(ons[h], jnp.maximum(ms[h], cmaxs[h]), ms[h]) for h in heads]
            m_ps = [jnp.where(ons[h], m_news[h], cmaxs[h]) for h in heads]
        ps = [jnp.exp2(ss[h] - m_ps[h]).astype(BF16) for h in heads]
        os = [jnp.dot(jnp.concatenate([vt_ref[h, :, pl.ds(koff, blk)], ones], axis=0), ps[h],
                      preferred_element_type=F32) for h in heads]
        if not causal:
            os = [jnp.where(ons[h], os[h], 0.0) for h in heads]
        return tuple((m_news[h], carry[h][1] * jnp.exp2(ms[h] - m_news[h]) + os[h]) for h in heads)

    init = tuple((jnp.full((1, bq), NEG_BIG, F32), jnp.zeros((HEAD_DIM + 16, bq), F32)) for _ in heads)

    def blocks(j0, count, carry):
        for d in range(count):
            put_scores(j0 + d + 1, (d + 1) % 2)
            carry = attend(j0 + d, d % 2, carry, False)
        return carry

    def finish(carry):
        for h in heads:
            acc = carry[h][1]
            o_ref[h] = acc[:HEAD_DIM] / acc[HEAD_DIM:HEAD_DIM + 1]

    put_scores(0, 0)
    select_blocks()
    carry, done = init, 0
    for u in MOBA_BLOCKS_PER_ITER:
        trips = (i - done) // u
        carry = lax.fori_loop(0, trips, lambda jj, c, u=u, done=done: blocks(done + u * jj, u, c), carry)
        done = done + trips * u

    @pl.when(i % 2 == 0)
    def _():
        finish(attend(i, 0, carry, True))

    @pl.when(i % 2 == 1)
    def _():
        put_scores(i, 1)
        finish(attend(i, 1, attend(i - 1, 0, carry, False), True))


def _moba_attention(qt, k, vt, kmean, hb):
    nh, dh, t = qt.shape
    nb = t // MOBA_BLOCK
    once = pl.Buffered(1)
    return pl.pallas_call(
        _moba_kernel,
        grid=(nh // hb, nb),
        in_specs=[pl.BlockSpec((hb, dh, MOBA_BLOCK), lambda g, i: (g, 0, i)),
                  pl.BlockSpec((hb, t, dh), lambda g, i: (g, 0, 0), pipeline_mode=once),
                  pl.BlockSpec((hb, dh, t), lambda g, i: (g, 0, 0), pipeline_mode=once),
                  pl.BlockSpec((hb, nb, dh), lambda g, i: (g, 0, 0))],
        out_specs=pl.BlockSpec((hb, dh, MOBA_BLOCK), lambda g, i: (g, 0, i)),
        out_shape=jax.ShapeDtypeStruct((nh, dh, t), F32),
        scratch_shapes=[pltpu.VMEM((hb, nb, MOBA_BLOCK), F32),
                        pltpu.VMEM((2, hb, MOBA_BLOCK, MOBA_BLOCK), F32)],
        compiler_params=_params("parallel", "arbitrary"),
    )(qt, k, vt, kmean)


def _row(a):
    return a.reshape(1, -1).astype(F32)


def _rwkv_layer(x, p, ln_g, ln_b, hs, ffn, tm):
    r, lw, k, v, kk, b, g, bonus = _rwkv_proj(x, p, hs, tm)
    t = x.shape[0]
    y = _wkv_scan(r, lw, k, v, kk, b, tb_local=min(WKV_LOCAL_TILE, t), tb_state=min(WKV_STATE_TILE, t))
    return _rwkv_out_ffn(y, g, bonus, x, p['gn_g'], p['gn_b'], p['w_o'], ln_g, ln_b, hs, ffn, tm)


def _rope_tables(t):
    inv = ROPE_THETA ** (-jnp.arange(0, HEAD_DIM, 2, dtype=F32) / HEAD_DIM)
    ang = jnp.arange(t, dtype=F32)[:, None] * inv[None, :]
    cos, sin = jnp.cos(ang), jnp.sin(ang)
    reps = LANES // HEAD_DIM
    cos_k = jnp.tile(jnp.concatenate([cos, cos], axis=1), (1, reps))
    sin_k = jnp.tile(jnp.concatenate([-sin, sin], axis=1), (1, reps))
    return cos.T, sin.T, cos_k, sin_k


def _moba_layer(x, w_qkv, w_o, ln_g, ln_b, ffn, tm):
    t, c = x.shape
    nh = c // HEAD_DIM
    wq_t = w_qkv[:, :c].T.astype(BF16)
    wk = w_qkv[:, c:2 * c].astype(BF16)
    wv_t = w_qkv[:, 2 * c:].T.astype(BF16)
    cos_t, sin_t, cos_k, sin_k = _rope_tables(t)
    qt, k, vt, kmean = _qkv_rope(x, wq_t, wk, wv_t, cos_t, sin_t, cos_k, sin_k, tm)
    kmean = jnp.transpose(kmean.reshape(t // MOBA_BLOCK, nh, HEAD_DIM), (1, 0, 2))
    at = _moba_attention(qt, k, vt, kmean, hb=min(MOBA_HEADS_PER_STEP, nh))
    return _attn_out_ffn(at.reshape(c, t), x, w_o.astype(BF16), ln_g, ln_b, ffn, tm)


def kernel(x, rwkv_mu, rwkv_w_rkv, rwkv_w0, rwkv_w1, rwkv_w2, rwkv_a0, rwkv_a1, rwkv_a2, rwkv_g1, rwkv_g2,
           rwkv_k_k, rwkv_k_a, rwkv_r_k, rwkv_gn_g, rwkv_gn_b, rwkv_w_o, moba_w_qkv, moba_w_o,
           ffn_w_in, ffn_w_down, ln_mix_g, ln_mix_b, ln_ffn_g, ln_ffn_b):
    bsz, t, c = x.shape
    assert c % WKV_GROUP == 0 and t % MOBA_BLOCK == 0
    tm = min(ROW_TILE, t)
    hs = _head_ones()
    outs = []
    for bi in range(bsz):
        h = x[bi]
        for i in range(DEPTH):
            j = i // 2
            ffn = (ffn_w_in[i].astype(BF16), ffn_w_down[i].astype(BF16), _row(ln_ffn_g[i]), _row(ln_ffn_b[i]))
            if i % 2 == 0:
                p = dict(mu=rwkv_mu[j], w_rkv=rwkv_w_rkv[j].astype(BF16), w0=_row(rwkv_w0[j]),
                         w1=rwkv_w1[j].astype(BF16), w2=rwkv_w2[j].astype(BF16), a0=_row(rwkv_a0[j]),
                         a1=rwkv_a1[j].astype(BF16), a2=rwkv_a2[j].astype(BF16),
                         g1=rwkv_g1[j].astype(BF16), g2=rwkv_g2[j].astype(BF16),
                         k_k=_row(rwkv_k_k[j]), k_a=_row(rwkv_k_a[j]), r_k=_row(rwkv_r_k[j]),
                         gn_g=_row(rwkv_gn_g[j]), gn_b=_row(rwkv_gn_b[j]), w_o=rwkv_w_o[j].astype(BF16))
                h = _rwkv_layer(h, p, _row(ln_mix_g[i]), _row(ln_mix_b[i]), hs, ffn, tm)
            else:
                h = _moba_layer(h, moba_w_qkv[j], moba_w_o[j], _row(ln_mix_g[i]), _row(ln_mix_b[i]), ffn, tm)
        outs.append(h)
    return jnp.stack(outs, axis=0)
```

```python
import jax
import jax.numpy as jnp
from jax import lax
from jax.experimental import pallas as pl
from jax.experimental.pallas import tpu as pltpu

F32 = jnp.float32
BF16 = jnp.bfloat16

HEAD_DIM = 64
DEPTH = 2
ALPHA = (2 * DEPTH) ** 0.25
LN_EPS = 1e-5
GN_EPS = HEAD_DIM * 1e-5
MOBA_BLOCK = 256
MOBA_TOPK = 3
ROPE_THETA = 10000.0
LOG2_E = 1.4426950408889634
MOBA_HEADS_PER_STEP = 4
MOBA_BLOCKS_PER_ITER = (16, 4, 2)

LANES = 128
MXU_DIM = 256
WKV_CHUNK = 64
WKV_GROUP = MXU_DIM
WKV_LOCAL_TILE = 1024
WKV_STATE_TILE = 256
ROW_TILE = 512
NEG_BIG = -1e30
VMEM_LIMIT = 56 * 1024 * 1024


def _params(*sem):
    return pltpu.CompilerParams(dimension_semantics=sem, vmem_limit_bytes=VMEM_LIMIT)


def _full(shape):
    n = len(shape)
    return pl.BlockSpec(shape, lambda *_: (0,) * n)


def _bdot(a, b):
    return jnp.dot(a.astype(BF16), b.astype(BF16), preferred_element_type=F32)


def _nt_dot(a, b):
    return lax.dot_general(a.astype(BF16), b.astype(BF16), (((1,), (1,)), ((), ())),
                           preferred_element_type=F32)


def _tn_dot(a, b):
    return lax.dot_general(a.astype(BF16), b.astype(BF16), (((0,), (0,)), ((), ())),
                           preferred_element_type=F32)


def _split2(a):
    hi = a.astype(BF16)
    lo = (a - hi.astype(F32)).astype(BF16)
    return hi, lo


def _dot_exact_rhs(a, b_exact):
    if a.dtype == BF16:
        return jnp.dot(a, b_exact, preferred_element_type=F32)
    hi, lo = _split2(a)
    return (jnp.dot(hi, b_exact, preferred_element_type=F32)
            + jnp.dot(lo, b_exact, preferred_element_type=F32))


def _head_sum(a, ones_bd):
    w = ones_bd.shape[0]
    return jnp.concatenate([_dot_exact_rhs(a[:, s:s + w], ones_bd) for s in range(0, a.shape[1], w)],
                           axis=1)


def _layer_norm(z, g, b):
    mu = jnp.mean(z, axis=-1, keepdims=True)
    d = z - mu
    var = jnp.mean(d * d, axis=-1, keepdims=True)
    return d * lax.rsqrt(var + LN_EPS) * g + b


def _head_ones():
    h = jnp.arange(MXU_DIM) // HEAD_DIM
    return (h[:, None] == h[None, :]).astype(BF16)


def _rwkv_proj_kernel(x_ref, xp_ref, mu_ref, wrkv_ref, w0_ref, w1_ref, w2_ref, a0_ref, a1_ref, a2_ref,
                      g1_ref, g2_ref, kk_w_ref, ka_w_ref, rk_w_ref, hs_ref,
                      r_out, lw_out, k_out, v_out, kk_out, b_out, g_out, bonus_out):
    i = pl.program_id(0)
    x = x_ref[...]
    prev_last = jnp.where(i > 0, xp_ref[7:8, :], 0.0)
    row = lax.broadcasted_iota(jnp.int32, x.shape, 0)
    shifted = jnp.where(row == 0, prev_last, pltpu.roll(x, 1, 0))
    xx = shifted - x
    mix = lambda n: x + xx * mu_ref[n:n + 1, :]
    hs = hs_ref[...]

    r = _bdot(mix(0), wrkv_ref[0])
    k = _bdot(mix(2), wrkv_ref[1])
    v = _bdot(mix(3), wrkv_ref[2])
    zw = w0_ref[...] + _bdot(jnp.tanh(_bdot(mix(1), w1_ref[...])), w2_ref[...])
    u = -zw
    w_log = -(jnp.maximum(u, 0.0) + jnp.log(1.0 + jnp.exp(-jnp.abs(u)))) - 0.5
    a = jax.nn.sigmoid(a0_ref[...] + _bdot(_bdot(mix(4), a1_ref[...]), a2_ref[...]))
    g = _bdot(jax.nn.sigmoid(_bdot(mix(5), g1_ref[...])), g2_ref[...])

    kk = k * kk_w_ref[...]
    norm = jnp.sqrt(_head_sum(kk * kk, hs))
    kk = kk / jnp.maximum(norm, 1e-12)
    k = k * (1.0 + (a - 1.0) * ka_w_ref[...])

    r_out[...] = r.astype(BF16)
    lw_out[...] = -jnp.exp(w_log)
    k_out[...] = k.astype(BF16)
    v_out[...] = v.astype(BF16)
    kk_out[...] = kk.astype(BF16)
    b_out[...] = (kk * a).astype(BF16)
    g_out[...] = g.astype(BF16)
    bonus_out[...] = (_head_sum(r * k * rk_w_ref[...], hs) * v).astype(BF16)


def _rwkv_proj(x, p, hs, tm):
    t, c = x.shape
    row = lambda: pl.BlockSpec((tm, c), lambda i: (i, 0))
    prev = pl.BlockSpec((8, c), lambda i: (jnp.maximum(i * (tm // 8) - 1, 0), 0))
    ins = [x, x, p['mu'], p['w_rkv'], p['w0'], p['w1'], p['w2'], p['a0'], p['a1'], p['a2'],
           p['g1'], p['g2'], p['k_k'], p['k_a'], p['r_k'], hs]
    in_specs = [row(), prev] + [_full(a.shape) for a in ins[2:]]
    f32 = jax.ShapeDtypeStruct((t, c), F32)
    bf16 = jax.ShapeDtypeStruct((t, c), BF16)
    return pl.pallas_call(
        _rwkv_proj_kernel,
        grid=(t // tm,),
        in_specs=in_specs,
        out_specs=[row() for _ in range(8)],
        out_shape=[bf16, f32, bf16, bf16, bf16, bf16, bf16, bf16],
        compiler_params=_params("parallel"),
    )(*ins)


def _bmm(a, b):
    return lax.dot_general(a.astype(BF16), b.astype(BF16), (((2,), (1,)), ((0,), (0,))),
                           preferred_element_type=F32)


def _wkv_local_kernel(r_ref, lw_ref, k_ref, v_ref, kk_ref, b_ref,
                      al_out, rt_out, u0_out, y0_out, bh_out, kh_out, pl_out):
    tb, gw = r_ref.shape
    L = WKV_CHUNK
    nc = tb // L
    nh = gw // HEAD_DIM
    n = nh * L

    lane = lax.broadcasted_iota(jnp.int32, (1, 1, gw), 2)
    head_masks = [lane // HEAD_DIM == h for h in range(nh)]
    col = lax.broadcasted_iota(jnp.int32, (1, 1, n), 2)
    col_masks = [col // L == h for h in range(nh)]
    ri = lax.broadcasted_iota(jnp.int32, (1, L, n), 1)
    ci = lax.broadcasted_iota(jnp.int32, (1, L, n), 2) % L
    strict = ci < ri
    incl = ci <= ri
    eye_cat = (ci == ri).astype(F32)
    tt = min(tb, MXU_DIM)
    rb = lax.broadcasted_iota(jnp.int32, (tt, tt), 0)
    cb = lax.broadcasted_iota(jnp.int32, (tt, tt), 1)
    tri = jnp.logical_and(cb <= rb, cb // L == rb // L).astype(BF16)

    def stack(a, masks):
        a16 = a.astype(BF16)
        return jnp.concatenate([jnp.where(m, a16, 0) for m in masks], axis=1)

    lw2 = lw_ref[...]
    hi, lo = _split2(lw2)
    cum2 = jnp.concatenate(
        [jnp.dot(tri, hi[s:s + tt], preferred_element_type=F32)
         + jnp.dot(tri, lo[s:s + tt], preferred_element_type=F32) for s in range(0, tb, tt)], axis=0)
    to3 = lambda a: a.reshape(nc, L, gw)
    cum, lw = to3(cum2), to3(lw2)
    r, k, v, kk, b = (to3(ref[...]) for ref in (r_ref, k_ref, v_ref, kk_ref, b_ref))
    cum_l = cum[:, L - 1:L, :]
    p_inv = jnp.exp(-cum)
    al = -(kk * jnp.exp(cum - lw))
    bt = b * p_inv
    kt = k * p_inv
    rt = r * jnp.exp(cum)
    dec = jnp.exp(cum_l - cum)

    xs = jnp.concatenate([al, rt], axis=1)
    ys = jnp.concatenate([stack(bt, head_masks), stack(kt, head_masks)], axis=1)
    res = lax.dot_general(xs.astype(BF16), ys.astype(BF16), (((2,), (2,)), ((0,), (0,))),
                          preferred_element_type=F32)
    a_ab = jnp.where(strict, res[:, :L, :n], 0.0)
    a_ak = jnp.where(strict, res[:, :L, n:], 0.0)
    a_rb = jnp.where(incl, res[:, L:, :n], 0.0)
    a_rk = jnp.where(incl, res[:, L:, n:], 0.0)

    tinv = eye_cat + jnp.where(jnp.logical_and(ri // 2 == ci // 2, ci < ri), a_ab, 0.0)
    s = 4
    while s <= L:
        quad = jnp.logical_and(ri // s == ci // s, jnp.logical_and(ri % s >= s // 2, ci % s < s // 2))
        x = _bmm(jnp.where(quad, a_ab, 0.0), stack(tinv, col_masks))
        tinv = tinv + _bmm(tinv, stack(x, col_masks))
        s *= 2

    av = _bmm(jnp.concatenate([a_ak, a_rk], axis=1), stack(v, head_masks))
    tg = jnp.concatenate([tinv, _bmm(a_rb, stack(tinv, col_masks))], axis=1)
    on_al = _bmm(tg, stack(al, head_masks))
    on_av = _bmm(tg, stack(av[:, :L], head_masks))
    al_p, rt_p = on_al[:, :L], rt + on_al[:, L:]
    u0, y0 = on_av[:, :L], on_av[:, L:] + av[:, L:]

    to2 = lambda a: a.reshape(tb, gw)
    al_out[...] = to2(al_p).astype(BF16)
    rt_out[...] = to2(rt_p).astype(BF16)
    u0_out[...] = to2(u0).astype(BF16)
    y0_out[...] = to2(y0).astype(BF16)
    bh_out[...] = to2(b * dec).astype(BF16)
    kh_out[...] = to2(k * dec).astype(BF16)
    pl_out[...] = jnp.exp(cum_l)


def _wkv_local(r, lw, k, v, kk, b, tb):
    t, c = r.shape
    gw = WKV_GROUP
    nc = tb // WKV_CHUNK
    spec = lambda: pl.BlockSpec((tb, gw), lambda i, g: (i, g))
    f32 = jax.ShapeDtypeStruct((t, c), F32)
    bf16 = jax.ShapeDtypeStruct((t, c), BF16)
    return pl.pallas_call(
        _wkv_local_kernel,
        grid=(t // tb, c // gw),
        in_specs=[spec() for _ in range(6)],
        out_specs=[spec() for _ in range(6)] + [pl.BlockSpec((nc, 1, gw), lambda i, g: (i, 0, g))],
        out_shape=[bf16, bf16, bf16, bf16, bf16, bf16,
                   jax.ShapeDtypeStruct((t // WKV_CHUNK, 1, c), F32)],
        compiler_params=_params("parallel", "parallel"),
    )(r, lw, k, v, kk, b)


def _wkv_state_kernel(al_ref, rt_ref, u0_ref, y0_ref, bh_ref, kh_ref, v_ref, pl_ref, y_ref, sv_ref):
    tb, c = v_ref.shape
    L = WKV_CHUNK
    gw = WKV_GROUP
    groups = range(c // gw)

    @pl.when(pl.program_id(0) == 0)
    def _():
        sv_ref[...] = jnp.zeros_like(sv_ref)

    bd_mask = (lax.broadcasted_iota(jnp.int32, (gw, gw), 0) // HEAD_DIM
               == lax.broadcasted_iota(jnp.int32, (gw, gw), 1) // HEAD_DIM)

    def chunk(ci, carry):
        rows = pl.ds(pl.multiple_of(ci * L, L), L)
        lanes = [slice(g * gw, (g + 1) * gw) for g in groups]
        svs = [sv_ref[g] for g in groups]
        uys = [_nt_dot(jnp.concatenate([al_ref[rows, lanes[g]], rt_ref[rows, lanes[g]]], axis=0), svs[g])
               for g in groups]
        us = [uys[g][:L] + u0_ref[rows, lanes[g]] for g in groups]
        for g in groups:
            y_ref[rows, lanes[g]] = (uys[g][L:] + y0_ref[rows, lanes[g]]).astype(BF16)
        upds = [_tn_dot(jnp.concatenate([us[g].astype(BF16), v_ref[rows, lanes[g]].astype(BF16)], axis=0),
                        jnp.concatenate([bh_ref[rows, lanes[g]], kh_ref[rows, lanes[g]]], axis=0))
                for g in groups]
        for g in groups:
            sv_ref[g] = svs[g] * pl_ref[ci, :, lanes[g]] + jnp.where(bd_mask, upds[g], 0.0)
        return carry

    lax.fori_loop(0, tb // L, chunk, 0)


def _wkv_state(al_p, rt_p, u0, y0, bh, kh, v, p_l, tb):
    t, c = v.shape
    nc = tb // WKV_CHUNK
    row = lambda: pl.BlockSpec((tb, c), lambda i: (i, 0))
    return pl.pallas_call(
        _wkv_state_kernel,
        grid=(t // tb,),
        in_specs=[row() for _ in range(7)] + [pl.BlockSpec((nc, 1, c), lambda i: (i, 0, 0))],
        out_specs=row(),
        out_shape=jax.ShapeDtypeStruct((t, c), BF16),
        scratch_shapes=[pltpu.VMEM((c // WKV_GROUP, WKV_GROUP, WKV_GROUP), F32)],
        compiler_params=_params("arbitrary"),
    )(al_p, rt_p, u0, y0, bh, kh, v, p_l)


def _wkv_scan(r, lw, k, v, kk, b, tb_local, tb_state):
    al_p, rt_p, u0, y0, bh, kh, p_l = _wkv_local(r, lw, k, v, kk, b, tb_local)
    return _wkv_state(al_p, rt_p, u0, y0, bh, kh, v, p_l, tb_state)


def _once(shape):
    n = len(shape)
    return pl.BlockSpec(shape, lambda *_: (0,) * n, pipeline_mode=pl.Buffered(1))


def _ffn_bounds(f):
    mid = min(f, -(-(f // MXU_DIM) // 2) * MXU_DIM) if f % MXU_DIM == 0 else f
    return tuple(b for b in ((0, mid), (mid, f)) if b[1] > b[0])


def _swiglu_ln(x, win_ref, wdown_ref, lng_ref, lnb_ref):
    xb = x.astype(BF16)
    f = wdown_ref.shape[0]
    acc = jnp.zeros(x.shape, F32)
    for lo, hi in _ffn_bounds(f):
        gate = jnp.dot(xb, win_ref[:, lo:hi], preferred_element_type=F32)
        up = jnp.dot(xb, win_ref[:, f + lo:f + hi], preferred_element_type=F32)
        act = (gate * jax.nn.sigmoid(gate) * up).astype(BF16)
        acc = acc + jnp.dot(act, wdown_ref[lo:hi, :], preferred_element_type=F32)
    return _layer_norm(ALPHA * x + acc, lng_ref[...], lnb_ref[...])


def _rwkv_out_ffn_kernel(y_ref, g_ref, bonus_ref, x_ref, gng_ref, gnb_ref, wo_ref, lng_ref, lnb_ref, hs_ref,
                         win_ref, wdown_ref, lng2_ref, lnb2_ref, o_ref):
    hs = hs_ref[...]
    y = y_ref[...]
    mean = _head_sum(y, hs) * (1.0 / HEAD_DIM)
    d = y - mean
    var = _head_sum(d * d, hs) * (1.0 / HEAD_DIM)
    yn = d * lax.rsqrt(var + GN_EPS) * gng_ref[...] + gnb_ref[...]
    z = (yn + bonus_ref[...]) * g_ref[...]
    h = _bdot(z, wo_ref[...])
    x1 = _layer_norm(ALPHA * x_ref[...] + h, lng_ref[...], lnb_ref[...])
    o_ref[...] = _swiglu_ln(x1, win_ref, wdown_ref, lng2_ref, lnb2_ref)


def _rwkv_out_ffn(y, g, bonus, x, gn_g, gn_b, w_o, ln_g, ln_b, hs, ffn, tm):
    t, c = x.shape
    row = lambda: pl.BlockSpec((tm, c), lambda i: (i, 0))
    consts = [gn_g, gn_b, w_o, ln_g, ln_b, hs, *ffn]
    return pl.pallas_call(
        _rwkv_out_ffn_kernel,
        grid=(t // tm,),
        in_specs=[row() for _ in range(4)] + [_once(a.shape) for a in consts],
        out_specs=row(),
        out_shape=jax.ShapeDtypeStruct((t, c), F32),
        compiler_params=_params("parallel"),
    )(y, g, bonus, x, *consts)


def _attn_out_ffn_kernel(at_ref, x_ref, wo_ref, lng_ref, lnb_ref, win_ref, wdown_ref, lng2_ref, lnb2_ref,
                         o_ref):
    h = _tn_dot(at_ref[...], wo_ref[...])
    x1 = _layer_norm(ALPHA * x_ref[...] + h, lng_ref[...], lnb_ref[...])
    o_ref[...] = _swiglu_ln(x1, win_ref, wdown_ref, lng2_ref, lnb2_ref)


def _attn_out_ffn(at, x, w_o, ln_g, ln_b, ffn, tm):
    t, c = x.shape
    consts = [w_o, ln_g, ln_b, *ffn]
    return pl.pallas_call(
        _attn_out_ffn_kernel,
        grid=(t // tm,),
        in_specs=[pl.BlockSpec((c, tm), lambda i: (0, i)),
                  pl.BlockSpec((tm, c), lambda i: (i, 0))] + [_once(a.shape) for a in consts],
        out_specs=pl.BlockSpec((tm, c), lambda i: (i, 0)),
        out_shape=jax.ShapeDtypeStruct((t, c), F32),
        compiler_params=_params("parallel"),
    )(at, x, *consts)


def _qkv_kernel(x_ref, wqt_ref, wk_ref, wvt_ref, cos_t_ref, sin_t_ref, cos_k_ref, sin_k_ref,
                qt_out, k_out, vt_out, kmean_out):
    xb = x_ref[...].astype(BF16)
    tm, c = xb.shape
    nh = c // HEAD_DIM
    half = HEAD_DIM // 2

    qt = _nt_dot(wqt_ref[...], xb)
    cos_t, sin_t = cos_t_ref[...], sin_t_ref[...]
    scale = HEAD_DIM ** -0.5 * LOG2_E
    for h in range(nh):
        q1 = qt[h * HEAD_DIM:h * HEAD_DIM + half]
        q2 = qt[h * HEAD_DIM + half:(h + 1) * HEAD_DIM]
        qt_out[h, 0:half, :] = ((q1 * cos_t - q2 * sin_t) * scale).astype(BF16)
        qt_out[h, half:HEAD_DIM, :] = ((q2 * cos_t + q1 * sin_t) * scale).astype(BF16)

    vt = _nt_dot(wvt_ref[...], xb)
    for h in range(nh):
        vt_out[h] = vt[h * HEAD_DIM:(h + 1) * HEAD_DIM].astype(BF16)

    k = jnp.dot(xb, wk_ref[...], preferred_element_type=F32)
    reps = c // cos_k_ref.shape[1]
    cos_k = jnp.concatenate([cos_k_ref[...]] * reps, axis=1)
    sin_k = jnp.concatenate([sin_k_ref[...]] * reps, axis=1)
    lane = lax.broadcasted_iota(jnp.int32, k.shape, 1)
    first = (lane % HEAD_DIM) < half
    rot = jnp.where(first, pltpu.roll(k, c - half, 1), pltpu.roll(k, half, 1))
    k = k * cos_k + rot * sin_k
    for h in range(nh):
        k_out[h] = k[:, h * HEAD_DIM:(h + 1) * HEAD_DIM].astype(BF16)
    for blk in range(tm // MOBA_BLOCK):
        kmean_out[blk] = jnp.mean(k[blk * MOBA_BLOCK:(blk + 1) * MOBA_BLOCK], axis=0, keepdims=True)


def _qkv_rope(x, wq_t, wk, wv_t, cos_t, sin_t, cos_k, sin_k, tm):
    t, c = x.shape
    nh = c // HEAD_DIM
    half = HEAD_DIM // 2
    nb = t // MOBA_BLOCK
    return pl.pallas_call(
        _qkv_kernel,
        grid=(t // tm,),
        in_specs=[pl.BlockSpec((tm, c), lambda i: (i, 0)),
                  _full(wq_t.shape), _full(wk.shape), _full(wv_t.shape),
                  pl.BlockSpec((half, tm), lambda i: (0, i)),
                  pl.BlockSpec((half, tm), lambda i: (0, i)),
                  pl.BlockSpec((tm, cos_k.shape[1]), lambda i: (i, 0)),
                  pl.BlockSpec((tm, sin_k.shape[1]), lambda i: (i, 0))],
        out_specs=[pl.BlockSpec((nh, HEAD_DIM, tm), lambda i: (0, 0, i)),
                   pl.BlockSpec((nh, tm, HEAD_DIM), lambda i: (0, i, 0)),
                   pl.BlockSpec((nh, HEAD_DIM, tm), lambda i: (0, 0, i)),
                   pl.BlockSpec((tm // MOBA_BLOCK, 1, c), lambda i: (i, 0, 0))],
        out_shape=[jax.ShapeDtypeStruct((nh, HEAD_DIM, t), BF16),
                   jax.ShapeDtypeStruct((nh, t, HEAD_DIM), BF16),
                   jax.ShapeDtypeStruct((nh, HEAD_DIM, t), BF16),
                   jax.ShapeDtypeStruct((nb, 1, c), F32)],
        compiler_params=_params("parallel"),
    )(x, wq_t, wk, wv_t, cos_t, sin_t, cos_k, sin_k)


def _moba_kernel(qt_ref, k_ref, vt_ref, kmean_ref, o_ref, sel_ref, s_ref):
    i = pl.program_id(1)
    hb, _, bq = qt_ref.shape
    nb = kmean_ref.shape[1]
    blk = MOBA_BLOCK

    qts = [qt_ref[h] for h in range(hb)]
    heads = range(hb)
    ones = jnp.ones((16, blk), BF16)

    def select_blocks():
        bidx = lax.broadcasted_iota(jnp.int32, (nb, bq), 0)
        gates = []
        for h in heads:
            km_hi, km_lo = _split2(kmean_ref[h])
            gate = (jnp.dot(km_hi, qts[h], preferred_element_type=F32)
                    + jnp.dot(km_lo, qts[h], preferred_element_type=F32))
            gates.append(jnp.where(bidx < i, gate, -jnp.inf))
        sels = [jnp.zeros((nb, bq), F32) for _ in heads]
        for _ in range(MOBA_TOPK):
            ms = [jnp.max(g, axis=0, keepdims=True) for g in gates]
            firsts = [jnp.min(jnp.where(jnp.logical_and(gates[h] == ms[h], ms[h] > -jnp.inf), bidx, nb),
                              axis=0, keepdims=True) for h in heads]
            picks = [bidx == f for f in firsts]
            sels = [jnp.where(picks[h], 1.0, sels[h]) for h in heads]
            gates = [jnp.where(picks[h], -jnp.inf, gates[h]) for h in heads]
        for h in heads:
            sel_ref[h] = sels[h]

    def put_scores(j, slot):
        koff = pl.multiple_of(j * blk, blk)
        for h in heads:
            s_ref[slot, h] = jnp.dot(k_ref[h, pl.ds(koff, blk), :], qts[h],
                                     preferred_element_type=F32)

    def attend(j, slot, carry, causal):
        koff = pl.multiple_of(j * blk, blk)
        ss = [s_ref[slot, h] for h in heads]
        if causal:
            key = lax.broadcasted_iota(jnp.int32, (blk, bq), 0)
            qry = lax.broadcasted_iota(jnp.int32, (blk, bq), 1)
            ss = [jnp.where(key <= qry, s, -jnp.inf) for s in ss]
        else:
            ons = [sel_ref[h, pl.ds(j, 1), :] > 0.0 for h in heads]
        cmaxs = [jnp.max(s, axis=0, keepdims=True) for s in ss]
        ms = [carry[h][0] for h in heads]
        if causal:
            m_news = [jnp.maximum(ms[h], cmaxs[h]) for h in heads]
            m_ps = m_news
        else:
            m_news = [jnp.where(ons[h], jnp.maximum(ms[h], cmaxs[h]), ms[h]) for h in heads]
            m_ps = [jnp.where(ons[h], m_news[h], cmaxs[h]) for h in heads]
        ps = [jnp.exp2(ss[h] - m_ps[h]).astype(BF16) for h in heads]
        os = [jnp.dot(jnp.concatenate([vt_ref[h, :, pl.ds(koff, blk)], ones], axis=0), ps[h],
                      preferred_element_type=F32) for h in heads]
        if not causal:
            os = [jnp.where(ons[h], os[h], 0.0) for h in heads]
        return tuple((m_news[h], carry[h][1] * jnp.exp2(ms[h] - m_news[h]) + os[h]) for h in heads)

    init = tuple((jnp.full((1, bq), NEG_BIG, F32), jnp.zeros((HEAD_DIM + 16, bq), F32)) for _ in heads)

    def blocks(j0, count, carry):
        for d in range(count):
            put_scores(j0 + d + 1, (d + 1) % 2)
            carry = attend(j0 + d, d % 2, carry, False)
        return carry

    def finish(carry):
        for h in heads:
            acc = carry[h][1]
            o_ref[h] = acc[:HEAD_DIM] / acc[HEAD_DIM:HEAD_DIM + 1]

    put_scores(0, 0)
    select_blocks()
    carry, done = init, 0
    for u in MOBA_BLOCKS_PER_ITER:
        trips = (i - done) // u
        carry = lax.fori_loop(0, trips, lambda jj, c, u=u, done=done: blocks(done + u * jj, u, c), carry)
        done = done + trips * u

    @pl.when(i % 2 == 0)
    def _():
        finish(attend(i, 0, carry, True))

    @pl.when(i % 2 == 1)
    def _():
        put_scores(i, 1)
        finish(attend(i, 1, attend(i - 1, 0, carry, False), True))


def _moba_attention(qt, k, vt, kmean, hb):
    nh, dh, t = qt.shape
    nb = t // MOBA_BLOCK
    once = pl.Buffered(1)
    return pl.pallas_call(
        _moba_kernel,
        grid=(nh // hb, nb),
        in_specs=[pl.BlockSpec((hb, dh, MOBA_BLOCK), lambda g, i: (g, 0, i)),
                  pl.BlockSpec((hb, t, dh), lambda g, i: (g, 0, 0), pipeline_mode=once),
                  pl.BlockSpec((hb, dh, t), lambda g, i: (g, 0, 0), pipeline_mode=once),
                  pl.BlockSpec((hb, nb, dh), lambda g, i: (g, 0, 0))],
        out_specs=pl.BlockSpec((hb, dh, MOBA_BLOCK), lambda g, i: (g, 0, i)),
        out_shape=jax.ShapeDtypeStruct((nh, dh, t), F32),
        scratch_shapes=[pltpu.VMEM((hb, nb, MOBA_BLOCK), F32),
                        pltpu.VMEM((2, hb, MOBA_BLOCK, MOBA_BLOCK), F32)],
        compiler_params=_params("parallel", "arbitrary"),
    )(qt, k, vt, kmean)


def _row(a):
    return a.reshape(1, -1).astype(F32)


def _rwkv_layer(x, p, ln_g, ln_b, hs, ffn, tm):
    r, lw, k, v, kk, b, g, bonus = _rwkv_proj(x, p, hs, tm)
    t = x.shape[0]
    y = _wkv_scan(r, lw, k, v, kk, b, tb_local=min(WKV_LOCAL_TILE, t), tb_state=min(WKV_STATE_TILE, t))
    return _rwkv_out_ffn(y, g, bonus, x, p['gn_g'], p['gn_b'], p['w_o'], ln_g, ln_b, hs, ffn, tm)


def _rope_tables(t):
    inv = ROPE_THETA ** (-jnp.arange(0, HEAD_DIM, 2, dtype=F32) / HEAD_DIM)
    ang = jnp.arange(t, dtype=F32)[:, None] * inv[None, :]
    cos, sin = jnp.cos(ang), jnp.sin(ang)
    reps = LANES // HEAD_DIM
    cos_k = jnp.tile(jnp.concatenate([cos, cos], axis=1), (1, reps))
    sin_k = jnp.tile(jnp.concatenate([-sin, sin], axis=1), (1, reps))
    return cos.T, sin.T, cos_k, sin_k


def _moba_layer(x, w_qkv, w_o, ln_g, ln_b, ffn, tm):
    t, c = x.shape
    nh = c // HEAD_DIM
    wq_t = w_qkv[:, :c].T.astype(BF16)
    wk = w_qkv[:, c:2 * c].astype(BF16)
    wv_t = w_qkv[:, 2 * c:].T.astype(BF16)
    cos_t, sin_t, cos_k, sin_k = _rope_tables(t)
    qt, k, vt, kmean = _qkv_rope(x, wq_t, wk, wv_t, cos_t, sin_t, cos_k, sin_k, tm)
    kmean = jnp.transpose(kmean.reshape(t // MOBA_BLOCK, nh, HEAD_DIM), (1, 0, 2))
    at = _moba_attention(qt, k, vt, kmean, hb=min(MOBA_HEADS_PER_STEP, nh))
    return _attn_out_ffn(at.reshape(c, t), x, w_o.astype(BF16), ln_g, ln_b, ffn, tm)


def kernel(x, rwkv_mu, rwkv_w_rkv, rwkv_w0, rwkv_w1, rwkv_w2, rwkv_a0, rwkv_a1, rwkv_a2, rwkv_g1, rwkv_g2,
           rwkv_k_k, rwkv_k_a, rwkv_r_k, rwkv_gn_g, rwkv_gn_b, rwkv_w_o, moba_w_qkv, moba_w_o,
           ffn_w_in, ffn_w_down, ln_mix_g, ln_mix_b, ln_ffn_g, ln_ffn_b):
    bsz, t, c = x.shape
    assert c % WKV_GROUP == 0 and t % MOBA_BLOCK == 0
    assert ffn_w_in.shape[0] == DEPTH, "ALPHA is the DeepNorm constant of a DEPTH-layer trunk"
    tm = min(ROW_TILE, t)
    hs = _head_ones()
    outs = []
    for bi in range(bsz):
        h = x[bi]
        for i in range(DEPTH):
            j = i // 2
            ffn = (ffn_w_in[i].astype(BF16), ffn_w_down[i].astype(BF16), _row(ln_ffn_g[i]), _row(ln_ffn_b[i]))
            if i % 2 == 0:
                p = dict(mu=rwkv_mu[j], w_rkv=rwkv_w_rkv[j].astype(BF16), w0=_row(rwkv_w0[j]),
                         w1=rwkv_w1[j].astype(BF16), w2=rwkv_w2[j].astype(BF16), a0=_row(rwkv_a0[j]),
                         a1=rwkv_a1[j].astype(BF16), a2=rwkv_a2[j].astype(BF16),
                         g1=rwkv_g1[j].astype(BF16), g2=rwkv_g2[j].astype(BF16),
                         k_k=_row(rwkv_k_k[j]), k_a=_row(rwkv_k_a[j]), r_k=_row(rwkv_r_k[j]),
                         gn_g=_row(rwkv_gn_g[j]), gn_b=_row(rwkv_gn_b[j]), w_o=rwkv_w_o[j].astype(BF16))
                h = _rwkv_layer(h, p, _row(ln_mix_g[i]), _row(ln_mix_b[i]), hs, ffn, tm)
            else:
                h = _moba_layer(h, moba_w_qkv[j], moba_w_o[j], _row(ln_mix_g[i]), _row(ln_mix_b[i]), ffn, tm)
        outs.append(h)
    return jnp.stack(outs, axis=0)
```

```python
import functools

import jax
import jax.numpy as jnp
from jax import lax
from jax.experimental import pallas as pl
from jax.experimental.pallas import tpu as pltpu

F32 = jnp.float32
BF16 = jnp.bfloat16

HEAD_DIM = 64
DEPTH = 2
ALPHA = (2 * DEPTH) ** 0.25
LN_EPS = 1e-5
GN_EPS = HEAD_DIM * 1e-5
MOBA_BLOCK = 256
MOBA_TOPK = 3
ROPE_THETA = 10000.0
LOG2_E = 1.4426950408889634
MOBA_HEADS_PER_STEP = 4
MOBA_BLOCKS_PER_ITER = (16, 4, 2)

LANES = 128
MXU_DIM = 256
WKV_CHUNK = 64
WKV_GROUP = MXU_DIM
WKV_LOCAL_TILE = 1024
WKV_STATE_TILE = 256
ROW_TILE = 512
SCORE_ROWS = 64
NEG_BIG = -1e30
VMEM_LIMIT = 56 * 1024 * 1024


def _params(*sem):
    return pltpu.CompilerParams(dimension_semantics=sem, vmem_limit_bytes=VMEM_LIMIT)


def _full(shape):
    n = len(shape)
    return pl.BlockSpec(shape, lambda *_: (0,) * n)


def _bdot(a, b):
    return jnp.dot(a.astype(BF16), b.astype(BF16), preferred_element_type=F32)


def _nt_dot(a, b):
    return lax.dot_general(a.astype(BF16), b.astype(BF16), (((1,), (1,)), ((), ())),
                           preferred_element_type=F32)


def _tn_dot(a, b):
    return lax.dot_general(a.astype(BF16), b.astype(BF16), (((0,), (0,)), ((), ())),
                           preferred_element_type=F32)


def _split2(a):
    hi = a.astype(BF16)
    lo = (a - hi.astype(F32)).astype(BF16)
    return hi, lo


def _dot_exact_rhs(a, b_exact):
    if a.dtype == BF16:
        return jnp.dot(a, b_exact, preferred_element_type=F32)
    hi, lo = _split2(a)
    return (jnp.dot(hi, b_exact, preferred_element_type=F32)
            + jnp.dot(lo, b_exact, preferred_element_type=F32))


def _head_sum(a, ones_bd):
    w = ones_bd.shape[0]
    return jnp.concatenate([_dot_exact_rhs(a[:, s:s + w], ones_bd) for s in range(0, a.shape[1], w)],
                           axis=1)


def _layer_norm(z, g, b):
    mu = jnp.mean(z, axis=-1, keepdims=True)
    d = z - mu
    var = jnp.mean(d * d, axis=-1, keepdims=True)
    return d * lax.rsqrt(var + LN_EPS) * g + b


def _head_ones():
    h = jnp.arange(MXU_DIM) // HEAD_DIM
    return (h[:, None] == h[None, :]).astype(BF16)


def _rwkv_proj_kernel(x_ref, xp_ref, mu_ref, wrkv_ref, w0_ref, w1_ref, w2_ref, a0_ref, a1_ref, a2_ref,
                      g1_ref, g2_ref, kk_w_ref, ka_w_ref, rk_w_ref, hs_ref,
                      r_out, lw_out, k_out, v_out, kk_out, b_out, g_out, bonus_out):
    i = pl.program_id(0)
    x = x_ref[...]
    prev_last = jnp.where(i > 0, xp_ref[7:8, :], 0.0)
    row = lax.broadcasted_iota(jnp.int32, x.shape, 0)
    shifted = jnp.where(row == 0, prev_last, pltpu.roll(x, 1, 0))
    xx = shifted - x
    mix = lambda n: x + xx * mu_ref[n:n + 1, :]
    hs = hs_ref[...]

    r = _bdot(mix(0), wrkv_ref[0])
    k = _bdot(mix(2), wrkv_ref[1])
    v = _bdot(mix(3), wrkv_ref[2])
    zw = w0_ref[...] + _bdot(jnp.tanh(_bdot(mix(1), w1_ref[...])), w2_ref[...])
    u = -zw
    w_log = -(jnp.maximum(u, 0.0) + jnp.log(1.0 + jnp.exp(-jnp.abs(u)))) - 0.5
    a = jax.nn.sigmoid(a0_ref[...] + _bdot(_bdot(mix(4), a1_ref[...]), a2_ref[...]))
    g = _bdot(jax.nn.sigmoid(_bdot(mix(5), g1_ref[...])), g2_ref[...])

    kk = k * kk_w_ref[...]
    norm = jnp.sqrt(_head_sum(kk * kk, hs))
    kk = kk / jnp.maximum(norm, 1e-12)
    k = k * (1.0 + (a - 1.0) * ka_w_ref[...])

    r_out[...] = r.astype(BF16)
    lw_out[...] = -jnp.exp(w_log)
    k_out[...] = k.astype(BF16)
    v_out[...] = v.astype(BF16)
    kk_out[...] = kk.astype(BF16)
    b_out[...] = (kk * a).astype(BF16)
    g_out[...] = g.astype(BF16)
    bonus_out[...] = (_head_sum(r * k * rk_w_ref[...], hs) * v).astype(BF16)


def _rwkv_proj(x, p, hs, tm):
    t, c = x.shape
    row = lambda: pl.BlockSpec((tm, c), lambda i: (i, 0))
    prev = pl.BlockSpec((8, c), lambda i: (jnp.maximum(i * (tm // 8) - 1, 0), 0))
    ins = [x, x, p['mu'], p['w_rkv'], p['w0'], p['w1'], p['w2'], p['a0'], p['a1'], p['a2'],
           p['g1'], p['g2'], p['k_k'], p['k_a'], p['r_k'], hs]
    in_specs = [row(), prev] + [_full(a.shape) for a in ins[2:]]
    f32 = jax.ShapeDtypeStruct((t, c), F32)
    bf16 = jax.ShapeDtypeStruct((t, c), BF16)
    return pl.pallas_call(
        _rwkv_proj_kernel,
        grid=(t // tm,),
        in_specs=in_specs,
        out_specs=[row() for _ in range(8)],
        out_shape=[bf16, f32, bf16, bf16, bf16, bf16, bf16, bf16],
        compiler_params=_params("parallel"),
    )(*ins)


def _bmm(a, b):
    return lax.dot_general(a.astype(BF16), b.astype(BF16), (((2,), (1,)), ((0,), (0,))),
                           preferred_element_type=F32)


def _wkv_local_kernel(r_ref, lw_ref, k_ref, v_ref, kk_ref, b_ref,
                      al_out, rt_out, u0_out, y0_out, bh_out, kh_out, pl_out):
    tb, gw = r_ref.shape
    L = WKV_CHUNK
    nc = tb // L
    nh = gw // HEAD_DIM
    n = nh * L

    lane = lax.broadcasted_iota(jnp.int32, (1, 1, gw), 2)
    head_masks = [lane // HEAD_DIM == h for h in range(nh)]
    col = lax.broadcasted_iota(jnp.int32, (1, 1, n), 2)
    col_masks = [col // L == h for h in range(nh)]
    ri = lax.broadcasted_iota(jnp.int32, (1, L, n), 1)
    ci = lax.broadcasted_iota(jnp.int32, (1, L, n), 2) % L
    strict = ci < ri
    incl = ci <= ri
    eye_cat = (ci == ri).astype(F32)
    tt = min(tb, MXU_DIM)
    rb = lax.broadcasted_iota(jnp.int32, (tt, tt), 0)
    cb = lax.broadcasted_iota(jnp.int32, (tt, tt), 1)
    tri = jnp.logical_and(cb <= rb, cb // L == rb // L).astype(BF16)

    def stack(a, masks):
        a16 = a.astype(BF16)
        return jnp.concatenate([jnp.where(m, a16, 0) for m in masks], axis=1)

    lw2 = lw_ref[...]
    hi, lo = _split2(lw2)
    cum2 = jnp.concatenate(
        [jnp.dot(tri, hi[s:s + tt], preferred_element_type=F32)
         + jnp.dot(tri, lo[s:s + tt], preferred_element_type=F32) for s in range(0, tb, tt)], axis=0)
    to3 = lambda a: a.reshape(nc, L, gw)
    cum, lw = to3(cum2), to3(lw2)
    r, k, v, kk, b = (to3(ref[...]) for ref in (r_ref, k_ref, v_ref, kk_ref, b_ref))
    cum_l = cum[:, L - 1:L, :]
    p_inv = jnp.exp(-cum)
    al = -(kk * jnp.exp(cum - lw))
    bt = b * p_inv
    kt = k * p_inv
    rt = r * jnp.exp(cum)
    dec = jnp.exp(cum_l - cum)

    xs = jnp.concatenate([al, rt], axis=1)
    ys = jnp.concatenate([stack(bt, head_masks), stack(kt, head_masks)], axis=1)
    res = lax.dot_general(xs.astype(BF16), ys.astype(BF16), (((2,), (2,)), ((0,), (0,))),
                          preferred_element_type=F32)
    a_ab = jnp.where(strict, res[:, :L, :n], 0.0)
    a_ak = jnp.where(strict, res[:, :L, n:], 0.0)
    a_rb = jnp.where(incl, res[:, L:, :n], 0.0)
    a_rk = jnp.where(incl, res[:, L:, n:], 0.0)

    tinv = eye_cat + jnp.where(jnp.logical_and(ri // 2 == ci // 2, ci < ri), a_ab, 0.0)
    s = 4
    while s <= L:
        quad = jnp.logical_and(ri // s == ci // s, jnp.logical_and(ri % s >= s // 2, ci % s < s // 2))
        x = _bmm(jnp.where(quad, a_ab, 0.0), stack(tinv, col_masks))
        tinv = tinv + _bmm(tinv, stack(x, col_masks))
        s *= 2

    av = _bmm(jnp.concatenate([a_ak, a_rk], axis=1), stack(v, head_masks))
    tg = jnp.concatenate([tinv, _bmm(a_rb, stack(tinv, col_masks))], axis=1)
    on_al = _bmm(tg, stack(al, head_masks))
    on_av = _bmm(tg, stack(av[:, :L], head_masks))
    al_p, rt_p = on_al[:, :L], rt + on_al[:, L:]
    u0, y0 = on_av[:, :L], on_av[:, L:] + av[:, L:]

    to2 = lambda a: a.reshape(tb, gw)
    al_out[...] = to2(al_p).astype(BF16)
    rt_out[...] = to2(rt_p).astype(BF16)
    u0_out[...] = to2(u0).astype(BF16)
    y0_out[...] = to2(y0).astype(BF16)
    bh_out[...] = to2(b * dec).astype(BF16)
    kh_out[...] = to2(k * dec).astype(BF16)
    pl_out[...] = jnp.exp(cum_l)


def _wkv_local(r, lw, k, v, kk, b, tb):
    t, c = r.shape
    gw = WKV_GROUP
    nc = tb // WKV_CHUNK
    spec = lambda: pl.BlockSpec((tb, gw), lambda i, g: (i, g))
    f32 = jax.ShapeDtypeStruct((t, c), F32)
    bf16 = jax.ShapeDtypeStruct((t, c), BF16)
    return pl.pallas_call(
        _wkv_local_kernel,
        grid=(t // tb, c // gw),
        in_specs=[spec() for _ in range(6)],
        out_specs=[spec() for _ in range(6)] + [pl.BlockSpec((nc, 1, gw), lambda i, g: (i, 0, g))],
        out_shape=[bf16, bf16, bf16, bf16, bf16, bf16,
                   jax.ShapeDtypeStruct((t // WKV_CHUNK, 1, c), F32)],
        compiler_params=_params("parallel", "parallel"),
    )(r, lw, k, v, kk, b)


def _wkv_state_kernel(al_ref, rt_ref, u0_ref, y0_ref, bh_ref, kh_ref, v_ref, pl_ref, y_ref, sv_ref):
    tb, c = v_ref.shape
    L = WKV_CHUNK
    gw = WKV_GROUP
    groups = range(c // gw)

    @pl.when(pl.program_id(0) == 0)
    def _():
        sv_ref[...] = jnp.zeros_like(sv_ref)

    bd_mask = (lax.broadcasted_iota(jnp.int32, (gw, gw), 0) // HEAD_DIM
               == lax.broadcasted_iota(jnp.int32, (gw, gw), 1) // HEAD_DIM)

    def chunk(ci, carry):
        rows = pl.ds(pl.multiple_of(ci * L, L), L)
        lanes = [slice(g * gw, (g + 1) * gw) for g in groups]
        svs = [sv_ref[g] for g in groups]
        uys = [_nt_dot(jnp.concatenate([al_ref[rows, lanes[g]], rt_ref[rows, lanes[g]]], axis=0), svs[g])
               for g in groups]
        us = [uys[g][:L] + u0_ref[rows, lanes[g]] for g in groups]
        for g in groups:
            y_ref[rows, lanes[g]] = (uys[g][L:] + y0_ref[rows, lanes[g]]).astype(BF16)
        upds = [_tn_dot(jnp.concatenate([us[g].astype(BF16), v_ref[rows, lanes[g]].astype(BF16)], axis=0),
                        jnp.concatenate([bh_ref[rows, lanes[g]], kh_ref[rows, lanes[g]]], axis=0))
                for g in groups]
        for g in groups:
            sv_ref[g] = svs[g] * pl_ref[ci, :, lanes[g]] + jnp.where(bd_mask, upds[g], 0.0)
        return carry

    lax.fori_loop(0, tb // L, chunk, 0)


def _wkv_state(al_p, rt_p, u0, y0, bh, kh, v, p_l, tb):
    t, c = v.shape
    nc = tb // WKV_CHUNK
    row = lambda: pl.BlockSpec((tb, c), lambda i: (i, 0))
    return pl.pallas_call(
        _wkv_state_kernel,
        grid=(t // tb,),
        in_specs=[row() for _ in range(7)] + [pl.BlockSpec((nc, 1, c), lambda i: (i, 0, 0))],
        out_specs=row(),
        out_shape=jax.ShapeDtypeStruct((t, c), BF16),
        scratch_shapes=[pltpu.VMEM((c // WKV_GROUP, WKV_GROUP, WKV_GROUP), F32)],
        compiler_params=_params("arbitrary"),
    )(al_p, rt_p, u0, y0, bh, kh, v, p_l)


def _wkv_scan(r, lw, k, v, kk, b, tb_local, tb_state):
    al_p, rt_p, u0, y0, bh, kh, p_l = _wkv_local(r, lw, k, v, kk, b, tb_local)
    return _wkv_state(al_p, rt_p, u0, y0, bh, kh, v, p_l, tb_state)


def _once(shape):
    n = len(shape)
    return pl.BlockSpec(shape, lambda *_: (0,) * n, pipeline_mode=pl.Buffered(1))


def _ffn_bounds(f):
    mid = min(f, -(-(f // MXU_DIM) // 2) * MXU_DIM) if f % MXU_DIM == 0 else f
    return tuple(b for b in ((0, mid), (mid, f)) if b[1] > b[0])


def _swiglu_ln(x, win_ref, wdown_ref, lng_ref, lnb_ref):
    xb = x.astype(BF16)
    f = wdown_ref.shape[0]
    acc = jnp.zeros(x.shape, F32)
    for lo, hi in _ffn_bounds(f):
        gate = jnp.dot(xb, win_ref[:, lo:hi], preferred_element_type=F32)
        up = jnp.dot(xb, win_ref[:, f + lo:f + hi], preferred_element_type=F32)
        act = (gate * jax.nn.sigmoid(gate) * up).astype(BF16)
        acc = acc + jnp.dot(act, wdown_ref[lo:hi, :], preferred_element_type=F32)
    return _layer_norm(ALPHA * x + acc, lng_ref[...], lnb_ref[...])


def _rwkv_out_ffn_kernel(y_ref, g_ref, bonus_ref, x_ref, gng_ref, gnb_ref, wo_ref, lng_ref, lnb_ref, hs_ref,
                         win_ref, wdown_ref, lng2_ref, lnb2_ref, o_ref):
    hs = hs_ref[...]
    y = y_ref[...]
    mean = _head_sum(y, hs) * (1.0 / HEAD_DIM)
    d = y - mean
    var = _head_sum(d * d, hs) * (1.0 / HEAD_DIM)
    yn = d * lax.rsqrt(var + GN_EPS) * gng_ref[...] + gnb_ref[...]
    z = (yn + bonus_ref[...]) * g_ref[...]
    h = _bdot(z, wo_ref[...])
    x1 = _layer_norm(ALPHA * x_ref[...] + h, lng_ref[...], lnb_ref[...])
    o_ref[...] = _swiglu_ln(x1, win_ref, wdown_ref, lng2_ref, lnb2_ref)


def _rwkv_out_ffn(y, g, bonus, x, gn_g, gn_b, w_o, ln_g, ln_b, hs, ffn, tm):
    t, c = x.shape
    row = lambda: pl.BlockSpec((tm, c), lambda i: (i, 0))
    consts = [gn_g, gn_b, w_o, ln_g, ln_b, hs, *ffn]
    return pl.pallas_call(
        _rwkv_out_ffn_kernel,
        grid=(t // tm,),
        in_specs=[row() for _ in range(4)] + [_once(a.shape) for a in consts],
        out_specs=row(),
        out_shape=jax.ShapeDtypeStruct((t, c), F32),
        compiler_params=_params("parallel"),
    )(y, g, bonus, x, *consts)


def _attn_out_ffn_kernel(at_ref, x_ref, wo_ref, lng_ref, lnb_ref, win_ref, wdown_ref, lng2_ref, lnb2_ref,
                         o_ref):
    h = _tn_dot(at_ref[...], wo_ref[...])
    x1 = _layer_norm(ALPHA * x_ref[...] + h, lng_ref[...], lnb_ref[...])
    o_ref[...] = _swiglu_ln(x1, win_ref, wdown_ref, lng2_ref, lnb2_ref)


def _attn_out_ffn(at, x, w_o, ln_g, ln_b, ffn, tm):
    t, c = x.shape
    consts = [w_o, ln_g, ln_b, *ffn]
    return pl.pallas_call(
        _attn_out_ffn_kernel,
        grid=(t // tm,),
        in_specs=[pl.BlockSpec((c, tm), lambda i: (0, i)),
                  pl.BlockSpec((tm, c), lambda i: (i, 0))] + [_once(a.shape) for a in consts],
        out_specs=pl.BlockSpec((tm, c), lambda i: (i, 0)),
        out_shape=jax.ShapeDtypeStruct((t, c), F32),
        compiler_params=_params("parallel"),
    )(at, x, *consts)


def _qkv_kernel(x_ref, wqt_ref, wk_ref, wvt_ref, cos_t_ref, sin_t_ref, cos_k_ref, sin_k_ref,
                qt_out, k_out, vt_out, kmean_out):
    xb = x_ref[...].astype(BF16)
    tm, c = xb.shape
    nh = c // HEAD_DIM
    half = HEAD_DIM // 2

    qt = _nt_dot(wqt_ref[...], xb)
    cos_t, sin_t = cos_t_ref[...], sin_t_ref[...]
    scale = HEAD_DIM ** -0.5 * LOG2_E
    for h in range(nh):
        q1 = qt[h * HEAD_DIM:h * HEAD_DIM + half]
        q2 = qt[h * HEAD_DIM + half:(h + 1) * HEAD_DIM]
        qt_out[h, 0:half, :] = ((q1 * cos_t - q2 * sin_t) * scale).astype(BF16)
        qt_out[h, half:HEAD_DIM, :] = ((q2 * cos_t + q1 * sin_t) * scale).astype(BF16)

    vt = _nt_dot(wvt_ref[...], xb)
    for h in range(nh):
        vt_out[h] = vt[h * HEAD_DIM:(h + 1) * HEAD_DIM].astype(BF16)

    k = jnp.dot(xb, wk_ref[...], preferred_element_type=F32)
    reps = c // cos_k_ref.shape[1]
    cos_k = jnp.concatenate([cos_k_ref[...]] * reps, axis=1)
    sin_k = jnp.concatenate([sin_k_ref[...]] * reps, axis=1)
    lane = lax.broadcasted_iota(jnp.int32, k.shape, 1)
    first = (lane % HEAD_DIM) < half
    rot = jnp.where(first, pltpu.roll(k, c - half, 1), pltpu.roll(k, half, 1))
    k = k * cos_k + rot * sin_k
    for h in range(nh):
        k_out[h] = k[:, h * HEAD_DIM:(h + 1) * HEAD_DIM].astype(BF16)
    for blk in range(tm // MOBA_BLOCK):
        kmean_out[blk] = jnp.mean(k[blk * MOBA_BLOCK:(blk + 1) * MOBA_BLOCK], axis=0, keepdims=True)


def _qkv_rope(x, wq_t, wk, wv_t, cos_t, sin_t, cos_k, sin_k, tm):
    t, c = x.shape
    nh = c // HEAD_DIM
    half = HEAD_DIM // 2
    nb = t // MOBA_BLOCK
    return pl.pallas_call(
        _qkv_kernel,
        grid=(t // tm,),
        in_specs=[pl.BlockSpec((tm, c), lambda i: (i, 0)),
                  _full(wq_t.shape), _full(wk.shape), _full(wv_t.shape),
                  pl.BlockSpec((half, tm), lambda i: (0, i)),
                  pl.BlockSpec((half, tm), lambda i: (0, i)),
                  pl.BlockSpec((tm, cos_k.shape[1]), lambda i: (i, 0)),
                  pl.BlockSpec((tm, sin_k.shape[1]), lambda i: (i, 0))],
        out_specs=[pl.BlockSpec((nh, HEAD_DIM, tm), lambda i: (0, 0, i)),
                   pl.BlockSpec((nh, tm, HEAD_DIM), lambda i: (0, i, 0)),
                   pl.BlockSpec((nh, HEAD_DIM, tm), lambda i: (0, 0, i)),
                   pl.BlockSpec((tm // MOBA_BLOCK, 1, c), lambda i: (i, 0, 0))],
        out_shape=[jax.ShapeDtypeStruct((nh, HEAD_DIM, t), BF16),
                   jax.ShapeDtypeStruct((nh, t, HEAD_DIM), BF16),
                   jax.ShapeDtypeStruct((nh, HEAD_DIM, t), BF16),
                   jax.ShapeDtypeStruct((nb, 1, c), F32)],
        compiler_params=_params("parallel"),
    )(x, wq_t, wk, wv_t, cos_t, sin_t, cos_k, sin_k)


def _moba_kernel(qt_ref, k_ref, vt_ref, kmean_ref, o_ref, sel_ref, s_ref):
    i = pl.program_id(1)
    hb, _, bq = qt_ref.shape
    nb = kmean_ref.shape[1]
    blk = MOBA_BLOCK

    qts = [qt_ref[h] for h in range(hb)]
    heads = range(hb)
    ones = jnp.ones((16, blk), BF16)

    def select_blocks():
        bidx = lax.broadcasted_iota(jnp.int32, (nb, bq), 0)
        gates = []
        for h in heads:
            km_hi, km_lo = _split2(kmean_ref[h])
            gate = (jnp.dot(km_hi, qts[h], preferred_element_type=F32)
                    + jnp.dot(km_lo, qts[h], preferred_element_type=F32))
            gates.append(jnp.where(bidx < i, gate, -jnp.inf))
        sels = [jnp.zeros((nb, bq), F32) for _ in heads]
        for _ in range(MOBA_TOPK):
            ms = [jnp.max(g, axis=0, keepdims=True) for g in gates]
            firsts = [jnp.min(jnp.where(jnp.logical_and(gates[h] == ms[h], ms[h] > -jnp.inf), bidx, nb),
                              axis=0, keepdims=True) for h in heads]
            picks = [bidx == f for f in firsts]
            sels = [jnp.where(picks[h], 1.0, sels[h]) for h in heads]
            gates = [jnp.where(picks[h], -jnp.inf, gates[h]) for h in heads]
        for h in heads:
            sel_ref[h] = sels[h]

    def put_scores(j, slot):
        koff = pl.multiple_of(j * blk, blk)
        for h in heads:
            s_ref[slot, h] = jnp.dot(k_ref[h, pl.ds(koff, blk), :], qts[h],
                                     preferred_element_type=F32)

    def attend(j, slot, carry, causal):
        koff = pl.multiple_of(j * blk, blk)
        rows = [slice(r0, r0 + SCORE_ROWS) for r0 in range(0, blk, SCORE_ROWS)]

        def scores(h, rs):
            s = s_ref[slot, h, rs, :]
            if causal:
                key = lax.broadcasted_iota(jnp.int32, s.shape, 0) + rs.start
                qry = lax.broadcasted_iota(jnp.int32, s.shape, 1)
                s = jnp.where(key <= qry, s, -jnp.inf)
            return s

        if not causal:
            ons = [sel_ref[h, pl.ds(j, 1), :] > 0.0 for h in heads]
        cmaxs = []
        for h in heads:
            part = [jnp.max(scores(h, rs), axis=0, keepdims=True) for rs in rows]
            cmaxs.append(functools.reduce(jnp.maximum, part))
        ms = [carry[h][0] for h in heads]
        if causal:
            m_news = [jnp.maximum(ms[h], cmaxs[h]) for h in heads]
            m_ps = m_news
        else:
            m_news = [jnp.where(ons[h], jnp.maximum(ms[h], cmaxs[h]), ms[h]) for h in heads]
            m_ps = [jnp.where(ons[h], m_news[h], cmaxs[h]) for h in heads]
        ps = [jnp.concatenate([jnp.exp2(scores(h, rs) - m_ps[h]).astype(BF16) for rs in rows], axis=0)
              for h in heads]
        os = [jnp.dot(jnp.concatenate([vt_ref[h, :, pl.ds(koff, blk)], ones], axis=0), ps[h],
                      preferred_element_type=F32) for h in heads]
        if not causal:
            os = [jnp.where(ons[h], os[h], 0.0) for h in heads]
        return tuple((m_news[h], carry[h][1] * jnp.exp2(ms[h] - m_news[h]) + os[h]) for h in heads)

    init = tuple((jnp.full((1, bq), NEG_BIG, F32), jnp.zeros((HEAD_DIM + 16, bq), F32)) for _ in heads)

    def blocks(j0, count, carry):
        for d in range(count):
            put_scores(j0 + d + 1, (d + 1) % 2)
            carry = attend(j0 + d, d % 2, carry, False)
        return carry

    def finish(carry):
        for h in heads:
            acc = carry[h][1]
            o_ref[h] = acc[:HEAD_DIM] / acc[HEAD_DIM:HEAD_DIM + 1]

    put_scores(0, 0)
    select_blocks()
    carry, done = init, 0
    for u in MOBA_BLOCKS_PER_ITER:
        trips = (i - done) // u
        carry = lax.fori_loop(0, trips, lambda jj, c, u=u, done=done: blocks(done + u * jj, u, c), carry)
        done = done + trips * u

    @pl.when(i % 2 == 0)
    def _():
        finish(attend(i, 0, carry, True))

    @pl.when(i % 2 == 1)
    def _():
        put_scores(i, 1)
        finish(attend(i, 1, attend(i - 1, 0, carry, False), True))


def _moba_attention(qt, k, vt, kmean, hb):
    nh, dh, t = qt.shape
    nb = t // MOBA_BLOCK
    once = pl.Buffered(1)
    return pl.pallas_call(
        _moba_kernel,
        grid=(nh // hb, nb),
        in_specs=[pl.BlockSpec((hb, dh, MOBA_BLOCK), lambda g, i: (g, 0, i)),
                  pl.BlockSpec((hb, t, dh), lambda g, i: (g, 0, 0), pipeline_mode=once),
                  pl.BlockSpec((hb, dh, t), lambda g, i: (g, 0, 0), pipeline_mode=once),
                  pl.BlockSpec((hb, nb, dh), lambda g, i: (g, 0, 0))],
        out_specs=pl.BlockSpec((hb, dh, MOBA_BLOCK), lambda g, i: (g, 0, i)),
        out_shape=jax.ShapeDtypeStruct((nh, dh, t), F32),
        scratch_shapes=[pltpu.VMEM((hb, nb, MOBA_BLOCK), F32),
                        pltpu.VMEM((2, hb, MOBA_BLOCK, MOBA_BLOCK), F32)],
        compiler_params=_params("parallel", "arbitrary"),
    )(qt, k, vt, kmean)


def _row(a):
    return a.reshape(1, -1).astype(F32)


def _rwkv_layer(x, p, ln_g, ln_b, hs, ffn, tm):
    r, lw, k, v, kk, b, g, bonus = _rwkv_proj(x, p, hs, tm)
    t = x.shape[0]
    y = _wkv_scan(r, lw, k, v, kk, b, tb_local=min(WKV_LOCAL_TILE, t), tb_state=min(WKV_STATE_TILE, t))
    return _rwkv_out_ffn(y, g, bonus, x, p['gn_g'], p['gn_b'], p['w_o'], ln_g, ln_b, hs, ffn, tm)


def _rope_tables(t):
    inv = ROPE_THETA ** (-jnp.arange(0, HEAD_DIM, 2, dtype=F32) / HEAD_DIM)
    ang = jnp.arange(t, dtype=F32)[:, None] * inv[None, :]
    cos, sin = jnp.cos(ang), jnp.sin(ang)
    reps = LANES // HEAD_DIM
    cos_k = jnp.tile(jnp.concatenate([cos, cos], axis=1), (1, reps))
    sin_k = jnp.tile(jnp.concatenate([-sin, sin], axis=1), (1, reps))
    return cos.T, sin.T, cos_k, sin_k


def _moba_layer(x, w_qkv, w_o, ln_g, ln_b, ffn, tm):
    t, c = x.shape
    nh = c // HEAD_DIM
    wq_t = w_qkv[:, :c].T.astype(BF16)
    wk = w_qkv[:, c:2 * c].astype(BF16)
    wv_t = w_qkv[:, 2 * c:].T.astype(BF16)
    cos_t, sin_t, cos_k, sin_k = _rope_tables(t)
    qt, k, vt, kmean = _qkv_rope(x, wq_t, wk, wv_t, cos_t, sin_t, cos_k, sin_k, tm)
    kmean = jnp.transpose(kmean.reshape(t // MOBA_BLOCK, nh, HEAD_DIM), (1, 0, 2))
    at = _moba_attention(qt, k, vt, kmean, hb=min(MOBA_HEADS_PER_STEP, nh))
    return _attn_out_ffn(at.reshape(c, t), x, w_o.astype(BF16), ln_g, ln_b, ffn, tm)


def kernel(x, rwkv_mu, rwkv_w_rkv, rwkv_w0, rwkv_w1, rwkv_w2, rwkv_a0, rwkv_a1, rwkv_a2, rwkv_g1, rwkv_g2,
           rwkv_k_k, rwkv_k_a, rwkv_r_k, rwkv_gn_g, rwkv_gn_b, rwkv_w_o, moba_w_qkv, moba_w_o,
           ffn_w_in, ffn_w_down, ln_mix_g, ln_mix_b, ln_ffn_g, ln_ffn_b):
    bsz, t, c = x.shape
    assert c % WKV_GROUP == 0 and t % MOBA_BLOCK == 0
    assert ffn_w_in.shape[0] == DEPTH, "ALPHA is the DeepNorm constant of a DEPTH-layer trunk"
    tm = min(ROW_TILE, t)
    hs = _head_ones()
    outs = []
    for bi in range(bsz):
        h = x[bi]
        for i in range(DEPTH):
            j = i // 2
            ffn = (ffn_w_in[i].astype(BF16), ffn_w_down[i].astype(BF16), _row(ln_ffn_g[i]), _row(ln_ffn_b[i]))
            if i % 2 == 0:
                p = dict(mu=rwkv_mu[j], w_rkv=rwkv_w_rkv[j].astype(BF16), w0=_row(rwkv_w0[j]),
                         w1=rwkv_w1[j].astype(BF16), w2=rwkv_w2[j].astype(BF16), a0=_row(rwkv_a0[j]),
                         a1=rwkv_a1[j].astype(BF16), a2=rwkv_a2[j].astype(BF16),
                         g1=rwkv_g1[j].astype(BF16), g2=rwkv_g2[j].astype(BF16),
                         k_k=_row(rwkv_k_k[j]), k_a=_row(rwkv_k_a[j]), r_k=_row(rwkv_r_k[j]),
                         gn_g=_row(rwkv_gn_g[j]), gn_b=_row(rwkv_gn_b[j]), w_o=rwkv_w_o[j].astype(BF16))
                h = _rwkv_layer(h, p, _row(ln_mix_g[i]), _row(ln_mix_b[i]), hs, ffn, tm)
            else:
                h = _moba_layer(h, moba_w_qkv[j], moba_w_o[j], _row(ln_mix_g[i]), _row(ln_mix_b[i]), ffn, tm)
        outs.append(h)
    return jnp.stack(outs, axis=0)
```

```python
import jax
import jax.numpy as jnp
from jax import lax
from jax.experimental import pallas as pl
from jax.experimental.pallas import tpu as pltpu

F32 = jnp.float32
BF16 = jnp.bfloat16

HEAD_DIM = 64
DEPTH = 2
ALPHA = (2 * DEPTH) ** 0.25
LN_EPS = 1e-5
GN_EPS = HEAD_DIM * 1e-5
MOBA_BLOCK = 256
MOBA_TOPK = 3
ROPE_THETA = 10000.0
LOG2_E = 1.4426950408889634
MOBA_HEADS_PER_STEP = 4
MOBA_BLOCKS_PER_ITER = (16, 4, 2)

LANES = 128
MXU_DIM = 256
WKV_CHUNK = 64
WKV_GROUP = MXU_DIM
WKV_LOCAL_TILE = 1024
WKV_STATE_TILE = 256
ROW_TILE = 512
NEG_BIG = -1e30
VMEM_LIMIT = 56 * 1024 * 1024


def _params(*sem):
    return pltpu.CompilerParams(dimension_semantics=sem, vmem_limit_bytes=VMEM_LIMIT)


def _full(shape):
    n = len(shape)
    return pl.BlockSpec(shape, lambda *_: (0,) * n)


def _bdot(a, b):
    return jnp.dot(a.astype(BF16), b.astype(BF16), preferred_element_type=F32)


def _nt_dot(a, b):
    return lax.dot_general(a.astype(BF16), b.astype(BF16), (((1,), (1,)), ((), ())),
                           preferred_element_type=F32)


def _tn_dot(a, b):
    return lax.dot_general(a.astype(BF16), b.astype(BF16), (((0,), (0,)), ((), ())),
                           preferred_element_type=F32)


def _split2(a):
    hi = a.astype(BF16)
    lo = (a - hi.astype(F32)).astype(BF16)
    return hi, lo


def _dot_exact_rhs(a, b_exact):
    if a.dtype == BF16:
        return jnp.dot(a, b_exact, preferred_element_type=F32)
    hi, lo = _split2(a)
    return (jnp.dot(hi, b_exact, preferred_element_type=F32)
            + jnp.dot(lo, b_exact, preferred_element_type=F32))


def _head_sum(a, ones_bd):
    w = ones_bd.shape[0]
    return jnp.concatenate([_dot_exact_rhs(a[:, s:s + w], ones_bd) for s in range(0, a.shape[1], w)],
                           axis=1)


def _layer_norm(z, g, b):
    mu = jnp.mean(z, axis=-1, keepdims=True)
    d = z - mu
    var = jnp.mean(d * d, axis=-1, keepdims=True)
    return d * lax.rsqrt(var + LN_EPS) * g + b


def _head_ones():
    h = jnp.arange(MXU_DIM) // HEAD_DIM
    return (h[:, None] == h[None, :]).astype(BF16)


def _rwkv_proj_kernel(x_ref, xp_ref, mu_ref, wrkv_ref, w0_ref, w1_ref, w2_ref, a0_ref, a1_ref, a2_ref,
                      g1_ref, g2_ref, kk_w_ref, ka_w_ref, rk_w_ref, hs_ref,
                      r_out, lw_out, k_out, v_out, kk_out, b_out, g_out, bonus_out):
    i = pl.program_id(0)
    x = x_ref[...]
    prev_last = jnp.where(i > 0, xp_ref[7:8, :], 0.0)
    row = lax.broadcasted_iota(jnp.int32, x.shape, 0)
    shifted = jnp.where(row == 0, prev_last, pltpu.roll(x, 1, 0))
    xx = shifted - x
    mix = lambda n: x + xx * mu_ref[n:n + 1, :]
    hs = hs_ref[...]

    r = _bdot(mix(0), wrkv_ref[0])
    k = _bdot(mix(2), wrkv_ref[1])
    v = _bdot(mix(3), wrkv_ref[2])
    zw = w0_ref[...] + _bdot(jnp.tanh(_bdot(mix(1), w1_ref[...])), w2_ref[...])
    u = -zw
    w_log = -(jnp.maximum(u, 0.0) + jnp.log(1.0 + jnp.exp(-jnp.abs(u)))) - 0.5
    a = jax.nn.sigmoid(a0_ref[...] + _bdot(_bdot(mix(4), a1_ref[...]), a2_ref[...]))
    g = _bdot(jax.nn.sigmoid(_bdot(mix(5), g1_ref[...])), g2_ref[...])

    kk = k * kk_w_ref[...]
    norm = jnp.sqrt(_head_sum(kk * kk, hs))
    kk = kk / jnp.maximum(norm, 1e-12)
    k = k * (1.0 + (a - 1.0) * ka_w_ref[...])

    r_out[...] = r.astype(BF16)
    lw_out[...] = -jnp.exp(w_log)
    k_out[...] = k.astype(BF16)
    v_out[...] = v.astype(BF16)
    kk_out[...] = kk.astype(BF16)
    b_out[...] = (kk * a).astype(BF16)
    g_out[...] = g.astype(BF16)
    bonus_out[...] = (_head_sum(r * k * rk_w_ref[...], hs) * v).astype(BF16)


def _rwkv_proj(x, p, hs, tm):
    t, c = x.shape
    row = lambda: pl.BlockSpec((tm, c), lambda i: (i, 0))
    prev = pl.BlockSpec((8, c), lambda i: (jnp.maximum(i * (tm // 8) - 1, 0), 0))
    ins = [x, x, p['mu'], p['w_rkv'], p['w0'], p['w1'], p['w2'], p['a0'], p['a1'], p['a2'],
           p['g1'], p['g2'], p['k_k'], p['k_a'], p['r_k'], hs]
    in_specs = [row(), prev] + [_full(a.shape) for a in ins[2:]]
    f32 = jax.ShapeDtypeStruct((t, c), F32)
    bf16 = jax.ShapeDtypeStruct((t, c), BF16)
    return pl.pallas_call(
        _rwkv_proj_kernel,
        grid=(t // tm,),
        in_specs=in_specs,
        out_specs=[row() for _ in range(8)],
        out_shape=[bf16, f32, bf16, bf16, bf16, bf16, bf16, bf16],
        compiler_params=_params("parallel"),
    )(*ins)


def _bmm(a, b):
    return lax.dot_general(a.astype(BF16), b.astype(BF16), (((2,), (1,)), ((0,), (0,))),
                           preferred_element_type=F32)


def _wkv_local_kernel(r_ref, lw_ref, k_ref, v_ref, kk_ref, b_ref,
                      al_out, rt_out, u0_out, y0_out, bh_out, kh_out, pl_out):
    tb, gw = r_ref.shape
    L = WKV_CHUNK
    nc = tb // L
    nh = gw // HEAD_DIM
    n = nh * L

    lane = lax.broadcasted_iota(jnp.int32, (1, 1, gw), 2)
    head_masks = [lane // HEAD_DIM == h for h in range(nh)]
    col = lax.broadcasted_iota(jnp.int32, (1, 1, n), 2)
    col_masks = [col // L == h for h in range(nh)]
    ri = lax.broadcasted_iota(jnp.int32, (1, L, n), 1)
    ci = lax.broadcasted_iota(jnp.int32, (1, L, n), 2) % L
    strict = ci < ri
    incl = ci <= ri
    eye_cat = (ci == ri).astype(F32)
    tt = min(tb, MXU_DIM)
    rb = lax.broadcasted_iota(jnp.int32, (tt, tt), 0)
    cb = lax.broadcasted_iota(jnp.int32, (tt, tt), 1)
    tri = jnp.logical_and(cb <= rb, cb // L == rb // L).astype(BF16)

    def stack(a, masks):
        a16 = a.astype(BF16)
        return jnp.concatenate([jnp.where(m, a16, 0) for m in masks], axis=1)

    lw2 = lw_ref[...]
    hi, lo = _split2(lw2)
    cum2 = jnp.concatenate(
        [jnp.dot(tri, hi[s:s + tt], preferred_element_type=F32)
         + jnp.dot(tri, lo[s:s + tt], preferred_element_type=F32) for s in range(0, tb, tt)], axis=0)
    to3 = lambda a: a.reshape(nc, L, gw)
    cum, lw = to3(cum2), to3(lw2)
    r, k, v, kk, b = (to3(ref[...]) for ref in (r_ref, k_ref, v_ref, kk_ref, b_ref))
    cum_l = cum[:, L - 1:L, :]
    p_inv = jnp.exp(-cum)
    al = -(kk * jnp.exp(cum - lw))
    bt = b * p_inv
    kt = k * p_inv
    rt = r * jnp.exp(cum)
    dec = jnp.exp(cum_l - cum)

    xs = jnp.concatenate([al, rt], axis=1)
    ys = jnp.concatenate([stack(bt, head_masks), stack(kt, head_masks)], axis=1)
    res = lax.dot_general(xs.astype(BF16), ys.astype(BF16), (((2,), (2,)), ((0,), (0,))),
                          preferred_element_type=F32)
    a_ab = jnp.where(strict, res[:, :L, :n], 0.0)
    a_ak = jnp.where(strict, res[:, :L, n:], 0.0)
    a_rb = jnp.where(incl, res[:, L:, :n], 0.0)
    a_rk = jnp.where(incl, res[:, L:, n:], 0.0)

    tinv = eye_cat + jnp.where(jnp.logical_and(ri // 2 == ci // 2, ci < ri), a_ab, 0.0)
    s = 4
    while s <= L:
        quad = jnp.logical_and(ri // s == ci // s, jnp.logical_and(ri % s >= s // 2, ci % s < s // 2))
        x = _bmm(jnp.where(quad, a_ab, 0.0), stack(tinv, col_masks))
        tinv = tinv + _bmm(tinv, stack(x, col_masks))
        s *= 2

    av = _bmm(jnp.concatenate([a_ak, a_rk], axis=1), stack(v, head_masks))
    tg = jnp.concatenate([tinv, _bmm(a_rb, stack(tinv, col_masks))], axis=1)
    on_al = _bmm(tg, stack(al, head_masks))
    on_av = _bmm(tg, stack(av[:, :L], head_masks))
    al_p, rt_p = on_al[:, :L], rt + on_al[:, L:]
    u0, y0 = on_av[:, :L], on_av[:, L:] + av[:, L:]

    to2 = lambda a: a.reshape(tb, gw)
    al_out[...] = to2(al_p).astype(BF16)
    rt_out[...] = to2(rt_p).astype(BF16)
    u0_out[...] = to2(u0).astype(BF16)
    y0_out[...] = to2(y0).astype(BF16)
    bh_out[...] = to2(b * dec).astype(BF16)
    kh_out[...] = to2(k * dec).astype(BF16)
    pl_out[...] = jnp.exp(cum_l)


def _wkv_local(r, lw, k, v, kk, b, tb):
    t, c = r.shape
    gw = WKV_GROUP
    nc = tb // WKV_CHUNK
    spec = lambda: pl.BlockSpec((tb, gw), lambda i, g: (i, g))
    f32 = jax.ShapeDtypeStruct((t, c), F32)
    bf16 = jax.ShapeDtypeStruct((t, c), BF16)
    return pl.pallas_call(
        _wkv_local_kernel,
        grid=(t // tb, c // gw),
        in_specs=[spec() for _ in range(6)],
        out_specs=[spec() for _ in range(6)] + [pl.BlockSpec((nc, 1, gw), lambda i, g: (i, 0, g))],
        out_shape=[bf16, bf16, bf16, bf16, bf16, bf16,
                   jax.ShapeDtypeStruct((t // WKV_CHUNK, 1, c), F32)],
        compiler_params=_params("parallel", "parallel"),
    )(r, lw, k, v, kk, b)


def _wkv_state_kernel(al_ref, rt_ref, u0_ref, y0_ref, bh_ref, kh_ref, v_ref, pl_ref, y_ref, sv_ref):
    tb, c = v_ref.shape
    L = WKV_CHUNK
    gw = WKV_GROUP
    groups = range(c // gw)

    @pl.when(pl.program_id(0) == 0)
    def _():
        sv_ref[...] = jnp.zeros_like(sv_ref)

    bd_mask = (lax.broadcasted_iota(jnp.int32, (gw, gw), 0) // HEAD_DIM
               == lax.broadcasted_iota(jnp.int32, (gw, gw), 1) // HEAD_DIM)

    def chunk(ci, carry):
        rows = pl.ds(pl.multiple_of(ci * L, L), L)
        lanes = [slice(g * gw, (g + 1) * gw) for g in groups]
        svs = [sv_ref[g] for g in groups]
        uys = [_nt_dot(jnp.concatenate([al_ref[rows, lanes[g]], rt_ref[rows, lanes[g]]], axis=0), svs[g])
               for g in groups]
        us = [uys[g][:L] + u0_ref[rows, lanes[g]] for g in groups]
        for g in groups:
            y_ref[rows, lanes[g]] = (uys[g][L:] + y0_ref[rows, lanes[g]]).astype(BF16)
        upds = [_tn_dot(jnp.concatenate([us[g].astype(BF16), v_ref[rows, lanes[g]].astype(BF16)], axis=0),
                        jnp.concatenate([bh_ref[rows, lanes[g]], kh_ref[rows, lanes[g]]], axis=0))
                for g in groups]
        for g in groups:
            sv_ref[g] = svs[g] * pl_ref[ci, :, lanes[g]] + jnp.where(bd_mask, upds[g], 0.0)
        return carry

    lax.fori_loop(0, tb // L, chunk, 0)


def _wkv_state(al_p, rt_p, u0, y0, bh, kh, v, p_l, tb):
    t, c = v.shape
    nc = tb // WKV_CHUNK
    row = lambda: pl.BlockSpec((tb, c), lambda i: (i, 0))
    return pl.pallas_call(
        _wkv_state_kernel,
        grid=(t // tb,),
        in_specs=[row() for _ in range(7)] + [pl.BlockSpec((nc, 1, c), lambda i: (i, 0, 0))],
        out_specs=row(),
        out_shape=jax.ShapeDtypeStruct((t, c), BF16),
        scratch_shapes=[pltpu.VMEM((c // WKV_GROUP, WKV_GROUP, WKV_GROUP), F32)],
        compiler_params=_params("arbitrary"),
    )(al_p, rt_p, u0, y0, bh, kh, v, p_l)


def _wkv_scan(r, lw, k, v, kk, b, tb_local, tb_state):
    al_p, rt_p, u0, y0, bh, kh, p_l = _wkv_local(r, lw, k, v, kk, b, tb_local)
    return _wkv_state(al_p, rt_p, u0, y0, bh, kh, v, p_l, tb_state)


def _once(shape):
    n = len(shape)
    return pl.BlockSpec(shape, lambda *_: (0,) * n, pipeline_mode=pl.Buffered(1))


def _ffn_bounds(f):
    mid = min(f, -(-(f // MXU_DIM) // 2) * MXU_DIM) if f % MXU_DIM == 0 else f
    return tuple(b for b in ((0, mid), (mid, f)) if b[1] > b[0])


def _swiglu_ln(x, win_ref, wdown_ref, lng_ref, lnb_ref):
    xb = x.astype(BF16)
    f = wdown_ref.shape[0]
    acc = jnp.zeros(x.shape, F32)
    for lo, hi in _ffn_bounds(f):
        gate = jnp.dot(xb, win_ref[:, lo:hi], preferred_element_type=F32)
        up = jnp.dot(xb, win_ref[:, f + lo:f + hi], preferred_element_type=F32)
        act = (gate * jax.nn.sigmoid(gate) * up).astype(BF16)
        acc = acc + jnp.dot(act, wdown_ref[lo:hi, :], preferred_element_type=F32)
    return _layer_norm(ALPHA * x + acc, lng_ref[...], lnb_ref[...])


def _rwkv_out_ffn_kernel(y_ref, g_ref, bonus_ref, x_ref, gng_ref, gnb_ref, wo_ref, lng_ref, lnb_ref, hs_ref,
                         win_ref, wdown_ref, lng2_ref, lnb2_ref, o_ref):
    hs = hs_ref[...]
    y = y_ref[...]
    mean = _head_sum(y, hs) * (1.0 / HEAD_DIM)
    d = y - mean
    var = _head_sum(d * d, hs) * (1.0 / HEAD_DIM)
    yn = d * lax.rsqrt(var + GN_EPS) * gng_ref[...] + gnb_ref[...]
    z = (yn + bonus_ref[...]) * g_ref[...]
    h = _bdot(z, wo_ref[...])
    x1 = _layer_norm(ALPHA * x_ref[...] + h, lng_ref[...], lnb_ref[...])
    o_ref[...] = _swiglu_ln(x1, win_ref, wdown_ref, lng2_ref, lnb2_ref)


def _rwkv_out_ffn(y, g, bonus, x, gn_g, gn_b, w_o, ln_g, ln_b, hs, ffn, tm):
    t, c = x.shape
    row = lambda: pl.BlockSpec((tm, c), lambda i: (i, 0))
    consts = [gn_g, gn_b, w_o, ln_g, ln_b, hs, *ffn]
    return pl.pallas_call(
        _rwkv_out_ffn_kernel,
        grid=(t // tm,),
        in_specs=[row() for _ in range(4)] + [_once(a.shape) for a in consts],
        out_specs=row(),
        out_shape=jax.ShapeDtypeStruct((t, c), F32),
        compiler_params=_params("parallel"),
    )(y, g, bonus, x, *consts)


def _attn_out_ffn_kernel(at_ref, x_ref, wo_ref, lng_ref, lnb_ref, win_ref, wdown_ref, lng2_ref, lnb2_ref,
                         o_ref):
    h = _tn_dot(at_ref[...], wo_ref[...])
    x1 = _layer_norm(ALPHA * x_ref[...] + h, lng_ref[...], lnb_ref[...])
    o_ref[...] = _swiglu_ln(x1, win_ref, wdown_ref, lng2_ref, lnb2_ref)


def _attn_out_ffn(at, x, w_o, ln_g, ln_b, ffn, tm):
    t, c = x.shape
    consts = [w_o, ln_g, ln_b, *ffn]
    return pl.pallas_call(
        _attn_out_ffn_kernel,
        grid=(t // tm,),
        in_specs=[pl.BlockSpec((c, tm), lambda i: (0, i)),
                  pl.BlockSpec((tm, c), lambda i: (i, 0))] + [_once(a.shape) for a in consts],
        out_specs=pl.BlockSpec((tm, c), lambda i: (i, 0)),
        out_shape=jax.ShapeDtypeStruct((t, c), F32),
        compiler_params=_params("parallel"),
    )(at, x, *consts)


def _qkv_kernel(x_ref, wqt_ref, wk_ref, wvt_ref, cos_t_ref, sin_t_ref, cos_k_ref, sin_k_ref,
                qt_out, k_out, vt_out, kmean_out):
    xb = x_ref[...].astype(BF16)
    tm, c = xb.shape
    nh = c // HEAD_DIM
    half = HEAD_DIM // 2

    qt = _nt_dot(wqt_ref[...], xb)
    cos_t, sin_t = cos_t_ref[...], sin_t_ref[...]
    scale = HEAD_DIM ** -0.5 * LOG2_E
    for h in range(nh):
        q1 = qt[h * HEAD_DIM:h * HEAD_DIM + half]
        q2 = qt[h * HEAD_DIM + half:(h + 1) * HEAD_DIM]
        qt_out[h, 0:half, :] = ((q1 * cos_t - q2 * sin_t) * scale).astype(BF16)
        qt_out[h, half:HEAD_DIM, :] = ((q2 * cos_t + q1 * sin_t) * scale).astype(BF16)

    vt = _nt_dot(wvt_ref[...], xb)
    for h in range(nh):
        vt_out[h] = vt[h * HEAD_DIM:(h + 1) * HEAD_DIM].astype(BF16)

    k = jnp.dot(xb, wk_ref[...], preferred_element_type=F32)
    reps = c // cos_k_ref.shape[1]
    cos_k = jnp.concatenate([cos_k_ref[...]] * reps, axis=1)
    sin_k = jnp.concatenate([sin_k_ref[...]] * reps, axis=1)
    lane = lax.broadcasted_iota(jnp.int32, k.shape, 1)
    first = (lane % HEAD_DIM) < half
    rot = jnp.where(first, pltpu.roll(k, c - half, 1), pltpu.roll(k, half, 1))
    k = k * cos_k + rot * sin_k
    for h in range(nh):
        k_out[h] = k[:, h * HEAD_DIM:(h + 1) * HEAD_DIM].astype(BF16)
    for blk in range(tm // MOBA_BLOCK):
        kmean_out[blk] = jnp.mean(k[blk * MOBA_BLOCK:(blk + 1) * MOBA_BLOCK], axis=0, keepdims=True)


def _qkv_rope(x, wq_t, wk, wv_t, cos_t, sin_t, cos_k, sin_k, tm):
    t, c = x.shape
    nh = c // HEAD_DIM
    half = HEAD_DIM // 2
    nb = t // MOBA_BLOCK
    return pl.pallas_call(
        _qkv_kernel,
        grid=(t // tm,),
        in_specs=[pl.BlockSpec((tm, c), lambda i: (i, 0)),
                  _full(wq_t.shape), _full(wk.shape), _full(wv_t.shape),
                  pl.BlockSpec((half, tm), lambda i: (0, i)),
                  pl.BlockSpec((half, tm), lambda i: (0, i)),
                  pl.BlockSpec((tm, cos_k.shape[1]), lambda i: (i, 0)),
                  pl.BlockSpec((tm, sin_k.shape[1]), lambda i: (i, 0))],
        out_specs=[pl.BlockSpec((nh, HEAD_DIM, tm), lambda i: (0, 0, i)),
                   pl.BlockSpec((nh, tm, HEAD_DIM), lambda i: (0, i, 0)),
                   pl.BlockSpec((nh, HEAD_DIM, tm), lambda i: (0, 0, i)),
                   pl.BlockSpec((tm // MOBA_BLOCK, 1, c), lambda i: (i, 0, 0))],
        out_shape=[jax.ShapeDtypeStruct((nh, HEAD_DIM, t), BF16),
                   jax.ShapeDtypeStruct((nh, t, HEAD_DIM), BF16),
                   jax.ShapeDtypeStruct((nh, HEAD_DIM, t), BF16),
                   jax.ShapeDtypeStruct((nb, 1, c), F32)],
        compiler_params=_params("parallel"),
    )(x, wq_t, wk, wv_t, cos_t, sin_t, cos_k, sin_k)


def _moba_kernel(qt_ref, k_ref, vt_ref, kmean_ref, o_ref, sel_ref, s_ref):
    i = pl.program_id(1)
    hb, _, bq = qt_ref.shape
    nb = kmean_ref.shape[1]
    blk = MOBA_BLOCK

    heads = range(hb)
    ones = jnp.ones((16, blk), BF16)

    def select_blocks():
        bidx = lax.broadcasted_iota(jnp.int32, (nb, bq), 0)
        gates = []
        for h in heads:
            km_hi, km_lo = _split2(kmean_ref[h])
            gate = (jnp.dot(km_hi, qt_ref[h], preferred_element_type=F32)
                    + jnp.dot(km_lo, qt_ref[h], preferred_element_type=F32))
            gates.append(jnp.where(bidx < i, gate, -jnp.inf))
        sels = [jnp.zeros((nb, bq), F32) for _ in heads]
        for _ in range(MOBA_TOPK):
            ms = [jnp.max(g, axis=0, keepdims=True) for g in gates]
            firsts = [jnp.min(jnp.where(jnp.logical_and(gates[h] == ms[h], ms[h] > -jnp.inf), bidx, nb),
                              axis=0, keepdims=True) for h in heads]
            picks = [bidx == f for f in firsts]
            sels = [jnp.where(picks[h], 1.0, sels[h]) for h in heads]
            gates = [jnp.where(picks[h], -jnp.inf, gates[h]) for h in heads]
        for h in heads:
            sel_ref[h] = sels[h]

    def put_scores(j, slot):
        koff = pl.multiple_of(j * blk, blk)
        for h in heads:
            s_ref[slot, h] = jnp.dot(k_ref[h, pl.ds(koff, blk), :], qt_ref[h],
                                     preferred_element_type=F32)

    def attend(j, slot, carry, causal):
        koff = pl.multiple_of(j * blk, blk)
        ss = [s_ref[slot, h] for h in heads]
        if causal:
            key = lax.broadcasted_iota(jnp.int32, (blk, bq), 0)
            qry = lax.broadcasted_iota(jnp.int32, (blk, bq), 1)
            ss = [jnp.where(key <= qry, s, -jnp.inf) for s in ss]
        else:
            ons = [sel_ref[h, pl.ds(j, 1), :] > 0.0 for h in heads]
        cmaxs = [jnp.max(s, axis=0, keepdims=True) for s in ss]
        ms = [carry[h][0] for h in heads]
        if causal:
            m_news = [jnp.maximum(ms[h], cmaxs[h]) for h in heads]
            m_ps = m_news
        else:
            m_news = [jnp.where(ons[h], jnp.maximum(ms[h], cmaxs[h]), ms[h]) for h in heads]
            m_ps = [jnp.where(ons[h], m_news[h], cmaxs[h]) for h in heads]
        ps = [jnp.exp2(ss[h] - m_ps[h]).astype(BF16) for h in heads]
        os = [jnp.dot(jnp.concatenate([vt_ref[h, :, pl.ds(koff, blk)], ones], axis=0), ps[h],
                      preferred_element_type=F32) for h in heads]
        if not causal:
            os = [jnp.where(ons[h], os[h], 0.0) for h in heads]
        return tuple((m_news[h], carry[h][1] * jnp.exp2(ms[h] - m_news[h]) + os[h]) for h in heads)

    init = tuple((jnp.full((1, bq), NEG_BIG, F32), jnp.zeros((HEAD_DIM + 16, bq), F32)) for _ in heads)

    def blocks(j0, count, carry):
        for d in range(count):
            put_scores(j0 + d + 1, (d + 1) % 2)
            carry = attend(j0 + d, d % 2, carry, False)
        return carry

    def finish(carry):
        for h in heads:
            acc = carry[h][1]
            o_ref[h] = acc[:HEAD_DIM] / acc[HEAD_DIM:HEAD_DIM + 1]

    put_scores(0, 0)
    select_blocks()
    carry, done = init, 0
    for u in MOBA_BLOCKS_PER_ITER:
        trips = (i - done) // u
        carry = lax.fori_loop(0, trips, lambda jj, c, u=u, done=done: blocks(done + u * jj, u, c), carry)
        done = done + trips * u

    @pl.when(i % 2 == 0)
    def _():
        finish(attend(i, 0, carry, True))

    @pl.when(i % 2 == 1)
    def _():
        put_scores(i, 1)
        finish(attend(i, 1, attend(i - 1, 0, carry, False), True))


def _moba_attention(qt, k, vt, kmean, hb):
    nh, dh, t = qt.shape
    nb = t // MOBA_BLOCK
    once = pl.Buffered(1)
    return pl.pallas_call(
        _moba_kernel,
        grid=(nh // hb, nb),
        in_specs=[pl.BlockSpec((hb, dh, MOBA_BLOCK), lambda g, i: (g, 0, i)),
                  pl.BlockSpec((hb, t, dh), lambda g, i: (g, 0, 0), pipeline_mode=once),
                  pl.BlockSpec((hb, dh, t), lambda g, i: (g, 0, 0), pipeline_mode=once),
                  pl.BlockSpec((hb, nb, dh), lambda g, i: (g, 0, 0))],
        out_specs=pl.BlockSpec((hb, dh, MOBA_BLOCK), lambda g, i: (g, 0, i)),
        out_shape=jax.ShapeDtypeStruct((nh, dh, t), F32),
        scratch_shapes=[pltpu.VMEM((hb, nb, MOBA_BLOCK), F32),
                        pltpu.VMEM((2, hb, MOBA_BLOCK, MOBA_BLOCK), F32)],
        compiler_params=_params("parallel", "arbitrary"),
    )(qt, k, vt, kmean)


def _row(a):
    return a.reshape(1, -1).astype(F32)


def _rwkv_layer(x, p, ln_g, ln_b, hs, ffn, tm):
    r, lw, k, v, kk, b, g, bonus = _rwkv_proj(x, p, hs, tm)
    t = x.shape[0]
    y = _wkv_scan(r, lw, k, v, kk, b, tb_local=min(WKV_LOCAL_TILE, t), tb_state=min(WKV_STATE_TILE, t))
    return _rwkv_out_ffn(y, g, bonus, x, p['gn_g'], p['gn_b'], p['w_o'], ln_g, ln_b, hs, ffn, tm)


def _rope_tables(t):
    inv = ROPE_THETA ** (-jnp.arange(0, HEAD_DIM, 2, dtype=F32) / HEAD_DIM)
    ang = jnp.arange(t, dtype=F32)[:, None] * inv[None, :]
    cos, sin = jnp.cos(ang), jnp.sin(ang)
    reps = LANES // HEAD_DIM
    cos_k = jnp.tile(jnp.concatenate([cos, cos], axis=1), (1, reps))
    sin_k = jnp.tile(jnp.concatenate([-sin, sin], axis=1), (1, reps))
    return cos.T, sin.T, cos_k, sin_k


def _moba_layer(x, w_qkv, w_o, ln_g, ln_b, ffn, tm):
    t, c = x.shape
    nh = c // HEAD_DIM
    wq_t = w_qkv[:, :c].T.astype(BF16)
    wk = w_qkv[:, c:2 * c].astype(BF16)
    wv_t = w_qkv[:, 2 * c:].T.astype(BF16)
    cos_t, sin_t, cos_k, sin_k = _rope_tables(t)
    qt, k, vt, kmean = _qkv_rope(x, wq_t, wk, wv_t, cos_t, sin_t, cos_k, sin_k, tm)
    kmean = jnp.transpose(kmean.reshape(t // MOBA_BLOCK, nh, HEAD_DIM), (1, 0, 2))
    at = _moba_attention(qt, k, vt, kmean, hb=min(MOBA_HEADS_PER_STEP, nh))
    return _attn_out_ffn(at.reshape(c, t), x, w_o.astype(BF16), ln_g, ln_b, ffn, tm)


def kernel(x, rwkv_mu, rwkv_w_rkv, rwkv_w0, rwkv_w1, rwkv_w2, rwkv_a0, rwkv_a1, rwkv_a2, rwkv_g1, rwkv_g2,
           rwkv_k_k, rwkv_k_a, rwkv_r_k, rwkv_gn_g, rwkv_gn_b, rwkv_w_o, moba_w_qkv, moba_w_o,
           ffn_w_in, ffn_w_down, ln_mix_g, ln_mix_b, ln_ffn_g, ln_ffn_b):
    bsz, t, c = x.shape
    assert c % WKV_GROUP == 0 and t % MOBA_BLOCK == 0
    assert ffn_w_in.shape[0] == DEPTH, "ALPHA is the DeepNorm constant of a DEPTH-layer trunk"
    tm = min(ROW_TILE, t)
    hs = _head_ones()
    outs = []
    for bi in range(bsz):
        h = x[bi]
        for i in range(DEPTH):
            j = i // 2
            ffn = (ffn_w_in[i].astype(BF16), ffn_w_down[i].astype(BF16), _row(ln_ffn_g[i]), _row(ln_ffn_b[i]))
            if i % 2 == 0:
                p = dict(mu=rwkv_mu[j], w_rkv=rwkv_w_rkv[j].astype(BF16), w0=_row(rwkv_w0[j]),
                         w1=rwkv_w1[j].astype(BF16), w2=rwkv_w2[j].astype(BF16), a0=_row(rwkv_a0[j]),
                         a1=rwkv_a1[j].astype(BF16), a2=rwkv_a2[j].astype(BF16),
                         g1=rwkv_g1[j].astype(BF16), g2=rwkv_g2[j].astype(BF16),
                         k_k=_row(rwkv_k_k[j]), k_a=_row(rwkv_k_a[j]), r_k=_row(rwkv_r_k[j]),
                         gn_g=_row(rwkv_gn_g[j]), gn_b=_row(rwkv_gn_b[j]), w_o=rwkv_w_o[j].astype(BF16))
                h = _rwkv_layer(h, p, _row(ln_mix_g[i]), _row(ln_mix_b[i]), hs, ffn, tm)
            else:
                h = _moba_layer(h, moba_w_qkv[j], moba_w_o[j], _row(ln_mix_g[i]), _row(ln_mix_b[i]), ffn, tm)
        outs.append(h)
    return jnp.stack(outs, axis=0)
```

```python
import jax
import jax.numpy as jnp
from jax import lax
from jax.experimental import pallas as pl
from jax.experimental.pallas import tpu as pltpu

F32 = jnp.float32
BF16 = jnp.bfloat16

HEAD_DIM = 64
DEPTH = 2
ALPHA = (2 * DEPTH) ** 0.25
LN_EPS = 1e-5
GN_EPS = HEAD_DIM * 1e-5
MOBA_BLOCK = 256
MOBA_TOPK = 3
ROPE_THETA = 10000.0
LOG2_E = 1.4426950408889634
MOBA_HEADS_PER_STEP = 4
MOBA_BLOCKS_PER_ITER = (16, 4, 2)

LANES = 128
MXU_DIM = 256
WKV_CHUNK = 64
WKV_GROUP = MXU_DIM
WKV_LOCAL_TILE = 1024
WKV_STATE_TILE = 256
ROW_TILE = 512
NEG_BIG = -1e30
VMEM_LIMIT = 56 * 1024 * 1024


def _params(*sem):
    return pltpu.CompilerParams(dimension_semantics=sem, vmem_limit_bytes=VMEM_LIMIT)


def _full(shape):
    n = len(shape)
    return pl.BlockSpec(shape, lambda *_: (0,) * n)


def _bdot(a, b):
    return jnp.dot(a.astype(BF16), b.astype(BF16), preferred_element_type=F32)


def _nt_dot(a, b):
    return lax.dot_general(a.astype(BF16), b.astype(BF16), (((1,), (1,)), ((), ())),
                           preferred_element_type=F32)


def _tn_dot(a, b):
    return lax.dot_general(a.astype(BF16), b.astype(BF16), (((0,), (0,)), ((), ())),
                           preferred_element_type=F32)


def _split2(a):
    hi = a.astype(BF16)
    lo = (a - hi.astype(F32)).astype(BF16)
    return hi, lo


def _dot_exact_rhs(a, b_exact):
    if a.dtype == BF16:
        return jnp.dot(a, b_exact, preferred_element_type=F32)
    hi, lo = _split2(a)
    return (jnp.dot(hi, b_exact, preferred_element_type=F32)
            + jnp.dot(lo, b_exact, preferred_element_type=F32))


def _head_sum(a, ones_bd):
    w = ones_bd.shape[0]
    return jnp.concatenate([_dot_exact_rhs(a[:, s:s + w], ones_bd) for s in range(0, a.shape[1], w)],
                           axis=1)


def _layer_norm(z, g, b):
    mu = jnp.mean(z, axis=-1, keepdims=True)
    d = z - mu
    var = jnp.mean(d * d, axis=-1, keepdims=True)
    return d * lax.rsqrt(var + LN_EPS) * g + b


def _head_ones():
    h = jnp.arange(MXU_DIM) // HEAD_DIM
    return (h[:, None] == h[None, :]).astype(BF16)


def _rwkv_proj_kernel(x_ref, xp_ref, mu_ref, wrkv_ref, w0_ref, w1_ref, w2_ref, a0_ref, a1_ref, a2_ref,
                      g1_ref, g2_ref, kk_w_ref, ka_w_ref, rk_w_ref, hs_ref,
                      r_out, lw_out, k_out, v_out, kk_out, b_out, g_out, bonus_out):
    i = pl.program_id(0)
    x = x_ref[...]
    prev_last = jnp.where(i > 0, xp_ref[7:8, :], 0.0)
    row = lax.broadcasted_iota(jnp.int32, x.shape, 0)
    shifted = jnp.where(row == 0, prev_last, pltpu.roll(x, 1, 0))
    xx = shifted - x
    mix = lambda n: x + xx * mu_ref[n:n + 1, :]
    hs = hs_ref[...]

    r = _bdot(mix(0), wrkv_ref[0])
    k = _bdot(mix(2), wrkv_ref[1])
    v = _bdot(mix(3), wrkv_ref[2])
    zw = w0_ref[...] + _bdot(jnp.tanh(_bdot(mix(1), w1_ref[...])), w2_ref[...])
    u = -zw
    w_log = -(jnp.maximum(u, 0.0) + jnp.log(1.0 + jnp.exp(-jnp.abs(u)))) - 0.5
    a = jax.nn.sigmoid(a0_ref[...] + _bdot(_bdot(mix(4), a1_ref[...]), a2_ref[...]))
    g = _bdot(jax.nn.sigmoid(_bdot(mix(5), g1_ref[...])), g2_ref[...])

    kk = k * kk_w_ref[...]
    norm = jnp.sqrt(_head_sum(kk * kk, hs))
    kk = kk / jnp.maximum(norm, 1e-12)
    k = k * (1.0 + (a - 1.0) * ka_w_ref[...])

    r_out[...] = r.astype(BF16)
    lw_out[...] = -jnp.exp(w_log)
    k_out[...] = k.astype(BF16)
    v_out[...] = v.astype(BF16)
    kk_out[...] = kk.astype(BF16)
    b_out[...] = (kk * a).astype(BF16)
    g_out[...] = g.astype(BF16)
    bonus_out[...] = (_head_sum(r * k * rk_w_ref[...], hs) * v).astype(BF16)


def _rwkv_proj(x, p, hs, tm):
    t, c = x.shape
    row = lambda: pl.BlockSpec((tm, c), lambda i: (i, 0))
    prev = pl.BlockSpec((8, c), lambda i: (jnp.maximum(i * (tm // 8) - 1, 0), 0))
    ins = [x, x, p['mu'], p['w_rkv'], p['w0'], p['w1'], p['w2'], p['a0'], p['a1'], p['a2'],
           p['g1'], p['g2'], p['k_k'], p['k_a'], p['r_k'], hs]
    in_specs = [row(), prev] + [_full(a.shape) for a in ins[2:]]
    f32 = jax.ShapeDtypeStruct((t, c), F32)
    bf16 = jax.ShapeDtypeStruct((t, c), BF16)
    return pl.pallas_call(
        _rwkv_proj_kernel,
        grid=(t // tm,),
        in_specs=in_specs,
        out_specs=[row() for _ in range(8)],
        out_shape=[bf16, f32, bf16, bf16, bf16, bf16, bf16, bf16],
        compiler_params=_params("parallel"),
    )(*ins)


def _bmm(a, b):
    return lax.dot_general(a.astype(BF16), b.astype(BF16), (((2,), (1,)), ((0,), (0,))),
                           preferred_element_type=F32)


def _wkv_local_kernel(r_ref, lw_ref, k_ref, v_ref, kk_ref, b_ref,
                      al_out, rt_out, u0_out, y0_out, bh_out, kh_out, pl_out):
    tb, gw = r_ref.shape
    L = WKV_CHUNK
    nc = tb // L
    nh = gw // HEAD_DIM
    n = nh * L

    lane = lax.broadcasted_iota(jnp.int32, (1, 1, gw), 2)
    head_masks = [lane // HEAD_DIM == h for h in range(nh)]
    col = lax.broadcasted_iota(jnp.int32, (1, 1, n), 2)
    col_masks = [col // L == h for h in range(nh)]
    ri = lax.broadcasted_iota(jnp.int32, (1, L, n), 1)
    ci = lax.broadcasted_iota(jnp.int32, (1, L, n), 2) % L
    strict = ci < ri
    incl = ci <= ri
    eye_cat = (ci == ri).astype(F32)
    tt = min(tb, MXU_DIM)
    rb = lax.broadcasted_iota(jnp.int32, (tt, tt), 0)
    cb = lax.broadcasted_iota(jnp.int32, (tt, tt), 1)
    tri = jnp.logical_and(cb <= rb, cb // L == rb // L).astype(BF16)

    def stack(a, masks):
        a16 = a.astype(BF16)
        return jnp.concatenate([jnp.where(m, a16, 0) for m in masks], axis=1)

    lw2 = lw_ref[...]
    hi, lo = _split2(lw2)
    cum2 = jnp.concatenate(
        [jnp.dot(tri, hi[s:s + tt], preferred_element_type=F32)
         + jnp.dot(tri, lo[s:s + tt], preferred_element_type=F32) for s in range(0, tb, tt)], axis=0)
    to3 = lambda a: a.reshape(nc, L, gw)
    cum, lw = to3(cum2), to3(lw2)
    r, k, v, kk, b = (to3(ref[...]) for ref in (r_ref, k_ref, v_ref, kk_ref, b_ref))
    cum_l = cum[:, L - 1:L, :]
    p_inv = jnp.exp(-cum)
    al = -(kk * jnp.exp(cum - lw))
    bt = b * p_inv
    kt = k * p_inv
    rt = r * jnp.exp(cum)
    dec = jnp.exp(cum_l - cum)

    xs = jnp.concatenate([al, rt], axis=1)
    ys = jnp.concatenate([stack(bt, head_masks), stack(kt, head_masks)], axis=1)
    res = lax.dot_general(xs.astype(BF16), ys.astype(BF16), (((2,), (2,)), ((0,), (0,))),
                          preferred_element_type=F32)
    a_ab = jnp.where(strict, res[:, :L, :n], 0.0)
    a_ak = jnp.where(strict, res[:, :L, n:], 0.0)
    a_rb = jnp.where(incl, res[:, L:, :n], 0.0)
    a_rk = jnp.where(incl, res[:, L:, n:], 0.0)

    tinv = eye_cat + jnp.where(jnp.logical_and(ri // 2 == ci // 2, ci < ri), a_ab, 0.0)
    s = 4
    while s <= L:
        quad = jnp.logical_and(ri // s == ci // s, jnp.logical_and(ri % s >= s // 2, ci % s < s // 2))
        x = _bmm(jnp.where(quad, a_ab, 0.0), stack(tinv, col_masks))
        tinv = tinv + _bmm(tinv, stack(x, col_masks))
        s *= 2

    av = _bmm(jnp.concatenate([a_ak, a_rk], axis=1), stack(v, head_masks))
    tg = jnp.concatenate([tinv, _bmm(a_rb, stack(tinv, col_masks))], axis=1)
    on_al = _bmm(tg, stack(al, head_masks))
    on_av = _bmm(tg, stack(av[:, :L], head_masks))
    al_p, rt_p = on_al[:, :L], rt + on_al[:, L:]
    u0, y0 = on_av[:, :L], on_av[:, L:] + av[:, L:]

    to2 = lambda a: a.reshape(tb, gw)
    al_out[...] = to2(al_p).astype(BF16)
    rt_out[...] = to2(rt_p).astype(BF16)
    u0_out[...] = to2(u0).astype(BF16)
    y0_out[...] = to2(y0).astype(BF16)
    bh_out[...] = to2(b * dec).astype(BF16)
    kh_out[...] = to2(k * dec).astype(BF16)
    pl_out[...] = jnp.exp(cum_l)


def _wkv_local(r, lw, k, v, kk, b, tb):
    t, c = r.shape
    gw = WKV_GROUP
    nc = tb // WKV_CHUNK
    spec = lambda: pl.BlockSpec((tb, gw), lambda i, g: (i, g))
    f32 = jax.ShapeDtypeStruct((t, c), F32)
    bf16 = jax.ShapeDtypeStruct((t, c), BF16)
    return pl.pallas_call(
        _wkv_local_kernel,
        grid=(t // tb, c // gw),
        in_specs=[spec() for _ in range(6)],
        out_specs=[spec() for _ in range(6)] + [pl.BlockSpec((nc, 1, gw), lambda i, g: (i, 0, g))],
        out_shape=[bf16, bf16, bf16, bf16, bf16, bf16,
                   jax.ShapeDtypeStruct((t // WKV_CHUNK, 1, c), F32)],
        compiler_params=_params("parallel", "parallel"),
    )(r, lw, k, v, kk, b)


def _wkv_state_kernel(al_ref, rt_ref, u0_ref, y0_ref, bh_ref, kh_ref, v_ref, pl_ref, y_ref, sv_ref):
    tb, c = v_ref.shape
    L = WKV_CHUNK
    gw = WKV_GROUP
    groups = range(c // gw)

    @pl.when(pl.program_id(0) == 0)
    def _():
        sv_ref[...] = jnp.zeros_like(sv_ref)

    bd_mask = (lax.broadcasted_iota(jnp.int32, (gw, gw), 0) // HEAD_DIM
               == lax.broadcasted_iota(jnp.int32, (gw, gw), 1) // HEAD_DIM)

    def chunk(ci, carry):
        rows = pl.ds(pl.multiple_of(ci * L, L), L)
        lanes = [slice(g * gw, (g + 1) * gw) for g in groups]
        svs = [sv_ref[g] for g in groups]
        uys = [_nt_dot(jnp.concatenate([al_ref[rows, lanes[g]], rt_ref[rows, lanes[g]]], axis=0), svs[g])
               for g in groups]
        us = [uys[g][:L] + u0_ref[rows, lanes[g]] for g in groups]
        for g in groups:
            y_ref[rows, lanes[g]] = (uys[g][L:] + y0_ref[rows, lanes[g]]).astype(BF16)
        upds = [_tn_dot(jnp.concatenate([us[g].astype(BF16), v_ref[rows, lanes[g]].astype(BF16)], axis=0),
                        jnp.concatenate([bh_ref[rows, lanes[g]], kh_ref[rows, lanes[g]]], axis=0))
                for g in groups]
        for g in groups:
            sv_ref[g] = svs[g] * pl_ref[ci, :, lanes[g]] + jnp.where(bd_mask, upds[g], 0.0)
        return carry

    lax.fori_loop(0, tb // L, chunk, 0)


def _wkv_state(al_p, rt_p, u0, y0, bh, kh, v, p_l, tb):
    t, c = v.shape
    nc = tb // WKV_CHUNK
    row = lambda: pl.BlockSpec((tb, c), lambda i: (i, 0))
    return pl.pallas_call(
        _wkv_state_kernel,
        grid=(t // tb,),
        in_specs=[row() for _ in range(7)] + [pl.BlockSpec((nc, 1, c), lambda i: (i, 0, 0))],
        out_specs=row(),
        out_shape=jax.ShapeDtypeStruct((t, c), BF16),
        scratch_shapes=[pltpu.VMEM((c // WKV_GROUP, WKV_GROUP, WKV_GROUP), F32)],
        compiler_params=_params("arbitrary"),
    )(al_p, rt_p, u0, y0, bh, kh, v, p_l)


def _wkv_scan(r, lw, k, v, kk, b, tb_local, tb_state):
    al_p, rt_p, u0, y0, bh, kh, p_l = _wkv_local(r, lw, k, v, kk, b, tb_local)
    return _wkv_state(al_p, rt_p, u0, y0, bh, kh, v, p_l, tb_state)


def _once(shape):
    n = len(shape)
    return pl.BlockSpec(shape, lambda *_: (0,) * n, pipeline_mode=pl.Buffered(1))


def _ffn_bounds(f):
    return ((0, f),)


def _swiglu_ln(x, win_ref, wdown_ref, lng_ref, lnb_ref):
    xb = x.astype(BF16)
    f = wdown_ref.shape[0]
    acc = jnp.zeros(x.shape, F32)
    for lo, hi in _ffn_bounds(f):
        gate = jnp.dot(xb, win_ref[:, lo:hi], preferred_element_type=F32)
        up = jnp.dot(xb, win_ref[:, f + lo:f + hi], preferred_element_type=F32)
        act = (gate * jax.nn.sigmoid(gate) * up).astype(BF16)
        acc = acc + jnp.dot(act, wdown_ref[lo:hi, :], preferred_element_type=F32)
    return _layer_norm(ALPHA * x + acc, lng_ref[...], lnb_ref[...])


def _rwkv_out_ffn_kernel(y_ref, g_ref, bonus_ref, x_ref, gng_ref, gnb_ref, wo_ref, lng_ref, lnb_ref, hs_ref,
                         win_ref, wdown_ref, lng2_ref, lnb2_ref, o_ref):
    hs = hs_ref[...]
    y = y_ref[...]
    mean = _head_sum(y, hs) * (1.0 / HEAD_DIM)
    d = y - mean
    var = _head_sum(d * d, hs) * (1.0 / HEAD_DIM)
    yn = d * lax.rsqrt(var + GN_EPS) * gng_ref[...] + gnb_ref[...]
    z = (yn + bonus_ref[...]) * g_ref[...]
    h = _bdot(z, wo_ref[...])
    x1 = _layer_norm(ALPHA * x_ref[...] + h, lng_ref[...], lnb_ref[...])
    o_ref[...] = _swiglu_ln(x1, win_ref, wdown_ref, lng2_ref, lnb2_ref)


def _rwkv_out_ffn(y, g, bonus, x, gn_g, gn_b, w_o, ln_g, ln_b, hs, ffn, tm):
    t, c = x.shape
    row = lambda: pl.BlockSpec((tm, c), lambda i: (i, 0))
    consts = [gn_g, gn_b, w_o, ln_g, ln_b, hs, *ffn]
    return pl.pallas_call(
        _rwkv_out_ffn_kernel,
        grid=(t // tm,),
        in_specs=[row() for _ in range(4)] + [_once(a.shape) for a in consts],
        out_specs=row(),
        out_shape=jax.ShapeDtypeStruct((t, c), F32),
        compiler_params=_params("parallel"),
    )(y, g, bonus, x, *consts)


def _attn_out_ffn_kernel(at_ref, x_ref, wo_ref, lng_ref, lnb_ref, win_ref, wdown_ref, lng2_ref, lnb2_ref,
                         o_ref):
    h = _tn_dot(at_ref[...], wo_ref[...])
    x1 = _layer_norm(ALPHA * x_ref[...] + h, lng_ref[...], lnb_ref[...])
    o_ref[...] = _swiglu_ln(x1, win_ref, wdown_ref, lng2_ref, lnb2_ref)


def _attn_out_ffn(at, x, w_o, ln_g, ln_b, ffn, tm):
    t, c = x.shape
    consts = [w_o, ln_g, ln_b, *ffn]
    return pl.pallas_call(
        _attn_out_ffn_kernel,
        grid=(t // tm,),
        in_specs=[pl.BlockSpec((c, tm), lambda i: (0, i)),
                  pl.BlockSpec((tm, c), lambda i: (i, 0))] + [_once(a.shape) for a in consts],
        out_specs=pl.BlockSpec((tm, c), lambda i: (i, 0)),
        out_shape=jax.ShapeDtypeStruct((t, c), F32),
        compiler_params=_params("parallel"),
    )(at, x, *consts)


def _qkv_kernel(x_ref, wqt_ref, wk_ref, wvt_ref, cos_t_ref, sin_t_ref, cos_k_ref, sin_k_ref,
                qt_out, k_out, vt_out, kmean_out):
    xb = x_ref[...].astype(BF16)
    tm, c = xb.shape
    nh = c // HEAD_DIM
    half = HEAD_DIM // 2

    qt = _nt_dot(wqt_ref[...], xb)
    cos_t, sin_t = cos_t_ref[...], sin_t_ref[...]
    scale = HEAD_DIM ** -0.5 * LOG2_E
    for h in range(nh):
        q1 = qt[h * HEAD_DIM:h * HEAD_DIM + half]
        q2 = qt[h * HEAD_DIM + half:(h + 1) * HEAD_DIM]
        qt_out[h, 0:half, :] = ((q1 * cos_t - q2 * sin_t) * scale).astype(BF16)
        qt_out[h, half:HEAD_DIM, :] = ((q2 * cos_t + q1 * sin_t) * scale).astype(BF16)

    vt = _nt_dot(wvt_ref[...], xb)
    for h in range(nh):
        vt_out[h] = vt[h * HEAD_DIM:(h + 1) * HEAD_DIM].astype(BF16)

    k = jnp.dot(xb, wk_ref[...], preferred_element_type=F32)
    reps = c // cos_k_ref.shape[1]
    cos_k = jnp.concatenate([cos_k_ref[...]] * reps, axis=1)
    sin_k = jnp.concatenate([sin_k_ref[...]] * reps, axis=1)
    lane = lax.broadcasted_iota(jnp.int32, k.shape, 1)
    first = (lane % HEAD_DIM) < half
    rot = jnp.where(first, pltpu.roll(k, c - half, 1), pltpu.roll(k, half, 1))
    k = k * cos_k + rot * sin_k
    for h in range(nh):
        k_out[h] = k[:, h * HEAD_DIM:(h + 1) * HEAD_DIM].astype(BF16)
    for blk in range(tm // MOBA_BLOCK):
        kmean_out[blk] = jnp.mean(k[blk * MOBA_BLOCK:(blk + 1) * MOBA_BLOCK], axis=0, keepdims=True)


def _qkv_rope(x, wq_t, wk, wv_t, cos_t, sin_t, cos_k, sin_k, tm):
    t, c = x.shape
    nh = c // HEAD_DIM
    half = HEAD_DIM // 2
    nb = t // MOBA_BLOCK
    return pl.pallas_call(
        _qkv_kernel,
        grid=(t // tm,),
        in_specs=[pl.BlockSpec((tm, c), lambda i: (i, 0)),
                  _full(wq_t.shape), _full(wk.shape), _full(wv_t.shape),
                  pl.BlockSpec((half, tm), lambda i: (0, i)),
                  pl.BlockSpec((half, tm), lambda i: (0, i)),
                  pl.BlockSpec((tm, cos_k.shape[1]), lambda i: (i, 0)),
                  pl.BlockSpec((tm, sin_k.shape[1]), lambda i: (i, 0))],
        out_specs=[pl.BlockSpec((nh, HEAD_DIM, tm), lambda i: (0, 0, i)),
                   pl.BlockSpec((nh, tm, HEAD_DIM), lambda i: (0, i, 0)),
                   pl.BlockSpec((nh, HEAD_DIM, tm), lambda i: (0, 0, i)),
                   pl.BlockSpec((tm // MOBA_BLOCK, 1, c), lambda i: (i, 0, 0))],
        out_shape=[jax.ShapeDtypeStruct((nh, HEAD_DIM, t), BF16),
                   jax.ShapeDtypeStruct((nh, t, HEAD_DIM), BF16),
                   jax.ShapeDtypeStruct((nh, HEAD_DIM, t), BF16),
                   jax.ShapeDtypeStruct((nb, 1, c), F32)],
        compiler_params=_params("parallel"),
    )(x, wq_t, wk, wv_t, cos_t, sin_t, cos_k, sin_k)


def _moba_kernel(qt_ref, k_ref, vt_ref, kmean_ref, o_ref, sel_ref, s_ref):
    i = pl.program_id(1)
    hb, _, bq = qt_ref.shape
    nb = kmean_ref.shape[1]
    blk = MOBA_BLOCK

    qts = [qt_ref[h] for h in range(hb)]
    heads = range(hb)
    ones = jnp.ones((16, blk), BF16)

    def select_blocks():
        bidx = lax.broadcasted_iota(jnp.int32, (nb, bq), 0)
        gates = []
        for h in heads:
            km_hi, km_lo = _split2(kmean_ref[h])
            gate = (jnp.dot(km_hi, qts[h], preferred_element_type=F32)
                    + jnp.dot(km_lo, qts[h], preferred_element_type=F32))
            gates.append(jnp.where(bidx < i, gate, -jnp.inf))
        sels = [jnp.zeros((nb, bq), F32) for _ in heads]
        for _ in range(MOBA_TOPK):
            ms = [jnp.max(g, axis=0, keepdims=True) for g in gates]
            firsts = [jnp.min(jnp.where(jnp.logical_and(gates[h] == ms[h], ms[h] > -jnp.inf), bidx, nb),
                              axis=0, keepdims=True) for h in heads]
            picks = [bidx == f for f in firsts]
            sels = [jnp.where(picks[h], 1.0, sels[h]) for h in heads]
            gates = [jnp.where(picks[h], -jnp.inf, gates[h]) for h in heads]
        for h in heads:
            sel_ref[h] = sels[h]

    def put_scores(j, slot):
        koff = pl.multiple_of(j * blk, blk)
        for h in heads:
            s_ref[slot, h] = jnp.dot(k_ref[h, pl.ds(koff, blk), :], qts[h],
                                     preferred_element_type=F32)

    def attend(j, slot, carry, causal):
        koff = pl.multiple_of(j * blk, blk)
        ss = [s_ref[slot, h] for h in heads]
        if causal:
            key = lax.broadcasted_iota(jnp.int32, (blk, bq), 0)
            qry = lax.broadcasted_iota(jnp.int32, (blk, bq), 1)
            ss = [jnp.where(key <= qry, s, -jnp.inf) for s in ss]
        else:
            ons = [sel_ref[h, pl.ds(j, 1), :] > 0.0 for h in heads]
        cmaxs = [jnp.max(s, axis=0, keepdims=True) for s in ss]
        ms = [carry[h][0] for h in heads]
        if causal:
            m_news = [jnp.maximum(ms[h], cmaxs[h]) for h in heads]
            m_ps = m_news
        else:
            m_news = [jnp.where(ons[h], jnp.maximum(ms[h], cmaxs[h]), ms[h]) for h in heads]
            m_ps = [jnp.where(ons[h], m_news[h], cmaxs[h]) for h in heads]
        ps = [jnp.exp2(ss[h] - m_ps[h]).astype(BF16) for h in heads]
        os = [jnp.dot(jnp.concatenate([vt_ref[h, :, pl.ds(koff, blk)], ones], axis=0), ps[h],
                      preferred_element_type=F32) for h in heads]
        if not causal:
            os = [jnp.where(ons[h], os[h], 0.0) for h in heads]
        return tuple((m_news[h], carry[h][1] * jnp.exp2(ms[h] - m_news[h]) + os[h]) for h in heads)

    init = tuple((jnp.full((1, bq), NEG_BIG, F32), jnp.zeros((HEAD_DIM + 16, bq), F32)) for _ in heads)

    def blocks(j0, count, carry):
        for d in range(count):
            put_scores(j0 + d + 1, (d + 1) % 2)
            carry = attend(j0 + d, d % 2, carry, False)
        return carry

    def finish(carry):
        for h in heads:
            acc = carry[h][1]
            o_ref[h] = acc[:HEAD_DIM] / acc[HEAD_DIM:HEAD_DIM + 1]

    put_scores(0, 0)
    select_blocks()
    carry, done = init, 0
    for u in MOBA_BLOCKS_PER_ITER:
        trips = (i - done) // u
        carry = lax.fori_loop(0, trips, lambda jj, c, u=u, done=done: blocks(done + u * jj, u, c), carry)
        done = done + trips * u

    @pl.when(i % 2 == 0)
    def _():
        finish(attend(i, 0, carry, True))

    @pl.when(i % 2 == 1)
    def _():
        put_scores(i, 1)
        finish(attend(i, 1, attend(i - 1, 0, carry, False), True))


def _moba_attention(qt, k, vt, kmean, hb):
    nh, dh, t = qt.shape
    nb = t // MOBA_BLOCK
    once = pl.Buffered(1)
    return pl.pallas_call(
        _moba_kernel,
        grid=(nh // hb, nb),
        in_specs=[pl.BlockSpec((hb, dh, MOBA_BLOCK), lambda g, i: (g, 0, i)),
                  pl.BlockSpec((hb, t, dh), lambda g, i: (g, 0, 0), pipeline_mode=once),
                  pl.BlockSpec((hb, dh, t), lambda g, i: (g, 0, 0), pipeline_mode=once),
                  pl.BlockSpec((hb, nb, dh), lambda g, i: (g, 0, 0))],
        out_specs=pl.BlockSpec((hb, dh, MOBA_BLOCK), lambda g, i: (g, 0, i)),
        out_shape=jax.ShapeDtypeStruct((nh, dh, t), F32),
        scratch_shapes=[pltpu.VMEM((hb, nb, MOBA_BLOCK), F32),
                        pltpu.VMEM((2, hb, MOBA_BLOCK, MOBA_BLOCK), F32)],
        compiler_params=_params("parallel", "arbitrary"),
    )(qt, k, vt, kmean)


def _row(a):
    return a.reshape(1, -1).astype(F32)


def _rwkv_layer(x, p, ln_g, ln_b, hs, ffn, tm):
    r, lw, k, v, kk, b, g, bonus = _rwkv_proj(x, p, hs, tm)
    t = x.shape[0]
    y = _wkv_scan(r, lw, k, v, kk, b, tb_local=min(WKV_LOCAL_TILE, t), tb_state=min(WKV_STATE_TILE, t))
    return _rwkv_out_ffn(y, g, bonus, x, p['gn_g'], p['gn_b'], p['w_o'], ln_g, ln_b, hs, ffn, tm)


def _rope_tables(t):
    inv = ROPE_THETA ** (-jnp.arange(0, HEAD_DIM, 2, dtype=F32) / HEAD_DIM)
    ang = jnp.arange(t, dtype=F32)[:, None] * inv[None, :]
    cos, sin = jnp.cos(ang), jnp.sin(ang)
    reps = LANES // HEAD_DIM
    cos_k = jnp.tile(jnp.concatenate([cos, cos], axis=1), (1, reps))
    sin_k = jnp.tile(jnp.concatenate([-sin, sin], axis=1), (1, reps))
    return cos.T, sin.T, cos_k, sin_k


def _moba_layer(x, w_qkv, w_o, ln_g, ln_b, ffn, tm):
    t, c = x.shape
    nh = c // HEAD_DIM
    wq_t = w_qkv[:, :c].T.astype(BF16)
    wk = w_qkv[:, c:2 * c].astype(BF16)
    wv_t = w_qkv[:, 2 * c:].T.astype(BF16)
    cos_t, sin_t, cos_k, sin_k = _rope_tables(t)
    qt, k, vt, kmean = _qkv_rope(x, wq_t, wk, wv_t, cos_t, sin_t, cos_k, sin_k, tm)
    kmean = jnp.transpose(kmean.reshape(t // MOBA_BLOCK, nh, HEAD_DIM), (1, 0, 2))
    at = _moba_attention(qt, k, vt, kmean, hb=min(MOBA_HEADS_PER_STEP, nh))
    return _attn_out_ffn(at.reshape(c, t), x, w_o.astype(BF16), ln_g, ln_b, ffn, tm)


def kernel(x, rwkv_mu, rwkv_w_rkv, rwkv_w0, rwkv_w1, rwkv_w2, rwkv_a0, rwkv_a1, rwkv_a2, rwkv_g1, rwkv_g2,
           rwkv_k_k, rwkv_k_a, rwkv_r_k, rwkv_gn_g, rwkv_gn_b, rwkv_w_o, moba_w_qkv, moba_w_o,
           ffn_w_in, ffn_w_down, ln_mix_g, ln_mix_b, ln_ffn_g, ln_ffn_b):
    bsz, t, c = x.shape
    assert c % WKV_GROUP == 0 and t % MOBA_BLOCK == 0
    assert ffn_w_in.shape[0] == DEPTH, "ALPHA is the DeepNorm constant of a DEPTH-layer trunk"
    tm = min(ROW_TILE, t)
    hs = _head_ones()
    outs = []
    for bi in range(bsz):
        h = x[bi]
        for i in range(DEPTH):
            j = i // 2
            ffn = (ffn_w_in[i].astype(BF16), ffn_w_down[i].astype(BF16), _row(ln_ffn_g[i]), _row(ln_ffn_b[i]))
            if i % 2 == 0:
                p = dict(mu=rwkv_mu[j], w_rkv=rwkv_w_rkv[j].astype(BF16), w0=_row(rwkv_w0[j]),
                         w1=rwkv_w1[j].astype(BF16), w2=rwkv_w2[j].astype(BF16), a0=_row(rwkv_a0[j]),
                         a1=rwkv_a1[j].astype(BF16), a2=rwkv_a2[j].astype(BF16),
                         g1=rwkv_g1[j].astype(BF16), g2=rwkv_g2[j].astype(BF16),
                         k_k=_row(rwkv_k_k[j]), k_a=_row(rwkv_k_a[j]), r_k=_row(rwkv_r_k[j]),
                         gn_g=_row(rwkv_gn_g[j]), gn_b=_row(rwkv_gn_b[j]), w_o=rwkv_w_o[j].astype(BF16))
                h = _rwkv_layer(h, p, _row(ln_mix_g[i]), _row(ln_mix_b[i]), hs, ffn, tm)
            else:
                h = _moba_layer(h, moba_w_qkv[j], moba_w_o[j], _row(ln_mix_g[i]), _row(ln_mix_b[i]), ffn, tm)
        outs.append(h)
    return jnp.stack(outs, axis=0)
```
